```python
import math
import jax
import jax.numpy as jnp
from jax import lax
import numpy as np

D_MODEL = 1024
BATCH = 1
SEQ = 16384
DEPTH = 1
DEC_BATCH = 2
DEC_SEQ = 8192
PAST_LEN = 128

HEAD_DIM = 64
RET_HEADS = 8
RET_DK = HEAD_DIM
RET_DV = HEAD_DIM
RET_WIDTH = RET_HEADS * RET_DV
RET_CHUNK = 128
ROPE_BASE = 10000.0
DIFF_HEADS = 4
DIFF_DK = HEAD_DIM
DIFF_DV = 2 * HEAD_DIM
DIFF_WIDTH = DIFF_HEADS * DIFF_DV
Q_BLOCK = 128
NUM_BUCKETS = 32
MAX_DISTANCE = 128
MIX_WIDTH = RET_WIDTH + DIFF_WIDTH
IN_SPLITS = (RET_HEADS * RET_DK, RET_HEADS * RET_DK, RET_WIDTH, RET_WIDTH,
             DIFF_HEADS * 2 * DIFF_DK, DIFF_HEADS * 2 * DIFF_DK, DIFF_WIDTH)
IN_WIDTH = sum(IN_SPLITS)
N_EXPERTS = 32
TOP_K = 4
D_FF = D_MODEL
SWIGLU_LIMIT = 7.0
SWIGLU_ALPHA = 1.702
MOE_BLOCK = 128
EPS = 1e-6

kernel_name = 'hybrid_retention_diffattn_moe_encoder'


def rms_norm(x, w):
    xf = x.astype(jnp.float32)
    y = xf * lax.rsqrt(jnp.mean(xf * xf, axis=-1, keepdims=True) + EPS)
    return (y * w.astype(jnp.float32)).astype(x.dtype)


def rotary(x, pos):
    d = x.shape[-1]
    inv_freq = ROPE_BASE ** (-jnp.arange(0, d, 2, dtype=jnp.float32) / d)
    ang = pos[:, None] * inv_freq[None, :]
    cos = jnp.cos(ang)[None, :, None, :]
    sin = jnp.sin(ang)[None, :, None, :]
    xf = x.astype(jnp.float32)
    x1, x2 = xf[..., : d // 2], xf[..., d // 2:]
    return jnp.concatenate([x1 * cos - x2 * sin, x2 * cos + x1 * sin], axis=-1).astype(x.dtype)


def t5_bucket(rel):
    nb = NUM_BUCKETS // 2
    max_exact = nb // 2
    n = jnp.abs(rel)
    base = jnp.where(rel > 0, nb, 0)
    nf = jnp.maximum(n, 1).astype(jnp.float32)
    large = max_exact + (jnp.log(nf / max_exact) / math.log(MAX_DISTANCE / max_exact)
                         * (nb - max_exact)).astype(jnp.int32)
    large = jnp.minimum(large, nb - 1)
    return base + jnp.where(n < max_exact, n, large)


def retention_one_direction(q, k, v, log_gamma, include_diag):
    B, S, H, dk = q.shape
    dv = v.shape[-1]
    n = S // RET_CHUNK
    qc = q.astype(jnp.float32).reshape(B, n, RET_CHUNK, H, dk)
    kc = k.astype(jnp.float32).reshape(B, n, RET_CHUNK, H, dk)
    vc = v.astype(jnp.float32).reshape(B, n, RET_CHUNK, H, dv)
    pos = jnp.arange(RET_CHUNK, dtype=jnp.float32)
    diff = pos[:, None] - pos[None, :]
    mask = (diff >= 0) if include_diag else (diff > 0)
    decay = jnp.where(mask[None], jnp.exp(jnp.maximum(diff, 0.0)[None] * log_gamma[:, None, None]), 0.0)
    scores = jnp.einsum('bnqhd,bnkhd->bnhqk', qc, kc) * decay[None, None]
    intra = jnp.einsum('bnhqk,bnkhe->bnqhe', scores, vc)
    k_decay = jnp.exp((RET_CHUNK - 1 - pos)[None, :] * log_gamma[:, None])
    q_decay = jnp.exp((pos + 1.0)[None, :] * log_gamma[:, None])
    chunk_decay = jnp.exp(RET_CHUNK * log_gamma)
    kv = jnp.einsum('bnkhd,hk,bnkhe->nbhde', kc, k_decay, vc)

    def step(state, kv_c):
        return state * chunk_decay[None, :, None, None] + kv_c, state

    _, prev = lax.scan(step, jnp.zeros((B, H, dk, dv), jnp.float32), kv)
    cross = jnp.einsum('bnqhd,hq,nbhde->bnqhe', qc, q_decay, prev)
    return (intra + cross).reshape(B, S, H, dv)


def bidirectional_retention(q, k, v, a_fwd, a_bwd):
    lg_f = jnp.log1p(-jnp.exp(a_fwd.astype(jnp.float32)))
    lg_b = jnp.log1p(-jnp.exp(a_bwd.astype(jnp.float32)))
    fwd = retention_one_direction(q, k, v, lg_f, True)
    bwd = jnp.flip(retention_one_direction(jnp.flip(q, axis=1), jnp.flip(k, axis=1),
                                           jnp.flip(v, axis=1), lg_b, False), axis=1)
    return fwd + bwd


def differential_attention(q, k, v, rel_bias, lam):
    B, S, H, _, dk = q.shape
    dv = v.shape[-1]
    scale = dk ** -0.5
    n_blocks = S // Q_BLOCK
    k_pos = jnp.arange(S, dtype=jnp.int32)
    table = rel_bias.astype(jnp.float32)

    def one_block(i):
        start = i * Q_BLOCK
        qb = lax.dynamic_slice_in_dim(q, start, Q_BLOCK, axis=1)
        logits = jnp.einsum('bqhtd,bkhtd->bhtqk', qb, k).astype(jnp.float32) * scale
        q_pos = start + jnp.arange(Q_BLOCK, dtype=jnp.int32)
        bucket = t5_bucket(k_pos[None, :] - q_pos[:, None])
        bias = jnp.transpose(table[bucket], (2, 0, 1))
        probs = jax.nn.softmax(logits + bias[None, :, None], axis=-1)
        weights = probs[:, :, 0] - lam * probs[:, :, 1]
        return jnp.einsum('bhqk,bkhd->bqhd', weights.astype(v.dtype), v)

    out = lax.map(one_block, jnp.arange(n_blocks))
    return jnp.transpose(out, (1, 0, 2, 3, 4)).reshape(B, S, H, dv)


def token_mixer(h, w_in, w_out, ret_a_fwd, ret_a_bwd, ret_norm_w, diff_q_norm_w, diff_k_norm_w,
                lam_q1, lam_k1, lam_q2, lam_k2, diff_subln_w, rel_bias, lam_init):
    B, S, _ = h.shape
    proj = h @ w_in
    split_points = np.cumsum(IN_SPLITS)[:-1].tolist()
    rq, rk, rv, rg, dq, dk, dv = jnp.split(proj, split_points, axis=-1)
    pos = jnp.arange(S, dtype=jnp.float32)
    rq = rotary(rq.reshape(B, S, RET_HEADS, RET_DK), pos)
    rk = rotary(rk.reshape(B, S, RET_HEADS, RET_DK), pos) * (RET_DK ** -0.5)
    rv = rv.reshape(B, S, RET_HEADS, RET_DV)
    ret = bidirectional_retention(rq, rk, rv, ret_a_fwd, ret_a_bwd)
    ret = rms_norm(ret, ret_norm_w).reshape(B, S, RET_WIDTH).astype(h.dtype)
    ret_out = jax.nn.silu(rg) * ret
    dq = rms_norm(dq.reshape(B, S, DIFF_HEADS, 2, DIFF_DK), diff_q_norm_w)
    dk = rms_norm(dk.reshape(B, S, DIFF_HEADS, 2, DIFF_DK), diff_k_norm_w)
    dv = dv.reshape(B, S, DIFF_HEADS, DIFF_DV)
    lam = (jnp.exp(jnp.sum(lam_q1.astype(jnp.float32) * lam_k1.astype(jnp.float32)))
           - jnp.exp(jnp.sum(lam_q2.astype(jnp.float32) * lam_k2.astype(jnp.float32))) + lam_init)
    att = differential_attention(dq, dk, dv, rel_bias, lam)
    diff_out = (rms_norm(att, diff_subln_w) * (1.0 - lam_init)).reshape(B, S, DIFF_WIDTH).astype(h.dtype)
    return jnp.concatenate([ret_out, diff_out], axis=-1) @ w_out


def moe_ffn(h, router_w, router_b, w_gate_up, b_gate_up, w_down, b_down):
    B, S, D = h.shape
    T = B * S
    tokens = h.reshape(T, D)
    logits = (tokens @ router_w).astype(jnp.float32) + router_b.astype(jnp.float32)
    top_logits, top_idx = lax.top_k(logits, TOP_K)
    gates = jax.nn.softmax(top_logits, axis=-1)
    flat_expert = top_idx.reshape(-1).astype(jnp.int32)
    flat_token = jnp.repeat(jnp.arange(T, dtype=jnp.int32), TOP_K)
    flat_gate = gates.reshape(-1)
    order = jnp.argsort(flat_expert)
    sorted_expert = flat_expert[order]
    counts = jnp.bincount(flat_expert, length=N_EXPERTS).astype(jnp.int32)
    padded = (counts + MOE_BLOCK - 1) // MOE_BLOCK * MOE_BLOCK
    ends_pad = jnp.cumsum(padded)
    start_pad = ends_pad - padded
    start = jnp.cumsum(counts) - counts
    rank = jnp.arange(T * TOP_K, dtype=jnp.int32) - start[sorted_expert]
    dest = start_pad[sorted_expert] + rank
    n_rows = (T * TOP_K + N_EXPERTS * (MOE_BLOCK - 1) + MOE_BLOCK - 1) // MOE_BLOCK * MOE_BLOCK
    n_blocks = n_rows // MOE_BLOCK
    row_token = jnp.zeros((n_rows,), jnp.int32).at[dest].set(flat_token[order])
    row_gate = jnp.zeros((n_rows,), jnp.float32).at[dest].set(flat_gate[order])
    block_start = jnp.arange(n_blocks, dtype=jnp.int32) * MOE_BLOCK
    block_expert = jnp.minimum(jnp.searchsorted(ends_pad, block_start, side='right'), N_EXPERTS - 1)

    def expert_block(args):
        e, toks = args
        xb = tokens[toks]
        gu = (xb @ w_gate_up[e] + b_gate_up[e]).astype(jnp.float32)
        glu = jnp.minimum(gu[:, :D_FF], SWIGLU_LIMIT)
        lin = jnp.clip(gu[:, D_FF:], -SWIGLU_LIMIT, SWIGLU_LIMIT)
        act = glu * jax.nn.sigmoid(SWIGLU_ALPHA * glu) * (lin + 1.0)
        return (act.astype(h.dtype) @ w_down[e] + b_down[e]).astype(jnp.float32)

    ys = lax.map(expert_block, (block_expert, row_token.reshape(n_blocks, MOE_BLOCK)))
    out = jnp.zeros((T, D), jnp.float32).at[row_token].add(ys.reshape(n_rows, D) * row_gate[:, None])
    return out.reshape(B, S, D).astype(h.dtype)


def encoder_layer(x, c, w_ada, b_ada, norm1_w, w_in, ret_a_fwd, ret_a_bwd, ret_norm_w,
                  diff_q_norm_w, diff_k_norm_w, lam_q1, lam_k1, lam_q2, lam_k2, diff_subln_w,
                  rel_bias, w_out, norm2_w, router_w, router_b, w_gate_up, b_gate_up,
                  w_down, b_down, lam_init):
    mod = jax.nn.silu(c) @ w_ada + b_ada
    shift1, scale1, gate1, shift2, scale2, gate2 = jnp.split(mod[:, None, :], 6, axis=-1)
    h = rms_norm(x, norm1_w) * (1.0 + scale1) + shift1
    x = x + gate1 * token_mixer(h, w_in, w_out, ret_a_fwd, ret_a_bwd, ret_norm_w, diff_q_norm_w,
                                diff_k_norm_w, lam_q1, lam_k1, lam_q2, lam_k2, diff_subln_w,
                                rel_bias, lam_init)
    h = rms_norm(x, norm2_w) * (1.0 + scale2) + shift2
    x = x + gate2 * moe_ffn(h, router_w, router_b, w_gate_up, b_gate_up, w_down, b_down)
    return x


def trunk(x, c, w_ada, b_ada, norm1_w, w_in, ret_a_fwd, ret_a_bwd, ret_norm_w, diff_q_norm_w,
          diff_k_norm_w, lam_q1, lam_k1, lam_q2, lam_k2, diff_subln_w, rel_bias, w_out, norm2_w,
          router_w, router_b, w_gate_up, b_gate_up, w_down, b_down):
    for l in range(DEPTH):
        lam_init = 0.8 - 0.6 * math.exp(-0.3 * l)
        x = encoder_layer(x, c, w_ada[l], b_ada[l], norm1_w[l], w_in[l], ret_a_fwd[l], ret_a_bwd[l],
                          ret_norm_w[l], diff_q_norm_w[l], diff_k_norm_w[l], lam_q1[l], lam_k1[l],
                          lam_q2[l], lam_k2[l], diff_subln_w[l], rel_bias, w_out[l], norm2_w[l],
                          router_w[l], router_b[l], w_gate_up[l], b_gate_up[l], w_down[l],
                          b_down[l], lam_init)
    return x


def setup_inputs(seed: int = 0) -> dict:
    key = jax.random.key(seed)
    ks = jax.random.split(key, 32)
    f32 = jnp.float32

    def nrm(k, shape, s):
        return s * jax.random.normal(k, shape, f32)

    base_a = jnp.linspace(math.log(1.0 / 32.0), math.log(1.0 / 512.0), RET_HEADS, dtype=f32)
    return {
        'x_prompt': nrm(ks[0], (BATCH, SEQ, D_MODEL), 1.0),
        'x_sample': nrm(ks[1], (DEC_BATCH, DEC_SEQ, D_MODEL), 1.0),
        'c_prompt': nrm(ks[2], (BATCH, D_MODEL), 1.0),
        'c_sample': nrm(ks[3], (DEC_BATCH, D_MODEL), 1.0),
        'w_ada': nrm(ks[4], (DEPTH, D_MODEL, 6 * D_MODEL), 0.5 * D_MODEL ** -0.5),
        'b_ada': nrm(ks[5], (DEPTH, 6 * D_MODEL), 0.02),
        'norm1_w': 1.0 + nrm(ks[6], (DEPTH, D_MODEL), 0.05),
        'w_in': nrm(ks[7], (DEPTH, D_MODEL, IN_WIDTH), D_MODEL ** -0.5),
        'ret_a_fwd': base_a[None, :] + nrm(ks[8], (DEPTH, RET_HEADS), 0.05),
        'ret_a_bwd': base_a[None, :] + nrm(ks[9], (DEPTH, RET_HEADS), 0.05),
        'ret_norm_w': 1.0 + nrm(ks[10], (DEPTH, RET_HEADS, RET_DV), 0.05),
        'diff_q_norm_w': 1.0 + nrm(ks[11], (DEPTH, DIFF_DK), 0.05),
        'diff_k_norm_w': 1.0 + nrm(ks[12], (DEPTH, DIFF_DK), 0.05),
        'lam_q1': nrm(ks[13], (DEPTH, DIFF_DK), 0.1),
        'lam_k1': nrm(ks[14], (DEPTH, DIFF_DK), 0.1),
        'lam_q2': nrm(ks[15], (DEPTH, DIFF_DK), 0.1),
        'lam_k2': nrm(ks[16], (DEPTH, DIFF_DK), 0.1),
        'diff_subln_w': 1.0 + nrm(ks[17], (DEPTH, DIFF_DV), 0.05),
        'rel_bias': nrm(ks[18], (NUM_BUCKETS, DIFF_HEADS), 0.2),
        'w_out': nrm(ks[19], (DEPTH, MIX_WIDTH, D_MODEL), MIX_WIDTH ** -0.5),
        'norm2_w': 1.0 + nrm(ks[20], (DEPTH, D_MODEL), 0.05),
        'router_w': nrm(ks[21], (DEPTH, D_MODEL, N_EXPERTS), D_MODEL ** -0.5),
        'router_b': nrm(ks[22], (DEPTH, N_EXPERTS), 0.01),
        'w_gate_up': nrm(ks[23], (DEPTH, N_EXPERTS, D_MODEL, 2 * D_FF), D_MODEL ** -0.5),
        'b_gate_up': nrm(ks[24], (DEPTH, N_EXPERTS, 2 * D_FF), 0.01),
        'w_down': nrm(ks[25], (DEPTH, N_EXPERTS, D_FF, D_MODEL), D_FF ** -0.5),
        'b_down': nrm(ks[26], (DEPTH, N_EXPERTS, D_MODEL), 0.01),
    }


def reference(x_prompt, x_sample, c_prompt, c_sample, w_ada, b_ada, norm1_w, w_in, ret_a_fwd,
              ret_a_bwd, ret_norm_w, diff_q_norm_w, diff_k_norm_w, lam_q1, lam_k1, lam_q2, lam_k2,
              diff_subln_w, rel_bias, w_out, norm2_w, router_w, router_b, w_gate_up, b_gate_up,
              w_down, b_down):
    y_prompt = trunk(x_prompt, c_prompt, w_ada, b_ada, norm1_w, w_in, ret_a_fwd, ret_a_bwd,
                     ret_norm_w, diff_q_norm_w, diff_k_norm_w, lam_q1, lam_k1, lam_q2, lam_k2,
                     diff_subln_w, rel_bias, w_out, norm2_w, router_w, router_b, w_gate_up,
                     b_gate_up, w_down, b_down)
    y_sample = trunk(x_sample, c_sample, w_ada, b_ada, norm1_w, w_in, ret_a_fwd, ret_a_bwd,
                     ret_norm_w, diff_q_norm_w, diff_k_norm_w, lam_q1, lam_k1, lam_q2, lam_k2,
                     diff_subln_w, rel_bias, w_out, norm2_w, router_w, router_b, w_gate_up,
                     b_gate_up, w_down, b_down)
    return (y_prompt, y_sample)
```

```python
import functools
import math

import jax
import jax.numpy as jnp
import numpy as np
from jax import lax
from jax.experimental import pallas as pl
from jax.experimental.pallas import tpu as pltpu

F32 = jnp.float32
BF16 = jnp.bfloat16
HIGHEST = lax.Precision.HIGHEST

LANES = 128
SUBLANES = 8
HEAD_DIM = 64
RET_HEADS = 8
DIFF_HEADS = 4
HALF_WIDTH = 512
N_PAIRS = HALF_WIDTH // LANES
ROPE_BASE = 10000.0
NUM_BUCKETS = 32
MAX_DISTANCE = 128
N_EXPERTS = 32
TOP_K = 4
SWIGLU_LIMIT = 7.0
SWIGLU_ALPHA = 1.702
EPS = 1e-6
VMEM_LIMIT = 56 * 1024 * 1024


def _cparams(sem):
    return pltpu.CompilerParams(dimension_semantics=sem, vmem_limit_bytes=VMEM_LIMIT)


def _ada_kernel(c_ref, w_ref, b_ref, o_ref):
    c = c_ref[...]
    a = c * jax.nn.sigmoid(c)
    o_ref[...] = jnp.dot(a, w_ref[...], preferred_element_type=F32, precision=HIGHEST) + b_ref[...]


def _ada(c_pad, w_ada, b_ada):
    rows, d = c_pad.shape
    n = w_ada.shape[1]
    tn = d
    return pl.pallas_call(
        _ada_kernel,
        grid=(n // tn,),
        in_specs=[pl.BlockSpec((rows, d), lambda j: (0, 0)),
                  pl.BlockSpec((d, tn), lambda j: (0, j)),
                  pl.BlockSpec((1, tn), lambda j: (0, j))],
        out_specs=pl.BlockSpec((rows, tn), lambda j: (0, j)),
        out_shape=jax.ShapeDtypeStruct((rows, n), F32),
        compiler_params=_cparams(("parallel",)),
    )(c_pad, w_ada, b_ada.reshape(1, n))


def _rotate_half(xg):
    lane = lax.broadcasted_iota(jnp.int32, xg.shape, 1)
    first = (lane % HEAD_DIM) < (HEAD_DIM // 2)
    return jnp.where(first, pltpu.roll(xg, LANES - HEAD_DIM // 2, 1), pltpu.roll(xg, HEAD_DIM // 2, 1))


def _inproj_kernel(x_ref, sc_ref, sh_ref, nw_ref, w_ref, cos_ref, sin_ref, qnw_ref, knw_ref, bd_ref,
                   rq_ref, rk_ref, rv_ref, rg_ref, dq_ref, dk_ref, dv_ref):
    x = x_ref[...]
    ms = jnp.mean(x * x, axis=-1, keepdims=True)
    h = x * lax.rsqrt(ms + EPS) * nw_ref[...]
    h = (h * (1.0 + sc_ref[...]) + sh_ref[...]).astype(BF16)

    def piece(n):
        return jnp.dot(h, w_ref[:, n * HALF_WIDTH:(n + 1) * HALF_WIDTH], preferred_element_type=F32)

    cos = cos_ref[...]
    sin = sin_ref[...]

    def rotary(p, out_ref, scale):
        for g in range(N_PAIRS):
            xg = p[:, g * LANES:(g + 1) * LANES]
            y = xg * cos + _rotate_half(xg) * sin
            out_ref[:, g * LANES:(g + 1) * LANES] = (y * scale).astype(BF16)

    def head_norm(p, w, out_ref, scale):
        msq = jnp.dot((p * p).astype(BF16), bd_ref[...], preferred_element_type=F32)
        out_ref[...] = (p * lax.rsqrt(msq + EPS) * w * scale).astype(BF16)

    rotary(piece(0), rq_ref, 1.0)
    rotary(piece(1), rk_ref, HEAD_DIM ** -0.5)
    rv_ref[...] = piece(2).astype(BF16)
    rg_ref[...] = piece(3).astype(BF16)
    head_norm(piece(4), qnw_ref[...], dq_ref, HEAD_DIM ** -0.5)
    head_norm(piece(5), knw_ref[...], dk_ref, 1.0)
    dv_ref[...] = piece(6).astype(BF16)


def _inproj(x2, scale1, shift1, norm_w, w_in, cos_t, sin_t, qnw, knw, bd, seq_len, tt):
    t_total, d = x2.shape
    nps = seq_len // tt
    n_in = w_in.shape[1]
    row = lambda i: (i, 0)
    mod = lambda i: (i // nps, 0, 0)
    pos = lambda i: (i % nps, 0)
    fixed = lambda i: (0, 0)
    out_sd = jax.ShapeDtypeStruct((t_total, HALF_WIDTH), BF16)
    return pl.pallas_call(
        _inproj_kernel,
        grid=(t_total // tt,),
        in_specs=[pl.BlockSpec((tt, d), row),
                  pl.BlockSpec((None, 1, d), mod),
                  pl.BlockSpec((None, 1, d), mod),
                  pl.BlockSpec((1, d), fixed),
                  pl.BlockSpec((d, n_in), fixed),
                  pl.BlockSpec((tt, LANES), pos),
                  pl.BlockSpec((tt, LANES), pos),
                  pl.BlockSpec((1, HALF_WIDTH), fixed),
                  pl.BlockSpec((1, HALF_WIDTH), fixed),
                  pl.BlockSpec((HALF_WIDTH, HALF_WIDTH), fixed)],
        out_specs=[pl.BlockSpec((tt, HALF_WIDTH), row)] * 7,
        out_shape=[out_sd] * 7,
        compiler_params=_cparams(("parallel",)),
    )(x2, scale1, shift1, norm_w, w_in, cos_t, sin_t, qnw, knw, bd)


def _kv_update(state_ref, p, k, v, kdec, cdec, bdmask):
    kd = (k.astype(F32) * kdec).astype(BF16)
    kv = lax.dot_general(kd, v, (((0,), (0,)), ((), ())), preferred_element_type=F32)
    state_ref[p] = state_ref[p] * cdec + kv * bdmask


def _ret_state_kernel(k_ref, v_ref, kdec_ref, cdec_ref, bdmask_ref, sb_ref, state_ref):
    @pl.when(pl.program_id(1) == 0)
    def _():
        state_ref[...] = jnp.zeros_like(state_ref)

    sb_ref[...] = state_ref[...]
    for p in range(N_PAIRS):
        sl = slice(p * LANES, (p + 1) * LANES)
        _kv_update(state_ref, p, k_ref[:, sl], v_ref[:, sl], kdec_ref[:, sl], cdec_ref[:, sl], bdmask_ref[...])


def _ret_main_kernel(q_ref, k_ref, v_ref, g_ref, dmat_ref, qdf_ref, qdb_ref, kdf_ref, cdf_ref, bdmask_ref,
                     nw_ref, sb_ref, o_ref, state_ref):
    @pl.when(pl.program_id(1) == 0)
    def _():
        state_ref[...] = jnp.zeros_like(state_ref)

    c = q_ref.shape[0]
    lane = lax.broadcasted_iota(jnp.int32, (c, LANES), 1)
    lo = lane < HEAD_DIM
    for p in range(N_PAIRS):
        sl = slice(p * LANES, (p + 1) * LANES)
        q = q_ref[:, sl]
        k = k_ref[:, sl]
        v = v_ref[:, sl]
        qf = q.astype(F32)
        acc = jnp.dot((qf * qdf_ref[:, sl]).astype(BF16), state_ref[p].astype(BF16), preferred_element_type=F32)
        acc += jnp.dot((qf * qdb_ref[:, sl]).astype(BF16), sb_ref[p].astype(BF16), preferred_element_type=F32)
        for hh in range(2):
            sel = lo if hh == 0 else jnp.logical_not(lo)
            qm = jnp.where(sel, q, jnp.zeros_like(q))
            vm = jnp.where(sel, v, jnp.zeros_like(v))
            s = lax.dot_general(qm, k, (((1,), (1,)), ((), ())), preferred_element_type=F32)
            w = (s * dmat_ref[2 * p + hh]).astype(BF16)
            acc += jnp.dot(w, vm, preferred_element_type=F32)
        _kv_update(state_ref, p, k, v, kdf_ref[:, sl], cdf_ref[:, sl], bdmask_ref[...])
        sq = acc * acc
        ms_lo = jnp.sum(jnp.where(lo, sq, 0.0), axis=-1, keepdims=True)
        ms_hi = jnp.sum(jnp.where(lo, 0.0, sq), axis=-1, keepdims=True)
        ms = jnp.where(lo, ms_lo, ms_hi) * (1.0 / HEAD_DIM)
        y = acc * lax.rsqrt(ms + EPS) * nw_ref[:, sl]
        gf = g_ref[:, sl].astype(F32)
        o_ref[:, sl] = (gf * jax.nn.sigmoid(gf) * y).astype(BF16)


def _retention(rq, rk, rv, rg, tabs, ret_nw, batch, seq_len, c):
    t_total = rq.shape[0]
    nc = seq_len // c
    dmat, qdf, qdb, kdf, kdb, cdf, cdb, bdmask = tabs
    fixed2 = lambda b, i: (0, 0)
    rev = lambda b, i: (b * nc + nc - 1 - i, 0)
    fwd = lambda b, i: (b * nc + i, 0)
    tile = pl.BlockSpec((c, HALF_WIDTH), fwd)
    tile_rev = pl.BlockSpec((c, HALF_WIDTH), rev)
    tab = pl.BlockSpec((c, HALF_WIDTH), fixed2)
    vec = pl.BlockSpec((1, HALF_WIDTH), fixed2)
    mask = pl.BlockSpec((LANES, LANES), fixed2)
    state = pltpu.VMEM((N_PAIRS, LANES, LANES), F32)
    sb = pl.pallas_call(
        _ret_state_kernel,
        grid=(batch, nc),
        in_specs=[tile_rev, tile_rev, tab, vec, mask],
        out_specs=pl.BlockSpec((None, None, N_PAIRS, LANES, LANES), lambda b, i: (b, nc - 1 - i, 0, 0, 0)),
        out_shape=jax.ShapeDtypeStruct((batch, nc, N_PAIRS, LANES, LANES), F32),
        scratch_shapes=[state],
        compiler_params=_cparams(("parallel", "arbitrary")),
    )(rk, rv, kdb, cdb, bdmask)
    return pl.pallas_call(
        _ret_main_kernel,
        grid=(batch, nc),
        in_specs=[tile, tile, tile, tile,
                  pl.BlockSpec((RET_HEADS, c, c), lambda b, i: (0, 0, 0)),
                  tab, tab, tab, vec, mask, vec,
                  pl.BlockSpec((None, None, N_PAIRS, LANES, LANES), lambda b, i: (b, i, 0, 0, 0))],
        out_specs=tile,
        out_shape=jax.ShapeDtypeStruct((t_total, HALF_WIDTH), BF16),
        scratch_shapes=[state],
        compiler_params=_cparams(("parallel", "arbitrary")),
    )(rq, rk, rv, rg, dmat, qdf, qdb, kdf, cdf, bdmask, ret_nw, sb)


def _retention_tables(a_fwd, a_bwd, c):
    lg_f = jnp.log1p(-jnp.exp(a_fwd.astype(F32)))
    lg_b = jnp.log1p(-jnp.exp(a_bwd.astype(F32)))
    pos = jnp.arange(c, dtype=F32)
    diff = pos[:, None] - pos[None, :]
    dmat = jnp.where(diff[None] >= 0,
                     jnp.exp(jnp.maximum(diff, 0.0)[None] * lg_f[:, None, None]),
                     jnp.exp(jnp.maximum(-diff, 0.0)[None] * lg_b[:, None, None]))
    lane_f = jnp.repeat(lg_f, HEAD_DIM)[None, :]
    lane_b = jnp.repeat(lg_b, HEAD_DIM)[None, :]
    qdf = jnp.exp((pos + 1.0)[:, None] * lane_f)
    kdf = jnp.exp((c - 1.0 - pos)[:, None] * lane_f)
    qdb = jnp.exp((c - pos)[:, None] * lane_b)
    kdb = jnp.exp(pos[:, None] * lane_b)
    cdf = jnp.exp(c * lane_f)
    cdb = jnp.exp(c * lane_b)
    r = jnp.arange(LANES)
    bdmask = ((r[:, None] // HEAD_DIM) == (r[None, :] // HEAD_DIM)).astype(F32)
    return dmat, qdf, qdb, kdf, kdb, cdf, cdb, bdmask


def _attn_kernel(sc_ref, q_ref, k_ref, v_ref, bias_ref, sw_ref, o_ref, m_ref, l_ref, acc_ref):
    h = pl.program_id(1)
    i = pl.program_id(2)
    t = q_ref.shape[0]
    nk = k_ref.shape[0] // t
    q = q_ref[...]
    lane = lax.broadcasted_iota(jnp.int32, q.shape, 1)
    zero = jnp.zeros_like(q)
    qq = jnp.concatenate([jnp.where(lane < HEAD_DIM, q, zero), jnp.where(lane < HEAD_DIM, zero, q)], axis=0)
    m_ref[...] = jnp.full_like(m_ref, -jnp.inf)
    l_ref[...] = jnp.zeros_like(l_ref)
    acc_ref[...] = jnp.zeros_like(acc_ref)

    def tile(j, bias):
        start = pl.multiple_of(j * t, t)
        kj = k_ref[pl.ds(start, t), :]
        vj = v_ref[pl.ds(start, t), :]
        s = lax.dot_general(qq, kj, (((1,), (1,)), ((), ())), preferred_element_type=F32) + bias
        m_old = m_ref[...]
        m_new = jnp.maximum(m_old, jnp.max(s, axis=-1, keepdims=True))
        alpha = jnp.exp(m_old - m_new)
        p = jnp.exp(s - m_new)
        l_ref[...] = alpha * l_ref[...] + jnp.sum(p, axis=-1, keepdims=True)
        acc_ref[...] = alpha * acc_ref[...] + jnp.dot(p.astype(BF16), vj, preferred_element_type=F32)
        m_ref[...] = m_new

    def far(c):
        def body(j, carry):
            tile(j, c)
            return carry
        return body

    lax.fori_loop(0, i - 1, far(sc_ref[1 + 2 * h]), 0)
    for d in (-1, 0, 1):
        j = i + d

        @pl.when(jnp.logical_and(j >= 0, j < nk))
        def _():
            b = bias_ref[d + 1]
            tile(j, jnp.concatenate([b, b], axis=0))

    lax.fori_loop(i + 2, nk, far(sc_ref[2 + 2 * h]), 0)

    o = acc_ref[...] / l_ref[...]
    att = o[:t] - sc_ref[0] * o[t:]
    msq = jnp.mean(att * att, axis=-1, keepdims=True)
    o_ref[...] = (att * lax.rsqrt(msq + EPS) * sw_ref[...]).astype(BF16)


def _attention(dq, dk, dv, bias_tiles, scalars, subln_w, batch, seq_len, t):
    t_total = dq.shape[0]
    nq = seq_len // t
    grid_spec = pltpu.PrefetchScalarGridSpec(
        num_scalar_prefetch=1,
        grid=(batch, DIFF_HEADS, nq),
        in_specs=[pl.BlockSpec((t, LANES), lambda b, h, i, s: (b * nq + i, h)),
                  pl.BlockSpec((seq_len, LANES), lambda b, h, i, s: (b, h)),
                  pl.BlockSpec((seq_len, LANES), lambda b, h, i, s: (b, h)),
                  pl.BlockSpec((None, 3, t, t), lambda b, h, i, s: (h, 0, 0, 0)),
                  pl.BlockSpec((1, LANES), lambda b, h, i, s: (0, 0))],
        out_specs=pl.BlockSpec((t, LANES), lambda b, h, i, s: (b * nq + i, h)),
        scratch_shapes=[pltpu.VMEM((2 * t, 1), F32), pltpu.VMEM((2 * t, 1), F32),
                        pltpu.VMEM((2 * t, LANES), F32)],
    )
    return pl.pallas_call(
        _attn_kernel,
        grid_spec=grid_spec,
        out_shape=jax.ShapeDtypeStruct((t_total, HALF_WIDTH), BF16),
        compiler_params=_cparams(("parallel", "parallel", "parallel")),
    )(scalars, dq, dk, dv, bias_tiles, subln_w)


def _t5_bucket(rel):
    nb = NUM_BUCKETS // 2
    max_exact = nb // 2
    n = jnp.abs(rel)
    base = jnp.where(rel > 0, nb, 0)
    nf = jnp.maximum(n, 1).astype(F32)
    large = max_exact + (jnp.log(nf / max_exact) / math.log(MAX_DISTANCE / max_exact)
                         * (nb - max_exact)).astype(jnp.int32)
    large = jnp.minimum(large, nb - 1)
    return base + jnp.where(n < max_exact, n, large)


def _bias_tables(rel_bias, t):
    assert t >= MAX_DISTANCE
    table = rel_bias.astype(F32)
    qq = jnp.arange(t, dtype=jnp.int32)[:, None]
    kk = jnp.arange(t, dtype=jnp.int32)[None, :]
    tiles = [table[_t5_bucket(d * t + kk - qq)] for d in (-1, 0, 1)]
    tiles = jnp.transpose(jnp.stack(tiles, axis=0), (3, 0, 1, 2))
    far_left = table[_t5_bucket(jnp.int32(-2 * t))]
    far_right = table[_t5_bucket(jnp.int32(2 * t))]
    return tiles, far_left, far_right


def _outproj_kernel(ret_ref, dif_ref, x_ref, g1_ref, sc2_ref, sh2_ref, nw2_ref, wo_ref, rw_ref, rb_ref,
                    x1_ref, h2_ref, idx_ref, gate_ref):
    mix = jnp.dot(ret_ref[...], wo_ref[:HALF_WIDTH, :], preferred_element_type=F32)
    mix += jnp.dot(dif_ref[...], wo_ref[HALF_WIDTH:, :], preferred_element_type=F32)
    x1 = x_ref[...] + g1_ref[...] * mix
    x1_ref[...] = x1
    ms = jnp.mean(x1 * x1, axis=-1, keepdims=True)
    h2 = x1 * lax.rsqrt(ms + EPS) * nw2_ref[...]
    h2 = h2 * (1.0 + sc2_ref[...]) + sh2_ref[...]
    h2_ref[...] = h2
    logits = jnp.dot(h2, rw_ref[...], preferred_element_type=F32, precision=HIGHEST) + rb_ref[...]
    lane = lax.broadcasted_iota(jnp.int32, logits.shape, 1)
    vals, idxs = [], []
    for _ in range(TOP_K):
        m = jnp.max(logits, axis=-1, keepdims=True)
        am = jnp.min(jnp.where(logits == m, lane, LANES), axis=-1, keepdims=True)
        vals.append(m)
        idxs.append(am)
        logits = jnp.where(lane == am, -jnp.inf, logits)
    es = [jnp.exp(v - vals[0]) for v in vals]
    den = es[0] + es[1] + es[2] + es[3]
    idx_out = jnp.zeros(lane.shape, jnp.int32)
    gate_out = jnp.zeros(lane.shape, F32)
    for kk in range(TOP_K):
        idx_out = jnp.where(lane == kk, idxs[kk], idx_out)
        gate_out = jnp.where(lane == kk, es[kk] / den, gate_out)
    idx_ref[...] = idx_out
    gate_ref[...] = gate_out


def _outproj(ret_out, diff_out, x2, gate1, scale2, shift2, norm2_w, w_out, rw_pad, rb_pad, seq_len, tt):
    t_total, d = x2.shape
    nps = seq_len // tt
    row = lambda i: (i, 0)
    mod = lambda i: (i // nps, 0, 0)
    fixed = lambda i: (0, 0)
    return pl.pallas_call(
        _outproj_kernel,
        grid=(t_total // tt,),
        in_specs=[pl.BlockSpec((tt, HALF_WIDTH), row),
                  pl.BlockSpec((tt, HALF_WIDTH), row),
                  pl.BlockSpec((tt, d), row),
                  pl.BlockSpec((None, 1, d), mod),
                  pl.BlockSpec((None, 1, d), mod),
                  pl.BlockSpec((None, 1, d), mod),
                  pl.BlockSpec((1, d), fixed),
                  pl.BlockSpec((2 * HALF_WIDTH, d), fixed),
                  pl.BlockSpec((d, LANES), fixed),
                  pl.BlockSpec((1, LANES), fixed)],
        out_specs=[pl.BlockSpec((tt, d), row), pl.BlockSpec((tt, d), row),
                   pl.BlockSpec((tt, LANES), row), pl.BlockSpec((tt, LANES), row)],
        out_shape=[jax.ShapeDtypeStruct((t_total, d), F32), jax.ShapeDtypeStruct((t_total, d), F32),
                   jax.ShapeDtypeStruct((t_total, LANES), jnp.int32),
                   jax.ShapeDtypeStruct((t_total, LANES), F32)],
        compiler_params=_cparams(("parallel",)),
    )(ret_out, diff_out, x2, gate1, scale2, shift2, norm2_w, w_out, rw_pad, rb_pad)


def _expert_kernel(be_ref, nv_ref, src_ref, dst_ref, h2_hbm, wgu_ref, bgu_ref, wd_ref, bd_ref, out_hbm,
                   xbuf, ybuf, trash, sem_in, sem_out):
    i = pl.program_id(0)
    nv = nv_ref[i]
    rows = xbuf.shape[0]
    d_ff = wd_ref.shape[1]

    @pl.when(nv > 0)
    def _():
        def gather(r, carry):
            pltpu.make_async_copy(h2_hbm.at[pl.ds(src_ref[r], 1)], xbuf.at[pl.ds(r, 1)], sem_in).start()
            return carry

        lax.fori_loop(0, rows, gather, 0)
        pltpu.make_async_copy(h2_hbm.at[pl.ds(0, rows)], xbuf, sem_in).wait()
        x = xbuf[...].astype(BF16)
        gu = jnp.dot(x, wgu_ref[0], preferred_element_type=F32) + bgu_ref[0]
        glu = jnp.minimum(gu[:, :d_ff], SWIGLU_LIMIT)
        lin = jnp.clip(gu[:, d_ff:], -SWIGLU_LIMIT, SWIGLU_LIMIT)
        act = glu * jax.nn.sigmoid(SWIGLU_ALPHA * glu) * (lin + 1.0)
        ybuf[...] = jnp.dot(act.astype(BF16), wd_ref[0], preferred_element_type=F32) + bd_ref[0]

        def scatter(r, carry):
            pltpu.make_async_copy(ybuf.at[pl.ds(r, 1)], out_hbm.at[pl.ds(dst_ref[r], 1)], sem_out).start()
            return carry

        def discard(r, carry):
            pltpu.make_async_copy(ybuf.at[pl.ds(r, 1)], trash.at[pl.ds(r - nv, 1)], sem_out).start()
            return carry

        nv8 = pl.multiple_of((nv + SUBLANES - 1) // SUBLANES * SUBLANES, SUBLANES)
        lax.fori_loop(0, nv, scatter, 0)
        lax.fori_loop(nv, nv8, discard, 0)
        pltpu.make_async_copy(ybuf.at[pl.ds(0, nv8)], out_hbm.at[pl.ds(0, nv8)], sem_out).wait()


def _experts(h2, block_expert, n_valid, row_src, row_dst, wgu, bgu, wd, bd, rows):
    t_total, d = h2.shape
    n_blocks = block_expert.shape[0]
    d_ff2 = wgu.shape[2]
    grid_spec = pltpu.PrefetchScalarGridSpec(
        num_scalar_prefetch=2,
        grid=(n_blocks,),
        in_specs=[pl.BlockSpec((rows,), lambda i, be, nv: (i,), memory_space=pltpu.SMEM),
                  pl.BlockSpec((rows,), lambda i, be, nv: (i,), memory_space=pltpu.SMEM),
                  pl.BlockSpec(memory_space=pl.ANY),
                  pl.BlockSpec((1, d, d_ff2), lambda i, be, nv: (be[i], 0, 0)),
                  pl.BlockSpec((1, 1, d_ff2), lambda i, be, nv: (be[i], 0, 0)),
                  pl.BlockSpec((1, d_ff2 // 2, d), lambda i, be, nv: (be[i], 0, 0)),
                  pl.BlockSpec((1, 1, d), lambda i, be, nv: (be[i], 0, 0))],
        out_specs=pl.BlockSpec(memory_space=pl.ANY),
        scratch_shapes=[pltpu.VMEM((rows, d), F32), pltpu.VMEM((rows, d), F32), pltpu.VMEM((SUBLANES, d), F32),
                        pltpu.SemaphoreType.DMA, pltpu.SemaphoreType.DMA],
    )
    return pl.pallas_call(
        _expert_kernel,
        grid_spec=grid_spec,
        out_shape=jax.ShapeDtypeStruct((TOP_K * t_total, d), F32),
        compiler_params=_cparams(("arbitrary",)),
    )(block_expert, n_valid, row_src, row_dst, h2, wgu, bgu, wd, bd)


def _routing(top_idx, rows):
    t_total = top_idx.shape[0]
    onehot = (top_idx[:, :, None] == jnp.arange(N_EXPERTS, dtype=jnp.int32)).astype(jnp.int32)
    member = jnp.sum(onehot, axis=1)
    before = jnp.cumsum(member, axis=0) - member
    rank = jnp.take_along_axis(before, top_idx, axis=1)
    counts = jnp.sum(member, axis=0)
    padded = (counts + rows - 1) // rows * rows
    ends = jnp.cumsum(padded)
    starts = ends - padded
    dest = (starts[top_idx] + rank).reshape(-1)
    n_rows = (t_total * TOP_K + N_EXPERTS * (rows - 1) + rows - 1) // rows * rows
    n_blocks = n_rows // rows
    tok = jnp.repeat(jnp.arange(t_total, dtype=jnp.int32), TOP_K)
    slot = jnp.tile(jnp.arange(TOP_K, dtype=jnp.int32), t_total)
    row_src = jnp.zeros((n_rows,), jnp.int32).at[dest].set(tok)
    row_dst = jnp.zeros((n_rows,), jnp.int32).at[dest].set(slot * t_total + tok)
    block_start = jnp.arange(n_blocks, dtype=jnp.int32) * rows
    block_expert = jnp.minimum(jnp.searchsorted(ends, block_start, side='right'), N_EXPERTS - 1).astype(jnp.int32)
    n_valid = jnp.clip(starts[block_expert] + counts[block_expert] - block_start, 0, rows).astype(jnp.int32)
    return block_expert, n_valid, row_src, row_dst


def _combine_kernel(x1_ref, g2_ref, gate_ref, y4_ref, o_ref):
    gates = gate_ref[...]
    moe = gates[:, 0:1] * y4_ref[0]
    for kk in range(1, TOP_K):
        moe += gates[:, kk:kk + 1] * y4_ref[kk]
    o_ref[...] = x1_ref[...] + g2_ref[...] * moe


def _combine(x1, gate2, gates, y4, seq_len, tt):
    t_total, d = x1.shape
    nps = seq_len // tt
    return pl.pallas_call(
        _combine_kernel,
        grid=(t_total // tt,),
        in_specs=[pl.BlockSpec((tt, d), lambda i: (i, 0)),
                  pl.BlockSpec((None, 1, d), lambda i: (i // nps, 0, 0)),
                  pl.BlockSpec((tt, LANES), lambda i: (i, 0)),
                  pl.BlockSpec((TOP_K, tt, d), lambda i: (0, i, 0))],
        out_specs=pl.BlockSpec((tt, d), lambda i: (i, 0)),
        out_shape=jax.ShapeDtypeStruct((t_total, d), F32),
        compiler_params=_cparams(("parallel",)),
    )(x1, gate2, gates, y4)


def _tiles(seq_len):
    token_tile = min(512, seq_len)
    ret_chunk = min(256, seq_len)
    attn_tile = min(512, seq_len)
    expert_rows = 256
    combine_tile = min(256, seq_len)
    return token_tile, ret_chunk, attn_tile, expert_rows, combine_tile


def _trunk(x, mod, lam_init, norm1_w, w_in_b, ret_a_fwd, ret_a_bwd, ret_norm_w, diff_q_norm_w, diff_k_norm_w,
           lam_q1, lam_k1, lam_q2, lam_k2, diff_subln_w, rel_bias, w_out_b, norm2_w, rw_pad, rb_pad,
           wgu_b, b_gate_up, wd_b, b_down):
    batch, seq_len, d = x.shape
    tt, c, t, rows, ct = _tiles(seq_len)
    x2 = x.reshape(batch * seq_len, d)
    shift1, scale1, gate1, shift2, scale2, gate2 = [mod[:, n][:, None, :] for n in range(6)]

    pos = jnp.arange(seq_len, dtype=F32)
    inv_freq = ROPE_BASE ** (-jnp.arange(0, HEAD_DIM, 2, dtype=F32) / HEAD_DIM)
    ang = pos[:, None] * inv_freq[None, :]
    cos_t = jnp.tile(jnp.cos(ang), (1, LANES // (HEAD_DIM // 2)))
    sin_h = jnp.sin(ang)
    sin_t = jnp.tile(jnp.concatenate([-sin_h, sin_h], axis=1), (1, LANES // HEAD_DIM))
    qnw = jnp.tile(diff_q_norm_w.astype(F32), HALF_WIDTH // HEAD_DIM)[None, :]
    knw = jnp.tile(diff_k_norm_w.astype(F32), HALF_WIDTH // HEAD_DIM)[None, :]
    r = jnp.arange(HALF_WIDTH)
    bd = jnp.where((r[:, None] // HEAD_DIM) == (r[None, :] // HEAD_DIM), 1.0 / HEAD_DIM, 0.0).astype(BF16)

    rq, rk, rv, rg, dq, dk, dv = _inproj(x2, scale1, shift1, norm1_w[None, :], w_in_b, cos_t, sin_t, qnw, knw, bd,
                                         seq_len, tt)

    tabs = _retention_tables(ret_a_fwd, ret_a_bwd, c)
    ret_out = _retention(rq, rk, rv, rg, tabs, ret_norm_w.reshape(1, HALF_WIDTH).astype(F32), batch, seq_len, c)

    lam = (jnp.exp(jnp.sum(lam_q1.astype(F32) * lam_k1.astype(F32)))
           - jnp.exp(jnp.sum(lam_q2.astype(F32) * lam_k2.astype(F32))) + lam_init)
    bias_tiles, far_left, far_right = _bias_tables(rel_bias, t)
    scalars = jnp.concatenate([lam[None], jnp.stack([far_left, far_right], axis=1).reshape(-1)]).astype(F32)
    subln = (diff_subln_w.astype(F32) * (1.0 - lam_init))[None, :]
    diff_out = _attention(dq, dk, dv, bias_tiles, scalars, subln, batch, seq_len, t)

    x1, h2, idx_pad, gates_pad = _outproj(ret_out, diff_out, x2, gate1, scale2, shift2, norm2_w[None, :], w_out_b,
                                          rw_pad, rb_pad, seq_len, tt)
    block_expert, n_valid, row_src, row_dst = _routing(idx_pad[:, :TOP_K], rows)
    y4 = _experts(h2, block_expert, n_valid, row_src, row_dst, wgu_b, b_gate_up, wd_b, b_down, rows)
    y = _combine(x1, gate2, gates_pad, y4.reshape(TOP_K, batch * seq_len, d), seq_len, ct)
    return y.reshape(batch, seq_len, d)


def kernel(x_prompt, x_sample, c_prompt, c_sample, w_ada, b_ada, norm1_w, w_in, ret_a_fwd, ret_a_bwd, ret_norm_w,
           diff_q_norm_w, diff_k_norm_w, lam_q1, lam_k1, lam_q2, lam_k2, diff_subln_w, rel_bias, w_out, norm2_w,
           router_w, router_b, w_gate_up, b_gate_up, w_down, b_down):
    depth = w_ada.shape[0]
    d = x_prompt.shape[-1]
    n_prompt = c_prompt.shape[0]
    n_cond = n_prompt + c_sample.shape[0]
    c_all = jnp.concatenate([c_prompt, c_sample], axis=0).astype(F32)
    c_pad = jnp.pad(c_all, ((0, -n_cond % 8), (0, 0)))
    xs = [x_prompt, x_sample]
    for l in range(depth):
        lam_init = 0.8 - 0.6 * math.exp(-0.3 * l)
        mod = _ada(c_pad, w_ada[l], b_ada[l])[:n_cond].reshape(n_cond, 6, d)
        rw_pad = jnp.pad(router_w[l].astype(F32), ((0, 0), (0, LANES - N_EXPERTS)))
        rb_pad = jnp.pad(router_b[l].astype(F32), (0, LANES - N_EXPERTS), constant_values=-jnp.inf)[None, :]
        shared = (norm1_w[l], w_in[l].astype(BF16), ret_a_fwd[l], ret_a_bwd[l], ret_norm_w[l], diff_q_norm_w[l],
                  diff_k_norm_w[l], lam_q1[l], lam_k1[l], lam_q2[l], lam_k2[l], diff_subln_w[l], rel_bias,
                  w_out[l].astype(BF16), norm2_w[l], rw_pad, rb_pad,
                  w_gate_up[l].astype(BF16), b_gate_up[l][:, None, :], w_down[l].astype(BF16), b_down[l][:, None, :])
        xs = [_trunk(xs[0], mod[:n_prompt], lam_init, *shared),
              _trunk(xs[1], mod[n_prompt:], lam_init, *shared)]
    return (xs[0], xs[1])
```

```python
import functools
import math

import jax
import jax.numpy as jnp
import numpy as np
from jax import lax
from jax.experimental import pallas as pl
from jax.experimental.pallas import tpu as pltpu

F32 = jnp.float32
BF16 = jnp.bfloat16
HIGHEST = lax.Precision.HIGHEST

LANES = 128
SUBLANES = 8
HEAD_DIM = 64
RET_HEADS = 8
DIFF_HEADS = 4
HALF_WIDTH = 512
N_PAIRS = HALF_WIDTH // LANES
ROPE_BASE = 10000.0
NUM_BUCKETS = 32
MAX_DISTANCE = 128
N_EXPERTS = 32
TOP_K = 4
SWIGLU_LIMIT = 7.0
SWIGLU_ALPHA = 1.702
EPS = 1e-6
LOG2E = math.log2(math.e)
VMEM_LIMIT = 56 * 1024 * 1024


def _cparams(sem):
    return pltpu.CompilerParams(dimension_semantics=sem, vmem_limit_bytes=VMEM_LIMIT)


def _ada_kernel(c_ref, w_ref, b_ref, o_ref):
    c = c_ref[...]
    a = c * jax.nn.sigmoid(c)
    o_ref[...] = jnp.dot(a, w_ref[...], preferred_element_type=F32, precision=HIGHEST) + b_ref[...]


def _ada(c_pad, w_ada, b_ada):
    rows, d = c_pad.shape
    n = w_ada.shape[1]
    tn = d
    return pl.pallas_call(
        _ada_kernel,
        grid=(n // tn,),
        in_specs=[pl.BlockSpec((rows, d), lambda j: (0, 0)),
                  pl.BlockSpec((d, tn), lambda j: (0, j)),
                  pl.BlockSpec((1, tn), lambda j: (0, j))],
        out_specs=pl.BlockSpec((rows, tn), lambda j: (0, j)),
        out_shape=jax.ShapeDtypeStruct((rows, n), F32),
        compiler_params=_cparams(("parallel",)),
    )(c_pad, w_ada, b_ada.reshape(1, n))


def _rotate_half(xg):
    lane = lax.broadcasted_iota(jnp.int32, xg.shape, 1)
    first = (lane % HEAD_DIM) < (HEAD_DIM // 2)
    return jnp.where(first, pltpu.roll(xg, LANES - HEAD_DIM // 2, 1), pltpu.roll(xg, HEAD_DIM // 2, 1))


def _inproj_kernel(x_ref, sc_ref, sh_ref, nw_ref, w_ref, cos_ref, sin_ref, qnw_ref, knw_ref, bd_ref,
                   rq_ref, rk_ref, rv_ref, rg_ref, dq_ref, dk_ref, dv_ref):
    x = x_ref[...]
    ms = jnp.mean(x * x, axis=-1, keepdims=True)
    h = x * lax.rsqrt(ms + EPS) * nw_ref[...]
    h = (h * (1.0 + sc_ref[...]) + sh_ref[...]).astype(BF16)

    def piece(n):
        return jnp.dot(h, w_ref[:, n * HALF_WIDTH:(n + 1) * HALF_WIDTH], preferred_element_type=F32)

    cos = cos_ref[...]
    sin = sin_ref[...]

    def rotary(p, out_ref, scale):
        for g in range(N_PAIRS):
            xg = p[:, g * LANES:(g + 1) * LANES]
            y = xg * cos + _rotate_half(xg) * sin
            out_ref[:, g * LANES:(g + 1) * LANES] = (y * scale).astype(BF16)

    def head_norm(p, w, out_ref, scale):
        msq = jnp.dot((p * p).astype(BF16), bd_ref[...], preferred_element_type=F32)
        out_ref[...] = (p * lax.rsqrt(msq + EPS) * w * scale).astype(BF16)

    rotary(piece(0), rq_ref, 1.0)
    rotary(piece(1), rk_ref, HEAD_DIM ** -0.5)
    rv_ref[...] = piece(2).astype(BF16)
    rg_ref[...] = piece(3).astype(BF16)
    head_norm(piece(4), qnw_ref[...], dq_ref, HEAD_DIM ** -0.5 * LOG2E)
    head_norm(piece(5), knw_ref[...], dk_ref, 1.0)
    dv = piece(6).astype(BF16)
    ones = jnp.ones((dv.shape[0], LANES), BF16)
    for hd in range(DIFF_HEADS):
        dv_ref[:, 2 * hd * LANES:(2 * hd + 1) * LANES] = dv[:, hd * LANES:(hd + 1) * LANES]
        dv_ref[:, (2 * hd + 1) * LANES:(2 * hd + 2) * LANES] = ones


def _inproj(x2, scale1, shift1, norm_w, w_in, cos_t, sin_t, qnw, knw, bd, seq_len, tt):
    t_total, d = x2.shape
    nps = seq_len // tt
    n_in = w_in.shape[1]
    row = lambda i: (i, 0)
    mod = lambda i: (i // nps, 0, 0)
    pos = lambda i: (i % nps, 0)
    fixed = lambda i: (0, 0)
    out_sd = jax.ShapeDtypeStruct((t_total, HALF_WIDTH), BF16)
    return pl.pallas_call(
        _inproj_kernel,
        grid=(t_total // tt,),
        in_specs=[pl.BlockSpec((tt, d), row),
                  pl.BlockSpec((None, 1, d), mod),
                  pl.BlockSpec((None, 1, d), mod),
                  pl.BlockSpec((1, d), fixed),
                  pl.BlockSpec((d, n_in), fixed),
                  pl.BlockSpec((tt, LANES), pos),
                  pl.BlockSpec((tt, LANES), pos),
                  pl.BlockSpec((1, HALF_WIDTH), fixed),
                  pl.BlockSpec((1, HALF_WIDTH), fixed),
                  pl.BlockSpec((HALF_WIDTH, HALF_WIDTH), fixed)],
        out_specs=[pl.BlockSpec((tt, HALF_WIDTH), row)] * 6 + [pl.BlockSpec((tt, 2 * HALF_WIDTH), row)],
        out_shape=[out_sd] * 6 + [jax.ShapeDtypeStruct((t_total, 2 * HALF_WIDTH), BF16)],
        compiler_params=_cparams(("parallel",)),
    )(x2, scale1, shift1, norm_w, w_in, cos_t, sin_t, qnw, knw, bd)


def _kv_update(state_ref, p, k, v, kdec, cdec, bdmask):
    kd = (k.astype(F32) * kdec).astype(BF16)
    kv = lax.dot_general(kd, v, (((0,), (0,)), ((), ())), preferred_element_type=F32)
    state_ref[p] = state_ref[p] * cdec + kv * bdmask


def _ret_state_kernel(k_ref, v_ref, kdec_ref, cdec_ref, bdmask_ref, sb_ref, state_ref):
    @pl.when(pl.program_id(1) == 0)
    def _():
        state_ref[...] = jnp.zeros_like(state_ref)

    sb_ref[...] = state_ref[...]
    for p in range(N_PAIRS):
        sl = slice(p * LANES, (p + 1) * LANES)
        _kv_update(state_ref, p, k_ref[:, sl], v_ref[:, sl], kdec_ref[:, sl], cdec_ref[:, sl], bdmask_ref[...])


def _ret_main_kernel(q_ref, k_ref, v_ref, g_ref, dmat_ref, qdf_ref, qdb_ref, kdf_ref, cdf_ref, bdmask_ref,
                     nw_ref, sb_ref, o_ref, state_ref):
    @pl.when(pl.program_id(1) == 0)
    def _():
        state_ref[...] = jnp.zeros_like(state_ref)

    c = q_ref.shape[0]
    lane = lax.broadcasted_iota(jnp.int32, (c, LANES), 1)
    lo = lane < HEAD_DIM
    for p in range(N_PAIRS):
        sl = slice(p * LANES, (p + 1) * LANES)
        q = q_ref[:, sl]
        k = k_ref[:, sl]
        v = v_ref[:, sl]
        qf = q.astype(F32)
        acc = jnp.dot((qf * qdf_ref[:, sl]).astype(BF16), state_ref[p].astype(BF16), preferred_element_type=F32)
        acc += jnp.dot((qf * qdb_ref[:, sl]).astype(BF16), sb_ref[p].astype(BF16), preferred_element_type=F32)
        for hh in range(2):
            sel = lo if hh == 0 else jnp.logical_not(lo)
            qm = jnp.where(sel, q, jnp.zeros_like(q))
            vm = jnp.where(sel, v, jnp.zeros_like(v))
            s = lax.dot_general(qm, k, (((1,), (1,)), ((), ())), preferred_element_type=F32)
            w = (s * dmat_ref[2 * p + hh]).astype(BF16)
            acc += jnp.dot(w, vm, preferred_element_type=F32)
        _kv_update(state_ref, p, k, v, kdf_ref[:, sl], cdf_ref[:, sl], bdmask_ref[...])
        sq = acc * acc
        ms_lo = jnp.sum(jnp.where(lo, sq, 0.0), axis=-1, keepdims=True)
        ms_hi = jnp.sum(jnp.where(lo, 0.0, sq), axis=-1, keepdims=True)
        ms = jnp.where(lo, ms_lo, ms_hi) * (1.0 / HEAD_DIM)
        y = acc * lax.rsqrt(ms + EPS) * nw_ref[:, sl]
        gf = g_ref[:, sl].astype(F32)
        o_ref[:, sl] = (gf * jax.nn.sigmoid(gf) * y).astype(BF16)


def _retention(rq, rk, rv, rg, tabs, ret_nw, batch, seq_len, c):
    t_total = rq.shape[0]
    nc = seq_len // c
    dmat, qdf, qdb, kdf, kdb, cdf, cdb, bdmask = tabs
    fixed2 = lambda b, i: (0, 0)
    rev = lambda b, i: (b * nc + nc - 1 - i, 0)
    fwd = lambda b, i: (b * nc + i, 0)
    tile = pl.BlockSpec((c, HALF_WIDTH), fwd)
    tile_rev = pl.BlockSpec((c, HALF_WIDTH), rev)
    tab = pl.BlockSpec((c, HALF_WIDTH), fixed2)
    vec = pl.BlockSpec((1, HALF_WIDTH), fixed2)
    mask = pl.BlockSpec((LANES, LANES), fixed2)
    state = pltpu.VMEM((N_PAIRS, LANES, LANES), F32)
    sb = pl.pallas_call(
        _ret_state_kernel,
        grid=(batch, nc),
        in_specs=[tile_rev, tile_rev, tab, vec, mask],
        out_specs=pl.BlockSpec((None, None, N_PAIRS, LANES, LANES), lambda b, i: (b, nc - 1 - i, 0, 0, 0)),
        out_shape=jax.ShapeDtypeStruct((batch, nc, N_PAIRS, LANES, LANES), F32),
        scratch_shapes=[state],
        compiler_params=_cparams(("parallel", "arbitrary")),
    )(rk, rv, kdb, cdb, bdmask)
    return pl.pallas_call(
        _ret_main_kernel,
        grid=(batch, nc),
        in_specs=[tile, tile, tile, tile,
                  pl.BlockSpec((RET_HEADS, c, c), lambda b, i: (0, 0, 0)),
                  tab, tab, tab, vec, mask, vec,
                  pl.BlockSpec((None, None, N_PAIRS, LANES, LANES), lambda b, i: (b, i, 0, 0, 0))],
        out_specs=tile,
        out_shape=jax.ShapeDtypeStruct((t_total, HALF_WIDTH), BF16),
        scratch_shapes=[state],
        compiler_params=_cparams(("parallel", "arbitrary")),
    )(rq, rk, rv, rg, dmat, qdf, qdb, kdf, cdf, bdmask, ret_nw, sb)


def _retention_tables(a_fwd, a_bwd, c):
    lg_f = jnp.log1p(-jnp.exp(a_fwd.astype(F32)))
    lg_b = jnp.log1p(-jnp.exp(a_bwd.astype(F32)))
    pos = jnp.arange(c, dtype=F32)
    diff = pos[:, None] - pos[None, :]
    dmat = jnp.where(diff[None] >= 0,
                     jnp.exp(jnp.maximum(diff, 0.0)[None] * lg_f[:, None, None]),
                     jnp.exp(jnp.maximum(-diff, 0.0)[None] * lg_b[:, None, None]))
    lane_f = jnp.repeat(lg_f, HEAD_DIM)[None, :]
    lane_b = jnp.repeat(lg_b, HEAD_DIM)[None, :]
    qdf = jnp.exp((pos + 1.0)[:, None] * lane_f)
    kdf = jnp.exp((c - 1.0 - pos)[:, None] * lane_f)
    qdb = jnp.exp((c - pos)[:, None] * lane_b)
    kdb = jnp.exp(pos[:, None] * lane_b)
    cdf = jnp.exp(c * lane_f)
    cdb = jnp.exp(c * lane_b)
    r = jnp.arange(LANES)
    bdmask = ((r[:, None] // HEAD_DIM) == (r[None, :] // HEAD_DIM)).astype(F32)
    return dmat, qdf, qdb, kdf, kdb, cdf, cdb, bdmask


def _attn_kernel(sc_ref, q_ref, k_ref, v_ref, bias_ref, sw_ref, o_ref,
                 s0_ref, s1_ref, p0_ref, p1_ref, a0_ref, a1_ref, m_ref, acc_ref):
    h = pl.program_id(1)
    i = pl.program_id(2)
    t = q_ref.shape[0]
    nk = k_ref.shape[0] // t
    s_refs, p_refs, a_refs = (s0_ref, s1_ref), (p0_ref, p1_ref), (a0_ref, a1_ref)
    q = q_ref[...]
    lane = lax.broadcasted_iota(jnp.int32, q.shape, 1)
    zero = jnp.zeros_like(q)
    qq = jnp.concatenate([jnp.where(lane < HEAD_DIM, q, zero), jnp.where(lane < HEAD_DIM, zero, q)], axis=0)
    m_ref[...] = jnp.full_like(m_ref, -jnp.inf)
    acc_ref[...] = jnp.zeros_like(acc_ref)
    far_left = sc_ref[1 + 2 * h]
    far_right = sc_ref[2 + 2 * h]

    def scores(j, slot):
        kj = k_ref[pl.ds(pl.multiple_of(j * t, t), t), :]
        s_refs[slot][...] = lax.dot_general(qq, kj, (((1,), (1,)), ((), ())), preferred_element_type=F32)

    def near_bias(j, slot):
        @pl.when(jnp.abs(j - i) <= 1)
        def _():
            b = bias_ref[j - i + 1]
            s_refs[slot][:t, :] += b
            s_refs[slot][t:, :] += b

    def softmax(j, slot):
        c = jnp.where(j < i - 1, far_left, jnp.where(j > i + 1, far_right, 0.0))
        s = s_refs[slot][...]
        m_prev = m_ref[...]
        m_new = jnp.maximum(m_prev, jnp.max(s, axis=1, keepdims=True) + c)
        a_refs[slot][...] = jnp.exp2(m_prev - m_new)
        p = jnp.exp2(s - pltpu.repeat(m_new - c, t // LANES, axis=1))
        p_refs[slot][...] = p.astype(BF16)
        m_ref[...] = m_new

    def accumulate(j, slot):
        vj = v_ref[pl.ds(pl.multiple_of(j * t, t), t), :]
        pv = jnp.dot(p_refs[slot][...], vj, preferred_element_type=F32)
        acc_ref[...] = pltpu.repeat(a_refs[slot][...], 2, axis=1) * acc_ref[...] + pv

    scores(0, 0)
    near_bias(0, 0)
    scores(1, 1)
    softmax(0, 0)
    near_bias(1, 1)

    def step(jj, carry):
        j0 = 2 * jj
        scores(j0, 0)
        softmax(j0 - 1, 1)
        accumulate(j0 - 2, 0)
        near_bias(j0, 0)
        scores(j0 + 1, 1)
        softmax(j0, 0)
        accumulate(j0 - 1, 1)
        near_bias(j0 + 1, 1)
        return carry

    lax.fori_loop(1, nk // 2, step, 0)
    softmax(nk - 1, 1)
    accumulate(nk - 2, 0)
    accumulate(nk - 1, 1)

    acc = acc_ref[...]
    o = acc[:, :LANES] / acc[:, LANES:]
    att = o[:t] - sc_ref[0] * o[t:]
    msq = jnp.mean(att * att, axis=-1, keepdims=True)
    o_ref[...] = (att * lax.rsqrt(msq + EPS) * sw_ref[...]).astype(BF16)


def _attention(dq, dk, dv, bias_tiles, scalars, subln_w, batch, seq_len, t):
    t_total = dq.shape[0]
    nq = seq_len // t
    assert nq % 2 == 0
    s_buf = pltpu.VMEM((2 * t, t), F32)
    p_buf = pltpu.VMEM((2 * t, t), BF16)
    a_buf = pltpu.VMEM((2 * t, LANES), F32)
    grid_spec = pltpu.PrefetchScalarGridSpec(
        num_scalar_prefetch=1,
        grid=(batch, DIFF_HEADS, nq),
        in_specs=[pl.BlockSpec((t, LANES), lambda b, h, i, s: (b * nq + i, h)),
                  pl.BlockSpec((seq_len, LANES), lambda b, h, i, s: (b, h)),
                  pl.BlockSpec((seq_len, 2 * LANES), lambda b, h, i, s: (b, h)),
                  pl.BlockSpec((None, 3, t, t), lambda b, h, i, s: (h, 0, 0, 0)),
                  pl.BlockSpec((1, LANES), lambda b, h, i, s: (0, 0))],
        out_specs=pl.BlockSpec((t, LANES), lambda b, h, i, s: (b * nq + i, h)),
        scratch_shapes=[s_buf, s_buf, p_buf, p_buf, a_buf, a_buf,
                        pltpu.VMEM((2 * t, LANES), F32), pltpu.VMEM((2 * t, 2 * LANES), F32)],
    )
    return pl.pallas_call(
        _attn_kernel,
        grid_spec=grid_spec,
        out_shape=jax.ShapeDtypeStruct((t_total, HALF_WIDTH), BF16),
        compiler_params=_cparams(("parallel", "parallel", "parallel")),
    )(scalars, dq, dk, dv, bias_tiles, subln_w)


def _t5_bucket(rel):
    nb = NUM_BUCKETS // 2
    max_exact = nb // 2
    n = jnp.abs(rel)
    base = jnp.where(rel > 0, nb, 0)
    nf = jnp.maximum(n, 1).astype(F32)
    large = max_exact + (jnp.log(nf / max_exact) / math.log(MAX_DISTANCE / max_exact)
                         * (nb - max_exact)).astype(jnp.int32)
    large = jnp.minimum(large, nb - 1)
    return base + jnp.where(n < max_exact, n, large)


def _bias_tables(rel_bias, t):
    assert t >= MAX_DISTANCE
    table = rel_bias.astype(F32) * LOG2E
    qq = jnp.arange(t, dtype=jnp.int32)[:, None]
    kk = jnp.arange(t, dtype=jnp.int32)[None, :]
    bucket = jnp.stack([_t5_bucket(d * t + kk - qq) for d in (-1, 0, 1)], axis=0)
    onehot = (bucket[..., None] == jnp.arange(NUM_BUCKETS, dtype=jnp.int32)).astype(F32)
    tiles = jnp.einsum('dqkn,nh->hdqk', onehot, table, precision=HIGHEST)
    far_left = table[NUM_BUCKETS // 2 - 1]
    far_right = table[NUM_BUCKETS - 1]
    return tiles, far_left, far_right


def _outproj_kernel(ret_ref, dif_ref, x_ref, g1_ref, sc2_ref, sh2_ref, nw2_ref, wo_ref, rw_ref, rb_ref,
                    x1_ref, h2_ref, idx_ref, gate_ref):
    mix = jnp.dot(ret_ref[...], wo_ref[:HALF_WIDTH, :], preferred_element_type=F32)
    mix += jnp.dot(dif_ref[...], wo_ref[HALF_WIDTH:, :], preferred_element_type=F32)
    x1 = x_ref[...] + g1_ref[...] * mix
    x1_ref[...] = x1
    ms = jnp.mean(x1 * x1, axis=-1, keepdims=True)
    h2 = x1 * lax.rsqrt(ms + EPS) * nw2_ref[...]
    h2 = h2 * (1.0 + sc2_ref[...]) + sh2_ref[...]
    h2_ref[...] = h2
    logits = jnp.dot(h2, rw_ref[...], preferred_element_type=F32, precision=HIGHEST) + rb_ref[...]
    lane = lax.broadcasted_iota(jnp.int32, logits.shape, 1)
    vals, idxs = [], []
    for _ in range(TOP_K):
        m = jnp.max(logits, axis=-1, keepdims=True)
        am = jnp.min(jnp.where(logits == m, lane, LANES), axis=-1, keepdims=True)
        vals.append(m)
        idxs.append(am)
        logits = jnp.where(lane == am, -jnp.inf, logits)
    es = [jnp.exp(v - vals[0]) for v in vals]
    den = es[0] + es[1] + es[2] + es[3]
    idx_out = jnp.zeros(lane.shape, jnp.int32)
    gate_out = jnp.zeros(lane.shape, F32)
    for kk in range(TOP_K):
        idx_out = jnp.where(lane == kk, idxs[kk], idx_out)
        gate_out = jnp.where(lane == kk, es[kk] / den, gate_out)
    idx_ref[...] = idx_out
    gate_ref[...] = gate_out


def _outproj(ret_out, diff_out, x2, gate1, scale2, shift2, norm2_w, w_out, rw_pad, rb_pad, seq_len, tt):
    t_total, d = x2.shape
    nps = seq_len // tt
    row = lambda i: (i, 0)
    mod = lambda i: (i // nps, 0, 0)
    fixed = lambda i: (0, 0)
    return pl.pallas_call(
        _outproj_kernel,
        grid=(t_total // tt,),
        in_specs=[pl.BlockSpec((tt, HALF_WIDTH), row),
                  pl.BlockSpec((tt, HALF_WIDTH), row),
                  pl.BlockSpec((tt, d), row),
                  pl.BlockSpec((None, 1, d), mod),
                  pl.BlockSpec((None, 1, d), mod),
                  pl.BlockSpec((None, 1, d), mod),
                  pl.BlockSpec((1, d), fixed),
                  pl.BlockSpec((2 * HALF_WIDTH, d), fixed),
                  pl.BlockSpec((d, LANES), fixed),
                  pl.BlockSpec((1, LANES), fixed)],
        out_specs=[pl.BlockSpec((tt, d), row), pl.BlockSpec((tt, d), row),
                   pl.BlockSpec((tt, LANES), row), pl.BlockSpec((tt, LANES), row)],
        out_shape=[jax.ShapeDtypeStruct((t_total, d), F32), jax.ShapeDtypeStruct((t_total, d), F32),
                   jax.ShapeDtypeStruct((t_total, LANES), jnp.int32),
                   jax.ShapeDtypeStruct((t_total, LANES), F32)],
        compiler_params=_cparams(("parallel",)),
    )(ret_out, diff_out, x2, gate1, scale2, shift2, norm2_w, w_out, rw_pad, rb_pad)


def _expert_kernel(be_ref, nv_ref, src_ref, dst_ref, h2_hbm, wgu_ref, bgu_ref, wd_ref, bd_ref, out_hbm,
                   xbuf, ybuf, trash, sem_in, sem_out):
    i = pl.program_id(0)
    nv = nv_ref[i]
    rows = xbuf.shape[0]
    d_ff = wd_ref.shape[1]

    @pl.when(nv > 0)
    def _():
        def gather(r, carry):
            pltpu.make_async_copy(h2_hbm.at[pl.ds(src_ref[r], 1)], xbuf.at[pl.ds(r, 1)], sem_in).start()
            return carry

        lax.fori_loop(0, rows, gather, 0)
        pltpu.make_async_copy(h2_hbm.at[pl.ds(0, rows)], xbuf, sem_in).wait()
        x = xbuf[...].astype(BF16)
        gu = jnp.dot(x, wgu_ref[0], preferred_element_type=F32) + bgu_ref[0]
        glu = jnp.minimum(gu[:, :d_ff], SWIGLU_LIMIT)
        lin = jnp.clip(gu[:, d_ff:], -SWIGLU_LIMIT, SWIGLU_LIMIT)
        act = glu * jax.nn.sigmoid(SWIGLU_ALPHA * glu) * (lin + 1.0)
        ybuf[...] = jnp.dot(act.astype(BF16), wd_ref[0], preferred_element_type=F32) + bd_ref[0]

        def scatter(r, carry):
            pltpu.make_async_copy(ybuf.at[pl.ds(r, 1)], out_hbm.at[pl.ds(dst_ref[r], 1)], sem_out).start()
            return carry

        def discard(r, carry):
            pltpu.make_async_copy(ybuf.at[pl.ds(r, 1)], trash.at[pl.ds(r - nv, 1)], sem_out).start()
            return carry

        nv8 = pl.multiple_of((nv + SUBLANES - 1) // SUBLANES * SUBLANES, SUBLANES)
        lax.fori_loop(0, nv, scatter, 0)
        lax.fori_loop(nv, nv8, discard, 0)
        pltpu.make_async_copy(ybuf.at[pl.ds(0, nv8)], out_hbm.at[pl.ds(0, nv8)], sem_out).wait()


def _experts(h2, block_expert, n_valid, row_src, row_dst, wgu, bgu, wd, bd, rows):
    t_total, d = h2.shape
    n_blocks = block_expert.shape[0]
    d_ff2 = wgu.shape[2]
    grid_spec = pltpu.PrefetchScalarGridSpec(
        num_scalar_prefetch=2,
        grid=(n_blocks,),
        in_specs=[pl.BlockSpec((rows,), lambda i, be, nv: (i,), memory_space=pltpu.SMEM),
                  pl.BlockSpec((rows,), lambda i, be, nv: (i,), memory_space=pltpu.SMEM),
                  pl.BlockSpec(memory_space=pl.ANY),
                  pl.BlockSpec((1, d, d_ff2), lambda i, be, nv: (be[i], 0, 0)),
                  pl.BlockSpec((1, 1, d_ff2), lambda i, be, nv: (be[i], 0, 0)),
                  pl.BlockSpec((1, d_ff2 // 2, d), lambda i, be, nv: (be[i], 0, 0)),
                  pl.BlockSpec((1, 1, d), lambda i, be, nv: (be[i], 0, 0))],
        out_specs=pl.BlockSpec(memory_space=pl.ANY),
        scratch_shapes=[pltpu.VMEM((rows, d), F32), pltpu.VMEM((rows, d), F32), pltpu.VMEM((SUBLANES, d), F32),
                        pltpu.SemaphoreType.DMA, pltpu.SemaphoreType.DMA],
    )
    return pl.pallas_call(
        _expert_kernel,
        grid_spec=grid_spec,
        out_shape=jax.ShapeDtypeStruct((TOP_K * t_total, d), F32),
        compiler_params=_cparams(("arbitrary",)),
    )(block_expert, n_valid, row_src, row_dst, h2, wgu, bgu, wd, bd)


def _routing(top_idx, rows):
    t_total = top_idx.shape[0]
    onehot = (top_idx[:, :, None] == jnp.arange(N_EXPERTS, dtype=jnp.int32)).astype(jnp.int32)
    member = jnp.sum(onehot, axis=1)
    before = jnp.cumsum(member, axis=0) - member
    rank = jnp.take_along_axis(before, top_idx, axis=1)
    counts = jnp.sum(member, axis=0)
    padded = (counts + rows - 1) // rows * rows
    ends = jnp.cumsum(padded)
    starts = ends - padded
    dest = (starts[top_idx] + rank).reshape(-1)
    n_rows = (t_total * TOP_K + N_EXPERTS * (rows - 1) + rows - 1) // rows * rows
    n_blocks = n_rows // rows
    tok = jnp.repeat(jnp.arange(t_total, dtype=jnp.int32), TOP_K)
    slot = jnp.tile(jnp.arange(TOP_K, dtype=jnp.int32), t_total)
    row_dst = jnp.zeros((n_rows,), jnp.int32).at[dest].set(slot * t_total + tok, unique_indices=True)
    row_src = row_dst % t_total
    block_start = jnp.arange(n_blocks, dtype=jnp.int32) * rows
    block_expert = jnp.minimum(jnp.searchsorted(ends, block_start, side='right'), N_EXPERTS - 1).astype(jnp.int32)
    n_valid = jnp.clip(starts[block_expert] + counts[block_expert] - block_start, 0, rows).astype(jnp.int32)
    return block_expert, n_valid, row_src, row_dst


def _combine_kernel(x1_ref, g2_ref, gate_ref, y4_ref, o_ref):
    gates = gate_ref[...]
    moe = gates[:, 0:1] * y4_ref[0]
    for kk in range(1, TOP_K):
        moe += gates[:, kk:kk + 1] * y4_ref[kk]
    o_ref[...] = x1_ref[...] + g2_ref[...] * moe


def _combine(x1, gate2, gates, y4, seq_len, tt):
    t_total, d = x1.shape
    nps = seq_len // tt
    return pl.pallas_call(
        _combine_kernel,
        grid=(t_total // tt,),
        in_specs=[pl.BlockSpec((tt, d), lambda i: (i, 0)),
                  pl.BlockSpec((None, 1, d), lambda i: (i // nps, 0, 0)),
                  pl.BlockSpec((tt, LANES), lambda i: (i, 0)),
                  pl.BlockSpec((TOP_K, tt, d), lambda i: (0, i, 0))],
        out_specs=pl.BlockSpec((tt, d), lambda i: (i, 0)),
        out_shape=jax.ShapeDtypeStruct((t_total, d), F32),
        compiler_params=_cparams(("parallel",)),
    )(x1, gate2, gates, y4)


def _tiles(seq_len):
    token_tile = min(512, seq_len)
    ret_chunk = min(256, seq_len)
    attn_tile = min(512, seq_len)
    expert_rows = 256
    combine_tile = min(256, seq_len)
    return token_tile, ret_chunk, attn_tile, expert_rows, combine_tile


def _trunk(x, mod, lam_init, norm1_w, w_in_b, ret_a_fwd, ret_a_bwd, ret_norm_w, diff_q_norm_w, diff_k_norm_w,
           lam_q1, lam_k1, lam_q2, lam_k2, diff_subln_w, rel_bias, w_out_b, norm2_w, rw_pad, rb_pad,
           wgu_b, b_gate_up, wd_b, b_down):
    batch, seq_len, d = x.shape
    tt, c, t, rows, ct = _tiles(seq_len)
    x2 = x.reshape(batch * seq_len, d)
    shift1, scale1, gate1, shift2, scale2, gate2 = [mod[:, n][:, None, :] for n in range(6)]

    pos = jnp.arange(seq_len, dtype=F32)
    inv_freq = ROPE_BASE ** (-jnp.arange(0, HEAD_DIM, 2, dtype=F32) / HEAD_DIM)
    ang = pos[:, None] * inv_freq[None, :]
    cos_t = jnp.tile(jnp.cos(ang), (1, LANES // (HEAD_DIM // 2)))
    sin_h = jnp.sin(ang)
    sin_t = jnp.tile(jnp.concatenate([-sin_h, sin_h], axis=1), (1, LANES // HEAD_DIM))
    qnw = jnp.tile(diff_q_norm_w.astype(F32), HALF_WIDTH // HEAD_DIM)[None, :]
    knw = jnp.tile(diff_k_norm_w.astype(F32), HALF_WIDTH // HEAD_DIM)[None, :]
    r = jnp.arange(HALF_WIDTH)
    bd = jnp.where((r[:, None] // HEAD_DIM) == (r[None, :] // HEAD_DIM), 1.0 / HEAD_DIM, 0.0).astype(BF16)

    rq, rk, rv, rg, dq, dk, dv = _inproj(x2, scale1, shift1, norm1_w[None, :], w_in_b, cos_t, sin_t, qnw, knw, bd,
                                         seq_len, tt)

    tabs = _retention_tables(ret_a_fwd, ret_a_bwd, c)
    ret_out = _retention(rq, rk, rv, rg, tabs, ret_norm_w.reshape(1, HALF_WIDTH).astype(F32), batch, seq_len, c)

    lam = (jnp.exp(jnp.sum(lam_q1.astype(F32) * lam_k1.astype(F32)))
           - jnp.exp(jnp.sum(lam_q2.astype(F32) * lam_k2.astype(F32))) + lam_init)
    bias_tiles, far_left, far_right = _bias_tables(rel_bias, t)
    scalars = jnp.concatenate([lam[None], jnp.stack([far_left, far_right], axis=1).reshape(-1)]).astype(F32)
    subln = (diff_subln_w.astype(F32) * (1.0 - lam_init))[None, :]
    diff_out = _attention(dq, dk, dv, bias_tiles, scalars, subln, batch, seq_len, t)

    x1, h2, idx_pad, gates_pad = _outproj(ret_out, diff_out, x2, gate1, scale2, shift2, norm2_w[None, :], w_out_b,
                                          rw_pad, rb_pad, seq_len, tt)
    block_expert, n_valid, row_src, row_dst = _routing(idx_pad[:, :TOP_K], rows)
    y4 = _experts(h2, block_expert, n_valid, row_src, row_dst, wgu_b, b_gate_up, wd_b, b_down, rows)
    y = _combine(x1, gate2, gates_pad, y4.reshape(TOP_K, batch * seq_len, d), seq_len, ct)
    return y.reshape(batch, seq_len, d)


def kernel(x_prompt, x_sample, c_prompt, c_sample, w_ada, b_ada, norm1_w, w_in, ret_a_fwd, ret_a_bwd, ret_norm_w,
           diff_q_norm_w, diff_k_norm_w, lam_q1, lam_k1, lam_q2, lam_k2, diff_subln_w, rel_bias, w_out, norm2_w,
           router_w, router_b, w_gate_up, b_gate_up, w_down, b_down):
    depth = w_ada.shape[0]
    d = x_prompt.shape[-1]
    n_prompt = c_prompt.shape[0]
    n_cond = n_prompt + c_sample.shape[0]
    c_all = jnp.concatenate([c_prompt, c_sample], axis=0).astype(F32)
    c_pad = jnp.pad(c_all, ((0, -n_cond % 8), (0, 0)))
    xs = [x_prompt, x_sample]
    for l in range(depth):
        lam_init = 0.8 - 0.6 * math.exp(-0.3 * l)
        mod = _ada(c_pad, w_ada[l], b_ada[l])[:n_cond].reshape(n_cond, 6, d)
        rw_pad = jnp.pad(router_w[l].astype(F32), ((0, 0), (0, LANES - N_EXPERTS)))
        rb_pad = jnp.pad(router_b[l].astype(F32), (0, LANES - N_EXPERTS), constant_values=-jnp.inf)[None, :]
        shared = (norm1_w[l], w_in[l].astype(BF16), ret_a_fwd[l], ret_a_bwd[l], ret_norm_w[l], diff_q_norm_w[l],
                  diff_k_norm_w[l], lam_q1[l], lam_k1[l], lam_q2[l], lam_k2[l], diff_subln_w[l], rel_bias,
                  w_out[l].astype(BF16), norm2_w[l], rw_pad, rb_pad,
                  w_gate_up[l].astype(BF16), b_gate_up[l][:, None, :], w_down[l].astype(BF16), b_down[l][:, None, :])
        xs = [_trunk(xs[0], mod[:n_prompt], lam_init, *shared),
              _trunk(xs[1], mod[n_prompt:], lam_init, *shared)]
    return (xs[0], xs[1])
```

```python
import functools
import math

import jax
import jax.numpy as jnp
import numpy as np
from jax import lax
from jax.experimental import pallas as pl
from jax.experimental.pallas import tpu as pltpu

F32 = jnp.float32
BF16 = jnp.bfloat16
HIGHEST = lax.Precision.HIGHEST

LANES = 128
SUBLANES = 8
HEAD_DIM = 64
RET_HEADS = 8
DIFF_HEADS = 4
HALF_WIDTH = 512
N_PAIRS = HALF_WIDTH // LANES
ROPE_BASE = 10000.0
NUM_BUCKETS = 32
MAX_DISTANCE = 128
N_EXPERTS = 32
TOP_K = 4
SWIGLU_LIMIT = 7.0
SWIGLU_ALPHA = 1.702
EPS = 1e-6
LOG2E = math.log2(math.e)
ATTN_ROWS = 128
ATTN_UNROLL = 4
VMEM_LIMIT = 56 * 1024 * 1024


def _cparams(sem):
    return pltpu.CompilerParams(dimension_semantics=sem, vmem_limit_bytes=VMEM_LIMIT)


def _ada_kernel(c_ref, w_ref, b_ref, o_ref):
    c = c_ref[...]
    a = c * jax.nn.sigmoid(c)
    o_ref[...] = jnp.dot(a, w_ref[...], preferred_element_type=F32, precision=HIGHEST) + b_ref[...]


def _ada(c_pad, w_ada, b_ada):
    rows, d = c_pad.shape
    n = w_ada.shape[1]
    tn = d
    return pl.pallas_call(
        _ada_kernel,
        grid=(n // tn,),
        in_specs=[pl.BlockSpec((rows, d), lambda j: (0, 0)),
                  pl.BlockSpec((d, tn), lambda j: (0, j)),
                  pl.BlockSpec((1, tn), lambda j: (0, j))],
        out_specs=pl.BlockSpec((rows, tn), lambda j: (0, j)),
        out_shape=jax.ShapeDtypeStruct((rows, n), F32),
        compiler_params=_cparams(("parallel",)),
    )(c_pad, w_ada, b_ada.reshape(1, n))


def _rotate_half(xg):
    lane = lax.broadcasted_iota(jnp.int32, xg.shape, 1)
    first = (lane % HEAD_DIM) < (HEAD_DIM // 2)
    return jnp.where(first, pltpu.roll(xg, LANES - HEAD_DIM // 2, 1), pltpu.roll(xg, HEAD_DIM // 2, 1))


def _inproj_kernel(x_ref, sc_ref, sh_ref, nw_ref, w_ref, cos_ref, sin_ref, qnw_ref, knw_ref, bd_ref,
                   rq_ref, rk_ref, rv_ref, rg_ref, dq_ref, dk_ref, dv_ref):
    x = x_ref[...]
    ms = jnp.mean(x * x, axis=-1, keepdims=True)
    h = x * lax.rsqrt(ms + EPS) * nw_ref[...]
    h = (h * (1.0 + sc_ref[...]) + sh_ref[...]).astype(BF16)

    def piece(n):
        return jnp.dot(h, w_ref[:, n * HALF_WIDTH:(n + 1) * HALF_WIDTH], preferred_element_type=F32)

    cos = cos_ref[...]
    sin = sin_ref[...]

    def rotary(p, out_ref, scale):
        for g in range(N_PAIRS):
            xg = p[:, g * LANES:(g + 1) * LANES]
            y = xg * cos + _rotate_half(xg) * sin
            out_ref[:, g * LANES:(g + 1) * LANES] = (y * scale).astype(BF16)

    def head_norm(p, w, out_ref, scale):
        msq = jnp.dot((p * p).astype(BF16), bd_ref[...], preferred_element_type=F32)
        out_ref[...] = (p * lax.rsqrt(msq + EPS) * w * scale).astype(BF16)

    rotary(piece(0), rq_ref, 1.0)
    rotary(piece(1), rk_ref, HEAD_DIM ** -0.5)
    rv_ref[...] = piece(2).astype(BF16)
    rg_ref[...] = piece(3).astype(BF16)
    head_norm(piece(4), qnw_ref[...], dq_ref, HEAD_DIM ** -0.5 * LOG2E)
    head_norm(piece(5), knw_ref[...], dk_ref, 1.0)
    dv = piece(6).astype(BF16)
    ones = jnp.ones((dv.shape[0], LANES), BF16)
    for hd in range(DIFF_HEADS):
        dv_ref[:, 2 * hd * LANES:(2 * hd + 1) * LANES] = dv[:, hd * LANES:(hd + 1) * LANES]
        dv_ref[:, (2 * hd + 1) * LANES:(2 * hd + 2) * LANES] = ones


def _inproj(x2, scale1, shift1, norm_w, w_in, cos_t, sin_t, qnw, knw, bd, seq_len, tt):
    t_total, d = x2.shape
    nps = seq_len // tt
    n_in = w_in.shape[1]
    row = lambda i: (i, 0)
    mod = lambda i: (i // nps, 0, 0)
    pos = lambda i: (i % nps, 0)
    fixed = lambda i: (0, 0)
    out_sd = jax.ShapeDtypeStruct((t_total, HALF_WIDTH), BF16)
    return pl.pallas_call(
        _inproj_kernel,
        grid=(t_total // tt,),
        in_specs=[pl.BlockSpec((tt, d), row),
                  pl.BlockSpec((None, 1, d), mod),
                  pl.BlockSpec((None, 1, d), mod),
                  pl.BlockSpec((1, d), fixed),
                  pl.BlockSpec((d, n_in), fixed),
                  pl.BlockSpec((tt, LANES), pos),
                  pl.BlockSpec((tt, LANES), pos),
                  pl.BlockSpec((1, HALF_WIDTH), fixed),
                  pl.BlockSpec((1, HALF_WIDTH), fixed),
                  pl.BlockSpec((HALF_WIDTH, HALF_WIDTH), fixed)],
        out_specs=[pl.BlockSpec((tt, HALF_WIDTH), row)] * 6 + [pl.BlockSpec((tt, 2 * HALF_WIDTH), row)],
        out_shape=[out_sd] * 6 + [jax.ShapeDtypeStruct((t_total, 2 * HALF_WIDTH), BF16)],
        compiler_params=_cparams(("parallel",)),
    )(x2, scale1, shift1, norm_w, w_in, cos_t, sin_t, qnw, knw, bd)


def _kv_update(state_ref, p, k, v, kdec, cdec, bdmask):
    kd = (k.astype(F32) * kdec).astype(BF16)
    kv = lax.dot_general(kd, v, (((0,), (0,)), ((), ())), preferred_element_type=F32)
    state_ref[p] = state_ref[p] * cdec + kv * bdmask


def _ret_state_kernel(k_ref, v_ref, kdec_ref, cdec_ref, bdmask_ref, sb_ref, state_ref):
    @pl.when(pl.program_id(1) == 0)
    def _():
        state_ref[...] = jnp.zeros_like(state_ref)

    sb_ref[...] = state_ref[...]
    for p in range(N_PAIRS):
        sl = slice(p * LANES, (p + 1) * LANES)
        _kv_update(state_ref, p, k_ref[:, sl], v_ref[:, sl], kdec_ref[:, sl], cdec_ref[:, sl], bdmask_ref[...])


def _ret_main_kernel(q_ref, k_ref, v_ref, g_ref, dmat_ref, qdf_ref, qdb_ref, kdf_ref, cdf_ref, bdmask_ref,
                     nw_ref, sb_ref, o_ref, state_ref):
    @pl.when(pl.program_id(1) == 0)
    def _():
        state_ref[...] = jnp.zeros_like(state_ref)

    c = q_ref.shape[0]
    lane = lax.broadcasted_iota(jnp.int32, (c, LANES), 1)
    lo = lane < HEAD_DIM
    for p in range(N_PAIRS):
        sl = slice(p * LANES, (p + 1) * LANES)
        q = q_ref[:, sl]
        k = k_ref[:, sl]
        v = v_ref[:, sl]
        qf = q.astype(F32)
        acc = jnp.dot((qf * qdf_ref[:, sl]).astype(BF16), state_ref[p].astype(BF16), preferred_element_type=F32)
        acc += jnp.dot((qf * qdb_ref[:, sl]).astype(BF16), sb_ref[p].astype(BF16), preferred_element_type=F32)
        for hh in range(2):
            sel = lo if hh == 0 else jnp.logical_not(lo)
            qm = jnp.where(sel, q, jnp.zeros_like(q))
            vm = jnp.where(sel, v, jnp.zeros_like(v))
            s = lax.dot_general(qm, k, (((1,), (1,)), ((), ())), preferred_element_type=F32)
            w = (s * dmat_ref[2 * p + hh]).astype(BF16)
            acc += jnp.dot(w, vm, preferred_element_type=F32)
        _kv_update(state_ref, p, k, v, kdf_ref[:, sl], cdf_ref[:, sl], bdmask_ref[...])
        sq = acc * acc
        ms_lo = jnp.sum(jnp.where(lo, sq, 0.0), axis=-1, keepdims=True)
        ms_hi = jnp.sum(jnp.where(lo, 0.0, sq), axis=-1, keepdims=True)
        ms = jnp.where(lo, ms_lo, ms_hi) * (1.0 / HEAD_DIM)
        y = acc * lax.rsqrt(ms + EPS) * nw_ref[:, sl]
        gf = g_ref[:, sl].astype(F32)
        o_ref[:, sl] = (gf * jax.nn.sigmoid(gf) * y).astype(BF16)


def _retention(rq, rk, rv, rg, tabs, ret_nw, batch, seq_len, c):
    t_total = rq.shape[0]
    nc = seq_len // c
    dmat, qdf, qdb, kdf, kdb, cdf, cdb, bdmask = tabs
    fixed2 = lambda b, i: (0, 0)
    rev = lambda b, i: (b * nc + nc - 1 - i, 0)
    fwd = lambda b, i: (b * nc + i, 0)
    tile = pl.BlockSpec((c, HALF_WIDTH), fwd)
    tile_rev = pl.BlockSpec((c, HALF_WIDTH), rev)
    tab = pl.BlockSpec((c, HALF_WIDTH), fixed2)
    vec = pl.BlockSpec((1, HALF_WIDTH), fixed2)
    mask = pl.BlockSpec((LANES, LANES), fixed2)
    state = pltpu.VMEM((N_PAIRS, LANES, LANES), F32)
    sb = pl.pallas_call(
        _ret_state_kernel,
        grid=(batch, nc),
        in_specs=[tile_rev, tile_rev, tab, vec, mask],
        out_specs=pl.BlockSpec((None, None, N_PAIRS, LANES, LANES), lambda b, i: (b, nc - 1 - i, 0, 0, 0)),
        out_shape=jax.ShapeDtypeStruct((batch, nc, N_PAIRS, LANES, LANES), F32),
        scratch_shapes=[state],
        compiler_params=_cparams(("parallel", "arbitrary")),
    )(rk, rv, kdb, cdb, bdmask)
    return pl.pallas_call(
        _ret_main_kernel,
        grid=(batch, nc),
        in_specs=[tile, tile, tile, tile,
                  pl.BlockSpec((RET_HEADS, c, c), lambda b, i: (0, 0, 0)),
                  tab, tab, tab, vec, mask, vec,
                  pl.BlockSpec((None, None, N_PAIRS, LANES, LANES), lambda b, i: (b, i, 0, 0, 0))],
        out_specs=tile,
        out_shape=jax.ShapeDtypeStruct((t_total, HALF_WIDTH), BF16),
        scratch_shapes=[state],
        compiler_params=_cparams(("parallel", "arbitrary")),
    )(rq, rk, rv, rg, dmat, qdf, qdb, kdf, cdf, bdmask, ret_nw, sb)


def _retention_tables(a_fwd, a_bwd, c):
    lg_f = jnp.log1p(-jnp.exp(a_fwd.astype(F32)))
    lg_b = jnp.log1p(-jnp.exp(a_bwd.astype(F32)))
    pos = jnp.arange(c, dtype=F32)
    diff = pos[:, None] - pos[None, :]
    dmat = jnp.where(diff[None] >= 0,
                     jnp.exp(jnp.maximum(diff, 0.0)[None] * lg_f[:, None, None]),
                     jnp.exp(jnp.maximum(-diff, 0.0)[None] * lg_b[:, None, None]))
    lane_f = jnp.repeat(lg_f, HEAD_DIM)[None, :]
    lane_b = jnp.repeat(lg_b, HEAD_DIM)[None, :]
    qdf = jnp.exp((pos + 1.0)[:, None] * lane_f)
    kdf = jnp.exp((c - 1.0 - pos)[:, None] * lane_f)
    qdb = jnp.exp((c - pos)[:, None] * lane_b)
    kdb = jnp.exp(pos[:, None] * lane_b)
    cdf = jnp.exp(c * lane_f)
    cdb = jnp.exp(c * lane_b)
    r = jnp.arange(LANES)
    bdmask = ((r[:, None] // HEAD_DIM) == (r[None, :] // HEAD_DIM)).astype(F32)
    return dmat, qdf, qdb, kdf, kdb, cdf, cdb, bdmask


def _attn_kernel(lam_ref, q_ref, k_ref, v_ref, bias_ref, sw_ref, o_ref, qq_ref, m_ref, acc_ref):
    i = pl.program_id(2)
    t = q_ref.shape[0]
    nk = k_ref.shape[0] // t
    q = q_ref[...]
    lane = lax.broadcasted_iota(jnp.int32, q.shape, 1)
    zero = jnp.zeros_like(q)
    qq_ref[...] = jnp.concatenate([jnp.where(lane < HEAD_DIM, q, zero), jnp.where(lane < HEAD_DIM, zero, q)], axis=0)
    m_ref[...] = jnp.full_like(m_ref, -jnp.inf)
    acc_ref[...] = jnp.zeros_like(acc_ref)

    def tile(j):
        keys = pl.ds(pl.multiple_of(j * t, t), t)
        which = jnp.clip(j - i, -2, 2) + 2
        for r in range(0, 2 * t, ATTN_ROWS):
            rows = slice(r, r + ATTN_ROWS)
            s = lax.dot_general(qq_ref[rows, :], k_ref[keys, :], (((1,), (1,)), ((), ())),
                                preferred_element_type=F32)
            s = s + bias_ref[which, r % t:r % t + ATTN_ROWS, :]
            m_prev = m_ref[rows, :]
            m_new = jnp.maximum(m_prev, jnp.max(s, axis=1, keepdims=True))
            alpha = jnp.exp2(m_prev - m_new)
            p = jnp.exp2(s - jnp.concatenate([m_new] * (t // LANES), axis=1)).astype(BF16)
            pv = jnp.dot(p, v_ref[keys, :], preferred_element_type=F32)
            acc_ref[rows, :] = jnp.concatenate([alpha, alpha], axis=1) * acc_ref[rows, :] + pv
            m_ref[rows, :] = m_new

    def step(jj, carry):
        for u in range(ATTN_UNROLL):
            tile(jj * ATTN_UNROLL + u)
        return carry

    lax.fori_loop(0, nk // ATTN_UNROLL, step, 0)

    acc = acc_ref[...]
    o = acc[:, :LANES] / acc[:, LANES:]
    att = o[:t] - lam_ref[0] * o[t:]
    msq = jnp.mean(att * att, axis=-1, keepdims=True)
    o_ref[...] = (att * lax.rsqrt(msq + EPS) * sw_ref[...]).astype(BF16)


def _attention(dq, dk, dv, bias_tiles, lam, subln_w, batch, seq_len, t):
    t_total = dq.shape[0]
    nq = seq_len // t
    assert nq % ATTN_UNROLL == 0 and t % ATTN_ROWS == 0
    grid_spec = pltpu.PrefetchScalarGridSpec(
        num_scalar_prefetch=1,
        grid=(batch, DIFF_HEADS, nq),
        in_specs=[pl.BlockSpec((t, LANES), lambda b, h, i, s: (b * nq + i, h)),
                  pl.BlockSpec((seq_len, LANES), lambda b, h, i, s: (b, h)),
                  pl.BlockSpec((seq_len, 2 * LANES), lambda b, h, i, s: (b, h)),
                  pl.BlockSpec((None, 5, t, t), lambda b, h, i, s: (h, 0, 0, 0)),
                  pl.BlockSpec((1, LANES), lambda b, h, i, s: (0, 0))],
        out_specs=pl.BlockSpec((t, LANES), lambda b, h, i, s: (b * nq + i, h)),
        scratch_shapes=[pltpu.VMEM((2 * t, LANES), BF16), pltpu.VMEM((2 * t, LANES), F32),
                        pltpu.VMEM((2 * t, 2 * LANES), F32)],
    )
    return pl.pallas_call(
        _attn_kernel,
        grid_spec=grid_spec,
        out_shape=jax.ShapeDtypeStruct((t_total, HALF_WIDTH), BF16),
        compiler_params=_cparams(("parallel", "parallel", "parallel")),
    )(lam, dq, dk, dv, bias_tiles, subln_w)


def _t5_bucket(rel):
    nb = NUM_BUCKETS // 2
    max_exact = nb // 2
    n = jnp.abs(rel)
    base = jnp.where(rel > 0, nb, 0)
    nf = jnp.maximum(n, 1).astype(F32)
    large = max_exact + (jnp.log(nf / max_exact) / math.log(MAX_DISTANCE / max_exact)
                         * (nb - max_exact)).astype(jnp.int32)
    large = jnp.minimum(large, nb - 1)
    return base + jnp.where(n < max_exact, n, large)


def _bias_tables(rel_bias, t):
    assert t >= MAX_DISTANCE
    table = rel_bias.astype(F32) * LOG2E
    qq = jnp.arange(t, dtype=jnp.int32)[:, None]
    kk = jnp.arange(t, dtype=jnp.int32)[None, :]
    bucket = jnp.stack([_t5_bucket(d * t + kk - qq) for d in (-2, -1, 0, 1, 2)], axis=0)
    onehot = (bucket[..., None] == jnp.arange(NUM_BUCKETS, dtype=jnp.int32)).astype(F32)
    return jnp.einsum('dqkn,nh->hdqk', onehot, table, precision=HIGHEST)


def _outproj_kernel(ret_ref, dif_ref, x_ref, g1_ref, sc2_ref, sh2_ref, nw2_ref, wo_ref, rw_ref, rb_ref,
                    x1_ref, h2_ref, idx_ref, gate_ref):
    mix = jnp.dot(ret_ref[...], wo_ref[:HALF_WIDTH, :], preferred_element_type=F32)
    mix += jnp.dot(dif_ref[...], wo_ref[HALF_WIDTH:, :], preferred_element_type=F32)
    x1 = x_ref[...] + g1_ref[...] * mix
    x1_ref[...] = x1
    ms = jnp.mean(x1 * x1, axis=-1, keepdims=True)
    h2 = x1 * lax.rsqrt(ms + EPS) * nw2_ref[...]
    h2 = h2 * (1.0 + sc2_ref[...]) + sh2_ref[...]
    h2_ref[...] = h2
    logits = jnp.dot(h2, rw_ref[...], preferred_element_type=F32, precision=HIGHEST) + rb_ref[...]
    lane = lax.broadcasted_iota(jnp.int32, logits.shape, 1)
    vals, idxs = [], []
    for _ in range(TOP_K):
        m = jnp.max(logits, axis=-1, keepdims=True)
        am = jnp.min(jnp.where(logits == m, lane, LANES), axis=-1, keepdims=True)
        vals.append(m)
        idxs.append(am)
        logits = jnp.where(lane == am, -jnp.inf, logits)
    es = [jnp.exp(v - vals[0]) for v in vals]
    den = es[0] + es[1] + es[2] + es[3]
    idx_out = jnp.zeros(lane.shape, jnp.int32)
    gate_out = jnp.zeros(lane.shape, F32)
    for kk in range(TOP_K):
        idx_out = jnp.where(lane == kk, idxs[kk], idx_out)
        gate_out = jnp.where(lane == kk, es[kk] / den, gate_out)
    idx_ref[...] = idx_out
    gate_ref[...] = gate_out


def _outproj(ret_out, diff_out, x2, gate1, scale2, shift2, norm2_w, w_out, rw_pad, rb_pad, seq_len, tt):
    t_total, d = x2.shape
    nps = seq_len // tt
    row = lambda i: (i, 0)
    mod = lambda i: (i // nps, 0, 0)
    fixed = lambda i: (0, 0)
    return pl.pallas_call(
        _outproj_kernel,
        grid=(t_total // tt,),
        in_specs=[pl.BlockSpec((tt, HALF_WIDTH), row),
                  pl.BlockSpec((tt, HALF_WIDTH), row),
                  pl.BlockSpec((tt, d), row),
                  pl.BlockSpec((None, 1, d), mod),
                  pl.BlockSpec((None, 1, d), mod),
                  pl.BlockSpec((None, 1, d), mod),
                  pl.BlockSpec((1, d), fixed),
                  pl.BlockSpec((2 * HALF_WIDTH, d), fixed),
                  pl.BlockSpec((d, LANES), fixed),
                  pl.BlockSpec((1, LANES), fixed)],
        out_specs=[pl.BlockSpec((tt, d), row), pl.BlockSpec((tt, d), row),
                   pl.BlockSpec((tt, LANES), row), pl.BlockSpec((tt, LANES), row)],
        out_shape=[jax.ShapeDtypeStruct((t_total, d), F32), jax.ShapeDtypeStruct((t_total, d), F32),
                   jax.ShapeDtypeStruct((t_total, LANES), jnp.int32),
                   jax.ShapeDtypeStruct((t_total, LANES), F32)],
        compiler_params=_cparams(("parallel",)),
    )(ret_out, diff_out, x2, gate1, scale2, shift2, norm2_w, w_out, rw_pad, rb_pad)


def _expert_kernel(be_ref, nv_ref, src_ref, dst_ref, h2_hbm, wgu_ref, bgu_ref, wd_ref, bd_ref, out_hbm,
                   xbuf, ybuf, trash, sem_in, sem_out):
    i = pl.program_id(0)
    nv = nv_ref[i]
    rows = xbuf.shape[0]
    d_ff = wd_ref.shape[1]

    @pl.when(nv > 0)
    def _():
        def gather(r, carry):
            pltpu.make_async_copy(h2_hbm.at[pl.ds(src_ref[r], 1)], xbuf.at[pl.ds(r, 1)], sem_in).start()
            return carry

        lax.fori_loop(0, rows, gather, 0)
        pltpu.make_async_copy(h2_hbm.at[pl.ds(0, rows)], xbuf, sem_in).wait()
        x = xbuf[...].astype(BF16)
        gu = jnp.dot(x, wgu_ref[0], preferred_element_type=F32) + bgu_ref[0]
        glu = jnp.minimum(gu[:, :d_ff], SWIGLU_LIMIT)
        lin = jnp.clip(gu[:, d_ff:], -SWIGLU_LIMIT, SWIGLU_LIMIT)
        act = glu * jax.nn.sigmoid(SWIGLU_ALPHA * glu) * (lin + 1.0)
        ybuf[...] = jnp.dot(act.astype(BF16), wd_ref[0], preferred_element_type=F32) + bd_ref[0]

        def scatter(r, carry):
            pltpu.make_async_copy(ybuf.at[pl.ds(r, 1)], out_hbm.at[pl.ds(dst_ref[r], 1)], sem_out).start()
            return carry

        def discard(r, carry):
            pltpu.make_async_copy(ybuf.at[pl.ds(r, 1)], trash.at[pl.ds(r - nv, 1)], sem_out).start()
            return carry

        nv8 = pl.multiple_of((nv + SUBLANES - 1) // SUBLANES * SUBLANES, SUBLANES)
        lax.fori_loop(0, nv, scatter, 0)
        lax.fori_loop(nv, nv8, discard, 0)
        pltpu.make_async_copy(ybuf.at[pl.ds(0, nv8)], out_hbm.at[pl.ds(0, nv8)], sem_out).wait()


def _experts(h2, block_expert, n_valid, row_src, row_dst, wgu, bgu, wd, bd, rows):
    t_total, d = h2.shape
    n_blocks = block_expert.shape[0]
    d_ff2 = wgu.shape[2]
    grid_spec = pltpu.PrefetchScalarGridSpec(
        num_scalar_prefetch=2,
        grid=(n_blocks,),
        in_specs=[pl.BlockSpec((rows,), lambda i, be, nv: (i,), memory_space=pltpu.SMEM),
                  pl.BlockSpec((rows,), lambda i, be, nv: (i,), memory_space=pltpu.SMEM),
                  pl.BlockSpec(memory_space=pl.ANY),
                  pl.BlockSpec((1, d, d_ff2), lambda i, be, nv: (be[i], 0, 0)),
                  pl.BlockSpec((1, 1, d_ff2), lambda i, be, nv: (be[i], 0, 0)),
                  pl.BlockSpec((1, d_ff2 // 2, d), lambda i, be, nv: (be[i], 0, 0)),
                  pl.BlockSpec((1, 1, d), lambda i, be, nv: (be[i], 0, 0))],
        out_specs=pl.BlockSpec(memory_space=pl.ANY),
        scratch_shapes=[pltpu.VMEM((rows, d), F32), pltpu.VMEM((rows, d), F32), pltpu.VMEM((SUBLANES, d), F32),
                        pltpu.SemaphoreType.DMA, pltpu.SemaphoreType.DMA],
    )
    return pl.pallas_call(
        _expert_kernel,
        grid_spec=grid_spec,
        out_shape=jax.ShapeDtypeStruct((TOP_K * t_total, d), F32),
        compiler_params=_cparams(("arbitrary",)),
    )(block_expert, n_valid, row_src, row_dst, h2, wgu, bgu, wd, bd)


def _routing(top_idx, rows):
    t_total = top_idx.shape[0]
    onehot = (top_idx[:, :, None] == jnp.arange(N_EXPERTS, dtype=jnp.int32)).astype(jnp.int32)
    member = jnp.sum(onehot, axis=1)
    before = jnp.cumsum(member, axis=0) - member
    rank = jnp.take_along_axis(before, top_idx, axis=1)
    counts = jnp.sum(member, axis=0)
    padded = (counts + rows - 1) // rows * rows
    ends = jnp.cumsum(padded)
    starts = ends - padded
    dest = (starts[top_idx] + rank).reshape(-1)
    n_rows = (t_total * TOP_K + N_EXPERTS * (rows - 1) + rows - 1) // rows * rows
    n_blocks = n_rows // rows
    tok = jnp.repeat(jnp.arange(t_total, dtype=jnp.int32), TOP_K)
    slot = jnp.tile(jnp.arange(TOP_K, dtype=jnp.int32), t_total)
    row_dst = jnp.zeros((n_rows,), jnp.int32).at[dest].set(slot * t_total + tok, unique_indices=True)
    row_src = row_dst % t_total
    block_start = jnp.arange(n_blocks, dtype=jnp.int32) * rows
    block_expert = jnp.minimum(jnp.searchsorted(ends, block_start, side='right'), N_EXPERTS - 1).astype(jnp.int32)
    n_valid = jnp.clip(starts[block_expert] + counts[block_expert] - block_start, 0, rows).astype(jnp.int32)
    return block_expert, n_valid, row_src, row_dst


def _combine_kernel(x1_ref, g2_ref, gate_ref, y4_ref, o_ref):
    gates = gate_ref[...]
    moe = gates[:, 0:1] * y4_ref[0]
    for kk in range(1, TOP_K):
        moe += gates[:, kk:kk + 1] * y4_ref[kk]
    o_ref[...] = x1_ref[...] + g2_ref[...] * moe


def _combine(x1, gate2, gates, y4, seq_len, tt):
    t_total, d = x1.shape
    nps = seq_len // tt
    return pl.pallas_call(
        _combine_kernel,
        grid=(t_total // tt,),
        in_specs=[pl.BlockSpec((tt, d), lambda i: (i, 0)),
                  pl.BlockSpec((None, 1, d), lambda i: (i // nps, 0, 0)),
                  pl.BlockSpec((tt, LANES), lambda i: (i, 0)),
                  pl.BlockSpec((TOP_K, tt, d), lambda i: (0, i, 0))],
        out_specs=pl.BlockSpec((tt, d), lambda i: (i, 0)),
        out_shape=jax.ShapeDtypeStruct((t_total, d), F32),
        compiler_params=_cparams(("parallel",)),
    )(x1, gate2, gates, y4)


def _tiles(seq_len):
    token_tile = min(512, seq_len)
    ret_chunk = min(256, seq_len)
    attn_tile = min(512, seq_len)
    expert_rows = 256
    combine_tile = min(256, seq_len)
    return token_tile, ret_chunk, attn_tile, expert_rows, combine_tile


def _trunk(x, mod, lam_init, norm1_w, w_in_b, ret_a_fwd, ret_a_bwd, ret_norm_w, diff_q_norm_w, diff_k_norm_w,
           lam_q1, lam_k1, lam_q2, lam_k2, diff_subln_w, rel_bias, w_out_b, norm2_w, rw_pad, rb_pad,
           wgu_b, b_gate_up, wd_b, b_down):
    batch, seq_len, d = x.shape
    tt, c, t, rows, ct = _tiles(seq_len)
    x2 = x.reshape(batch * seq_len, d)
    shift1, scale1, gate1, shift2, scale2, gate2 = [mod[:, n][:, None, :] for n in range(6)]

    pos = jnp.arange(seq_len, dtype=F32)
    inv_freq = ROPE_BASE ** (-jnp.arange(0, HEAD_DIM, 2, dtype=F32) / HEAD_DIM)
    ang = pos[:, None] * inv_freq[None, :]
    cos_t = jnp.tile(jnp.cos(ang), (1, LANES // (HEAD_DIM // 2)))
    sin_h = jnp.sin(ang)
    sin_t = jnp.tile(jnp.concatenate([-sin_h, sin_h], axis=1), (1, LANES // HEAD_DIM))
    qnw = jnp.tile(diff_q_norm_w.astype(F32), HALF_WIDTH // HEAD_DIM)[None, :]
    knw = jnp.tile(diff_k_norm_w.astype(F32), HALF_WIDTH // HEAD_DIM)[None, :]
    r = jnp.arange(HALF_WIDTH)
    bd = jnp.where((r[:, None] // HEAD_DIM) == (r[None, :] // HEAD_DIM), 1.0 / HEAD_DIM, 0.0).astype(BF16)

    rq, rk, rv, rg, dq, dk, dv = _inproj(x2, scale1, shift1, norm1_w[None, :], w_in_b, cos_t, sin_t, qnw, knw, bd,
                                         seq_len, tt)

    tabs = _retention_tables(ret_a_fwd, ret_a_bwd, c)
    ret_out = _retention(rq, rk, rv, rg, tabs, ret_norm_w.reshape(1, HALF_WIDTH).astype(F32), batch, seq_len, c)

    lam = (jnp.exp(jnp.sum(lam_q1.astype(F32) * lam_k1.astype(F32)))
           - jnp.exp(jnp.sum(lam_q2.astype(F32) * lam_k2.astype(F32))) + lam_init)
    subln = (diff_subln_w.astype(F32) * (1.0 - lam_init))[None, :]
    diff_out = _attention(dq, dk, dv, _bias_tables(rel_bias, t), lam[None].astype(F32), subln, batch, seq_len, t)

    x1, h2, idx_pad, gates_pad = _outproj(ret_out, diff_out, x2, gate1, scale2, shift2, norm2_w[None, :], w_out_b,
                                          rw_pad, rb_pad, seq_len, tt)
    block_expert, n_valid, row_src, row_dst = _routing(idx_pad[:, :TOP_K], rows)
    y4 = _experts(h2, block_expert, n_valid, row_src, row_dst, wgu_b, b_gate_up, wd_b, b_down, rows)
    y = _combine(x1, gate2, gates_pad, y4.reshape(TOP_K, batch * seq_len, d), seq_len, ct)
    return y.reshape(batch, seq_len, d)


def kernel(x_prompt, x_sample, c_prompt, c_sample, w_ada, b_ada, norm1_w, w_in, ret_a_fwd, ret_a_bwd, ret_norm_w,
           diff_q_norm_w, diff_k_norm_w, lam_q1, lam_k1, lam_q2, lam_k2, diff_subln_w, rel_bias, w_out, norm2_w,
           router_w, router_b, w_gate_up, b_gate_up, w_down, b_down):
    depth = w_ada.shape[0]
    d = x_prompt.shape[-1]
    n_prompt = c_prompt.shape[0]
    n_cond = n_prompt + c_sample.shape[0]
    c_all = jnp.concatenate([c_prompt, c_sample], axis=0).astype(F32)
    c_pad = jnp.pad(c_all, ((0, -n_cond % 8), (0, 0)))
    xs = [x_prompt, x_sample]
    for l in range(depth):
        lam_init = 0.8 - 0.6 * math.exp(-0.3 * l)
        mod = _ada(c_pad, w_ada[l], b_ada[l])[:n_cond].reshape(n_cond, 6, d)
        rw_pad = jnp.pad(router_w[l].astype(F32), ((0, 0), (0, LANES - N_EXPERTS)))
        rb_pad = jnp.pad(router_b[l].astype(F32), (0, LANES - N_EXPERTS), constant_values=-jnp.inf)[None, :]
        shared = (norm1_w[l], w_in[l].astype(BF16), ret_a_fwd[l], ret_a_bwd[l], ret_norm_w[l], diff_q_norm_w[l],
                  diff_k_norm_w[l], lam_q1[l], lam_k1[l], lam_q2[l], lam_k2[l], diff_subln_w[l], rel_bias,
                  w_out[l].astype(BF16), norm2_w[l], rw_pad, rb_pad,
                  w_gate_up[l].astype(BF16), b_gate_up[l][:, None, :], w_down[l].astype(BF16), b_down[l][:, None, :])
        xs = [_trunk(xs[0], mod[:n_prompt], lam_init, *shared),
              _trunk(xs[1], mod[n_prompt:], lam_init, *shared)]
    return (xs[0], xs[1])
```

```python
import functools
import math

import jax
import jax.numpy as jnp
import numpy as np
from jax import lax
from jax.experimental import pallas as pl
from jax.experimental.pallas import tpu as pltpu

F32 = jnp.float32
BF16 = jnp.bfloat16
HIGHEST = lax.Precision.HIGHEST

LANES = 128
SUBLANES = 8
HEAD_DIM = 64
RET_HEADS = 8
DIFF_HEADS = 4
HALF_WIDTH = 512
N_PAIRS = HALF_WIDTH // LANES
ROPE_BASE = 10000.0
NUM_BUCKETS = 32
MAX_DISTANCE = 128
N_EXPERTS = 32
TOP_K = 4
SWIGLU_LIMIT = 7.0
SWIGLU_ALPHA = 1.702
EPS = 1e-6
LOG2E = math.log2(math.e)
ATTN_ROWS = 128
ATTN_UNROLL = 4
VMEM_LIMIT = 56 * 1024 * 1024


def _cparams(sem):
    return pltpu.CompilerParams(dimension_semantics=sem, vmem_limit_bytes=VMEM_LIMIT)


def _ada_kernel(c_ref, w_ref, b_ref, o_ref):
    c = c_ref[...]
    a = c * jax.nn.sigmoid(c)
    o_ref[...] = jnp.dot(a, w_ref[...], preferred_element_type=F32, precision=HIGHEST) + b_ref[...]


def _ada(c_pad, w_ada, b_ada):
    rows, d = c_pad.shape
    n = w_ada.shape[1]
    tn = d
    return pl.pallas_call(
        _ada_kernel,
        grid=(n // tn,),
        in_specs=[pl.BlockSpec((rows, d), lambda j: (0, 0)),
                  pl.BlockSpec((d, tn), lambda j: (0, j)),
                  pl.BlockSpec((1, tn), lambda j: (0, j))],
        out_specs=pl.BlockSpec((rows, tn), lambda j: (0, j)),
        out_shape=jax.ShapeDtypeStruct((rows, n), F32),
        compiler_params=_cparams(("parallel",)),
    )(c_pad, w_ada, b_ada.reshape(1, n))


def _rotate_half(xg):
    lane = lax.broadcasted_iota(jnp.int32, xg.shape, 1)
    first = (lane % HEAD_DIM) < (HEAD_DIM // 2)
    return jnp.where(first, pltpu.roll(xg, LANES - HEAD_DIM // 2, 1), pltpu.roll(xg, HEAD_DIM // 2, 1))


def _inproj_kernel(x_ref, sc_ref, sh_ref, nw_ref, w_ref, cos_ref, sin_ref, qnw_ref, knw_ref, bd_ref,
                   rq_ref, rk_ref, rv_ref, rg_ref, dq_ref, dk_ref, dv_ref):
    x = x_ref[...]
    ms = jnp.mean(x * x, axis=-1, keepdims=True)
    h = x * lax.rsqrt(ms + EPS) * nw_ref[...]
    h = (h * (1.0 + sc_ref[...]) + sh_ref[...]).astype(BF16)

    def piece(n):
        return jnp.dot(h, w_ref[:, n * HALF_WIDTH:(n + 1) * HALF_WIDTH], preferred_element_type=F32)

    cos = cos_ref[...]
    sin = sin_ref[...]

    def rotary(p, out_ref, scale):
        for g in range(N_PAIRS):
            xg = p[:, g * LANES:(g + 1) * LANES]
            y = xg * cos + _rotate_half(xg) * sin
            out_ref[:, g * LANES:(g + 1) * LANES] = (y * scale).astype(BF16)

    def head_norm(p, w, out_ref, scale):
        msq = jnp.dot((p * p).astype(BF16), bd_ref[...], preferred_element_type=F32)
        out_ref[...] = (p * lax.rsqrt(msq + EPS) * w * scale).astype(BF16)

    rotary(piece(0), rq_ref, 1.0)
    rotary(piece(1), rk_ref, HEAD_DIM ** -0.5)
    rv_ref[...] = piece(2).astype(BF16)
    rg_ref[...] = piece(3).astype(BF16)
    head_norm(piece(4), qnw_ref[...], dq_ref, HEAD_DIM ** -0.5 * LOG2E)
    head_norm(piece(5), knw_ref[...], dk_ref, 1.0)
    dv = piece(6).astype(BF16)
    ones = jnp.ones((dv.shape[0], LANES), BF16)
    for hd in range(DIFF_HEADS):
        dv_ref[:, 2 * hd * LANES:(2 * hd + 1) * LANES] = dv[:, hd * LANES:(hd + 1) * LANES]
        dv_ref[:, (2 * hd + 1) * LANES:(2 * hd + 2) * LANES] = ones


def _inproj(x2, scale1, shift1, norm_w, w_in, cos_t, sin_t, qnw, knw, bd, seq_len, tt):
    t_total, d = x2.shape
    nps = seq_len // tt
    n_in = w_in.shape[1]
    row = lambda i: (i, 0)
    mod = lambda i: (i // nps, 0, 0)
    pos = lambda i: (i % nps, 0)
    fixed = lambda i: (0, 0)
    out_sd = jax.ShapeDtypeStruct((t_total, HALF_WIDTH), BF16)
    return pl.pallas_call(
        _inproj_kernel,
        grid=(t_total // tt,),
        in_specs=[pl.BlockSpec((tt, d), row),
                  pl.BlockSpec((None, 1, d), mod),
                  pl.BlockSpec((None, 1, d), mod),
                  pl.BlockSpec((1, d), fixed),
                  pl.BlockSpec((d, n_in), fixed),
                  pl.BlockSpec((tt, LANES), pos),
                  pl.BlockSpec((tt, LANES), pos),
                  pl.BlockSpec((1, HALF_WIDTH), fixed),
                  pl.BlockSpec((1, HALF_WIDTH), fixed),
                  pl.BlockSpec((HALF_WIDTH, HALF_WIDTH), fixed)],
        out_specs=[pl.BlockSpec((tt, HALF_WIDTH), row)] * 6 + [pl.BlockSpec((tt, 2 * HALF_WIDTH), row)],
        out_shape=[out_sd] * 6 + [jax.ShapeDtypeStruct((t_total, 2 * HALF_WIDTH), BF16)],
        compiler_params=_cparams(("parallel",)),
    )(x2, scale1, shift1, norm_w, w_in, cos_t, sin_t, qnw, knw, bd)


def _kv_update(state_ref, p, k, v, kdec, cdec, bdmask):
    kd = (k.astype(F32) * kdec).astype(BF16)
    kv = lax.dot_general(kd, v, (((0,), (0,)), ((), ())), preferred_element_type=F32)
    state_ref[p] = state_ref[p] * cdec + kv * bdmask


def _ret_state_kernel(k_ref, v_ref, kdec_ref, cdec_ref, bdmask_ref, sb_ref, state_ref):
    @pl.when(pl.program_id(1) == 0)
    def _():
        state_ref[...] = jnp.zeros_like(state_ref)

    sb_ref[...] = state_ref[...]
    for p in range(N_PAIRS):
        sl = slice(p * LANES, (p + 1) * LANES)
        _kv_update(state_ref, p, k_ref[:, sl], v_ref[:, sl], kdec_ref[:, sl], cdec_ref[:, sl], bdmask_ref[...])


def _ret_main_kernel(q_ref, k_ref, v_ref, g_ref, dmat_ref, qdf_ref, qdb_ref, kdf_ref, cdf_ref, bdmask_ref,
                     nw_ref, sb_ref, o_ref, state_ref):
    @pl.when(pl.program_id(1) == 0)
    def _():
        state_ref[...] = jnp.zeros_like(state_ref)

    c = q_ref.shape[0]
    lane = lax.broadcasted_iota(jnp.int32, (c, LANES), 1)
    lo = lane < HEAD_DIM
    for p in range(N_PAIRS):
        sl = slice(p * LANES, (p + 1) * LANES)
        q = q_ref[:, sl]
        k = k_ref[:, sl]
        v = v_ref[:, sl]
        qf = q.astype(F32)
        acc = jnp.dot((qf * qdf_ref[:, sl]).astype(BF16), state_ref[p].astype(BF16), preferred_element_type=F32)
        acc += jnp.dot((qf * qdb_ref[:, sl]).astype(BF16), sb_ref[p].astype(BF16), preferred_element_type=F32)
        for hh in range(2):
            sel = lo if hh == 0 else jnp.logical_not(lo)
            qm = jnp.where(sel, q, jnp.zeros_like(q))
            vm = jnp.where(sel, v, jnp.zeros_like(v))
            s = lax.dot_general(qm, k, (((1,), (1,)), ((), ())), preferred_element_type=F32)
            w = (s * dmat_ref[2 * p + hh]).astype(BF16)
            acc += jnp.dot(w, vm, preferred_element_type=F32)
        _kv_update(state_ref, p, k, v, kdf_ref[:, sl], cdf_ref[:, sl], bdmask_ref[...])
        sq = acc * acc
        ms_lo = jnp.sum(jnp.where(lo, sq, 0.0), axis=-1, keepdims=True)
        ms_hi = jnp.sum(jnp.where(lo, 0.0, sq), axis=-1, keepdims=True)
        ms = jnp.where(lo, ms_lo, ms_hi) * (1.0 / HEAD_DIM)
        y = acc * lax.rsqrt(ms + EPS) * nw_ref[:, sl]
        gf = g_ref[:, sl].astype(F32)
        o_ref[:, sl] = (gf * jax.nn.sigmoid(gf) * y).astype(BF16)


def _retention(rq, rk, rv, rg, tabs, ret_nw, batch, seq_len, c):
    t_total = rq.shape[0]
    nc = seq_len // c
    dmat, qdf, qdb, kdf, kdb, cdf, cdb, bdmask = tabs
    fixed2 = lambda b, i: (0, 0)
    rev = lambda b, i: (b * nc + nc - 1 - i, 0)
    fwd = lambda b, i: (b * nc + i, 0)
    tile = pl.BlockSpec((c, HALF_WIDTH), fwd)
    tile_rev = pl.BlockSpec((c, HALF_WIDTH), rev)
    tab = pl.BlockSpec((c, HALF_WIDTH), fixed2)
    vec = pl.BlockSpec((1, HALF_WIDTH), fixed2)
    mask = pl.BlockSpec((LANES, LANES), fixed2)
    state = pltpu.VMEM((N_PAIRS, LANES, LANES), F32)
    sb = pl.pallas_call(
        _ret_state_kernel,
        grid=(batch, nc),
        in_specs=[tile_rev, tile_rev, tab, vec, mask],
        out_specs=pl.BlockSpec((None, None, N_PAIRS, LANES, LANES), lambda b, i: (b, nc - 1 - i, 0, 0, 0)),
        out_shape=jax.ShapeDtypeStruct((batch, nc, N_PAIRS, LANES, LANES), F32),
        scratch_shapes=[state],
        compiler_params=_cparams(("parallel", "arbitrary")),
    )(rk, rv, kdb, cdb, bdmask)
    return pl.pallas_call(
        _ret_main_kernel,
        grid=(batch, nc),
        in_specs=[tile, tile, tile, tile,
                  pl.BlockSpec((RET_HEADS, c, c), lambda b, i: (0, 0, 0)),
                  tab, tab, tab, vec, mask, vec,
                  pl.BlockSpec((None, None, N_PAIRS, LANES, LANES), lambda b, i: (b, i, 0, 0, 0))],
        out_specs=tile,
        out_shape=jax.ShapeDtypeStruct((t_total, HALF_WIDTH), BF16),
        scratch_shapes=[state],
        compiler_params=_cparams(("parallel", "arbitrary")),
    )(rq, rk, rv, rg, dmat, qdf, qdb, kdf, cdf, bdmask, ret_nw, sb)


def _retention_tables(a_fwd, a_bwd, c):
    lg_f = jnp.log1p(-jnp.exp(a_fwd.astype(F32)))
    lg_b = jnp.log1p(-jnp.exp(a_bwd.astype(F32)))
    pos = jnp.arange(c, dtype=F32)
    diff = pos[:, None] - pos[None, :]
    dmat = jnp.where(diff[None] >= 0,
                     jnp.exp(jnp.maximum(diff, 0.0)[None] * lg_f[:, None, None]),
                     jnp.exp(jnp.maximum(-diff, 0.0)[None] * lg_b[:, None, None]))
    lane_f = jnp.repeat(lg_f, HEAD_DIM)[None, :]
    lane_b = jnp.repeat(lg_b, HEAD_DIM)[None, :]
    qdf = jnp.exp((pos + 1.0)[:, None] * lane_f)
    kdf = jnp.exp((c - 1.0 - pos)[:, None] * lane_f)
    qdb = jnp.exp((c - pos)[:, None] * lane_b)
    kdb = jnp.exp(pos[:, None] * lane_b)
    cdf = jnp.exp(c * lane_f)
    cdb = jnp.exp(c * lane_b)
    r = jnp.arange(LANES)
    bdmask = ((r[:, None] // HEAD_DIM) == (r[None, :] // HEAD_DIM)).astype(F32)
    return dmat, qdf, qdb, kdf, kdb, cdf, cdb, bdmask


def _attn_kernel(lam_ref, q_ref, k_ref, v_ref, bias_ref, sw_ref, o_ref, qq_ref, m_ref, acc_ref):
    i = pl.program_id(2)
    t = q_ref.shape[0]
    nk = k_ref.shape[0] // t
    q = q_ref[...]
    lane = lax.broadcasted_iota(jnp.int32, q.shape, 1)
    zero = jnp.zeros_like(q)
    qq_ref[...] = jnp.concatenate([jnp.where(lane < HEAD_DIM, q, zero), jnp.where(lane < HEAD_DIM, zero, q)], axis=0)
    m_ref[...] = jnp.full_like(m_ref, -jnp.inf)
    acc_ref[...] = jnp.zeros_like(acc_ref)

    def tile(j):
        keys = pl.ds(pl.multiple_of(j * t, t), t)
        which = jnp.clip(j - i, -2, 2) + 2
        for r in range(0, 2 * t, ATTN_ROWS):
            rows = slice(r, r + ATTN_ROWS)
            s = lax.dot_general(qq_ref[rows, :], k_ref[keys, :], (((1,), (1,)), ((), ())),
                                preferred_element_type=F32)
            s = s + bias_ref[which, r % t:r % t + ATTN_ROWS, :]
            m_prev = m_ref[rows, :]
            m_new = jnp.maximum(m_prev, jnp.max(s, axis=1, keepdims=True))
            alpha = jnp.exp2(m_prev - m_new)
            p = jnp.exp2(s - jnp.concatenate([m_new] * (t // LANES), axis=1)).astype(BF16)
            pv = jnp.dot(p, v_ref[keys, :], preferred_element_type=F32)
            acc_ref[rows, :] = jnp.concatenate([alpha, alpha], axis=1) * acc_ref[rows, :] + pv
            m_ref[rows, :] = m_new

    def step(jj, carry):
        for u in range(ATTN_UNROLL):
            tile(jj * ATTN_UNROLL + u)
        return carry

    lax.fori_loop(0, nk // ATTN_UNROLL, step, 0)

    acc = acc_ref[...]
    o = acc[:, :LANES] / acc[:, LANES:]
    att = o[:t] - lam_ref[0] * o[t:]
    msq = jnp.mean(att * att, axis=-1, keepdims=True)
    o_ref[...] = (att * lax.rsqrt(msq + EPS) * sw_ref[...]).astype(BF16)


def _attention(dq, dk, dv, bias_tiles, lam, subln_w, batch, seq_len, t):
    t_total = dq.shape[0]
    nq = seq_len // t
    assert nq % ATTN_UNROLL == 0 and t % ATTN_ROWS == 0
    grid_spec = pltpu.PrefetchScalarGridSpec(
        num_scalar_prefetch=1,
        grid=(batch, DIFF_HEADS, nq),
        in_specs=[pl.BlockSpec((t, LANES), lambda b, h, i, s: (b * nq + i, h)),
                  pl.BlockSpec((seq_len, LANES), lambda b, h, i, s: (b, h)),
                  pl.BlockSpec((seq_len, 2 * LANES), lambda b, h, i, s: (b, h)),
                  pl.BlockSpec((None, 5, t, t), lambda b, h, i, s: (h, 0, 0, 0)),
                  pl.BlockSpec((1, LANES), lambda b, h, i, s: (0, 0))],
        out_specs=pl.BlockSpec((t, LANES), lambda b, h, i, s: (b * nq + i, h)),
        scratch_shapes=[pltpu.VMEM((2 * t, LANES), BF16), pltpu.VMEM((2 * t, LANES), F32),
                        pltpu.VMEM((2 * t, 2 * LANES), F32)],
    )
    return pl.pallas_call(
        _attn_kernel,
        grid_spec=grid_spec,
        out_shape=jax.ShapeDtypeStruct((t_total, HALF_WIDTH), BF16),
        compiler_params=_cparams(("parallel", "parallel", "parallel")),
    )(lam, dq, dk, dv, bias_tiles, subln_w)


def _t5_bucket(rel):
    nb = NUM_BUCKETS // 2
    max_exact = nb // 2
    n = jnp.abs(rel)
    base = jnp.where(rel > 0, nb, 0)
    nf = jnp.maximum(n, 1).astype(F32)
    large = max_exact + (jnp.log(nf / max_exact) / math.log(MAX_DISTANCE / max_exact)
                         * (nb - max_exact)).astype(jnp.int32)
    large = jnp.minimum(large, nb - 1)
    return base + jnp.where(n < max_exact, n, large)


def _bias_tables(rel_bias, t):
    assert t >= MAX_DISTANCE
    table = rel_bias.astype(F32) * LOG2E
    qq = jnp.arange(t, dtype=jnp.int32)[:, None]
    kk = jnp.arange(t, dtype=jnp.int32)[None, :]
    bucket = jnp.stack([_t5_bucket(d * t + kk - qq) for d in (-2, -1, 0, 1, 2)], axis=0)
    onehot = (bucket[..., None] == jnp.arange(NUM_BUCKETS, dtype=jnp.int32)).astype(F32)
    return jnp.einsum('dqkn,nh->hdqk', onehot, table, precision=HIGHEST)


def _store_token_tiles(ref, value):
    n = value.shape[0]
    for s in range(SUBLANES):
        ref[pl.ds(s, n, stride=SUBLANES), :] = value[:, s * LANES:(s + 1) * LANES]


def _load_token_tiles(ref):
    n = ref.shape[0] // SUBLANES
    return jnp.concatenate([ref[pl.ds(s, n, stride=SUBLANES), :] for s in range(SUBLANES)], axis=1)


def _outproj_kernel(ret_ref, dif_ref, x_ref, g1_ref, sc2_ref, sh2_ref, nw2_ref, wo_ref, rw_ref, rb_ref,
                    x1_ref, h2_ref, idx_ref, gate_ref):
    mix = jnp.dot(ret_ref[...], wo_ref[:HALF_WIDTH, :], preferred_element_type=F32)
    mix += jnp.dot(dif_ref[...], wo_ref[HALF_WIDTH:, :], preferred_element_type=F32)
    x1 = x_ref[...] + g1_ref[...] * mix
    x1_ref[...] = x1
    ms = jnp.mean(x1 * x1, axis=-1, keepdims=True)
    h2 = x1 * lax.rsqrt(ms + EPS) * nw2_ref[...]
    h2 = h2 * (1.0 + sc2_ref[...]) + sh2_ref[...]
    _store_token_tiles(h2_ref, h2)
    logits = jnp.dot(h2, rw_ref[...], preferred_element_type=F32, precision=HIGHEST) + rb_ref[...]
    lane = lax.broadcasted_iota(jnp.int32, logits.shape, 1)
    vals, idxs = [], []
    for _ in range(TOP_K):
        m = jnp.max(logits, axis=-1, keepdims=True)
        am = jnp.min(jnp.where(logits == m, lane, LANES), axis=-1, keepdims=True)
        vals.append(m)
        idxs.append(am)
        logits = jnp.where(lane == am, -jnp.inf, logits)
    es = [jnp.exp(v - vals[0]) for v in vals]
    den = es[0] + es[1] + es[2] + es[3]
    idx_out = jnp.zeros(lane.shape, jnp.int32)
    gate_out = jnp.zeros(lane.shape, F32)
    for kk in range(TOP_K):
        idx_out = jnp.where(lane == kk, idxs[kk], idx_out)
        gate_out = jnp.where(lane == kk, es[kk] / den, gate_out)
    idx_ref[...] = idx_out
    gate_ref[...] = gate_out


def _outproj(ret_out, diff_out, x2, gate1, scale2, shift2, norm2_w, w_out, rw_pad, rb_pad, seq_len, tt):
    t_total, d = x2.shape
    nps = seq_len // tt
    row = lambda i: (i, 0)
    mod = lambda i: (i // nps, 0, 0)
    fixed = lambda i: (0, 0)
    return pl.pallas_call(
        _outproj_kernel,
        grid=(t_total // tt,),
        in_specs=[pl.BlockSpec((tt, HALF_WIDTH), row),
                  pl.BlockSpec((tt, HALF_WIDTH), row),
                  pl.BlockSpec((tt, d), row),
                  pl.BlockSpec((None, 1, d), mod),
                  pl.BlockSpec((None, 1, d), mod),
                  pl.BlockSpec((None, 1, d), mod),
                  pl.BlockSpec((1, d), fixed),
                  pl.BlockSpec((2 * HALF_WIDTH, d), fixed),
                  pl.BlockSpec((d, LANES), fixed),
                  pl.BlockSpec((1, LANES), fixed)],
        out_specs=[pl.BlockSpec((tt, d), row), pl.BlockSpec((tt * SUBLANES, LANES), row),
                   pl.BlockSpec((tt, LANES), row), pl.BlockSpec((tt, LANES), row)],
        out_shape=[jax.ShapeDtypeStruct((t_total, d), F32), jax.ShapeDtypeStruct((t_total * SUBLANES, LANES), F32),
                   jax.ShapeDtypeStruct((t_total, LANES), jnp.int32),
                   jax.ShapeDtypeStruct((t_total, LANES), F32)],
        compiler_params=_cparams(("parallel",)),
    )(ret_out, diff_out, x2, gate1, scale2, shift2, norm2_w, w_out, rw_pad, rb_pad)


def _expert_kernel(be_ref, act_ref, src_ref, src_next_ref, dst_prev_ref, h2_hbm, wgu_ref, bgu_ref, wd_ref, bd_ref,
                   out_hbm, x0, x1, y0, y1, sem_in, sem_out):
    i = pl.program_id(0)
    rows = x0.shape[0] // SUBLANES
    d_ff = wd_ref.shape[1]
    prev_active = jnp.logical_and(i > 0, act_ref[jnp.maximum(i - 1, 0)] == 1)

    def gather_copy(src, r, xbuf, sem):
        return pltpu.make_async_copy(h2_hbm.at[pl.ds(pl.multiple_of(src[r], SUBLANES), SUBLANES)],
                                     xbuf.at[pl.ds(r * SUBLANES, SUBLANES)], sem)

    @pl.when(i == 0)
    def _():
        y1[...] = jnp.zeros_like(y1)
        for r in range(rows):
            gather_copy(src_ref, r, x0, sem_in.at[0]).start()

    def body(xcur, xnext, ycur, yprev, sem_cur, sem_next):
        pltpu.make_async_copy(h2_hbm.at[pl.ds(0, rows * SUBLANES)], xcur, sem_cur).wait()

        @pl.when(prev_active)
        def _():
            pltpu.make_async_copy(ycur, out_hbm.at[pl.ds(0, rows * SUBLANES)], sem_out).wait()

        @pl.when(act_ref[i] == 1)
        def _():
            for r in range(rows):
                gather_copy(src_next_ref, r, xnext, sem_next).start()
            for r in range(rows):
                pltpu.make_async_copy(yprev.at[pl.ds(r * SUBLANES, SUBLANES)],
                                      out_hbm.at[pl.ds(pl.multiple_of(dst_prev_ref[r], SUBLANES), SUBLANES)],
                                      sem_out).start()
            x = _load_token_tiles(xcur).astype(BF16)
            gu = jnp.dot(x, wgu_ref[0], preferred_element_type=F32) + bgu_ref[0]
            glu = jnp.minimum(gu[:, :d_ff], SWIGLU_LIMIT)
            lin = jnp.clip(gu[:, d_ff:], -SWIGLU_LIMIT, SWIGLU_LIMIT)
            hidden = glu * jax.nn.sigmoid(SWIGLU_ALPHA * glu) * (lin + 1.0)
            _store_token_tiles(ycur, jnp.dot(hidden.astype(BF16), wd_ref[0], preferred_element_type=F32) + bd_ref[0])

    rows_requested = jnp.logical_or(i == 0, prev_active)

    @pl.when(jnp.logical_and(rows_requested, i % 2 == 0))
    def _():
        body(x0, x1, y0, y1, sem_in.at[0], sem_in.at[1])

    @pl.when(jnp.logical_and(rows_requested, i % 2 == 1))
    def _():
        body(x1, x0, y1, y0, sem_in.at[1], sem_in.at[0])


def _experts(h2, block_expert, active, row_src, dst_prev, wgu, bgu, wd, bd, rows):
    t_total = h2.shape[0] // SUBLANES
    n_blocks = block_expert.shape[0]
    d, d_ff2 = wgu.shape[1:]
    assert d == SUBLANES * LANES
    last = n_blocks - 1
    buf = pltpu.VMEM((rows * SUBLANES, LANES), F32)
    grid_spec = pltpu.PrefetchScalarGridSpec(
        num_scalar_prefetch=2,
        grid=(n_blocks,),
        in_specs=[pl.BlockSpec((rows,), lambda i, be, act: (i,), memory_space=pltpu.SMEM),
                  pl.BlockSpec((rows,), lambda i, be, act: (jnp.minimum(i + 1, last),), memory_space=pltpu.SMEM),
                  pl.BlockSpec((rows,), lambda i, be, act: (i,), memory_space=pltpu.SMEM),
                  pl.BlockSpec(memory_space=pl.ANY),
                  pl.BlockSpec((1, d, d_ff2), lambda i, be, act: (be[i], 0, 0)),
                  pl.BlockSpec((1, 1, d_ff2), lambda i, be, act: (be[i], 0, 0)),
                  pl.BlockSpec((1, d_ff2 // 2, d), lambda i, be, act: (be[i], 0, 0)),
                  pl.BlockSpec((1, 1, d), lambda i, be, act: (be[i], 0, 0))],
        out_specs=pl.BlockSpec(memory_space=pl.ANY),
        scratch_shapes=[buf, buf, buf, buf, pltpu.SemaphoreType.DMA((2,)), pltpu.SemaphoreType.DMA],
    )
    return pl.pallas_call(
        _expert_kernel,
        grid_spec=grid_spec,
        out_shape=jax.ShapeDtypeStruct(((TOP_K * t_total + rows) * SUBLANES, LANES), F32),
        compiler_params=_cparams(("arbitrary",)),
    )(block_expert, active, row_src, row_src, dst_prev, h2, wgu, bgu, wd, bd)


def _routing(top_idx, rows):
    t_total = top_idx.shape[0]
    onehot = (top_idx[:, :, None] == jnp.arange(N_EXPERTS, dtype=jnp.int32)).astype(jnp.int32)
    member = jnp.sum(onehot, axis=1)
    before = jnp.cumsum(member, axis=0) - member
    rank = jnp.take_along_axis(before, top_idx, axis=1)
    counts = jnp.sum(member, axis=0)
    padded = (counts + rows - 1) // rows * rows
    ends = jnp.cumsum(padded)
    starts = ends - padded
    dest = (starts[top_idx] + rank).reshape(-1)
    n_blocks = (t_total * TOP_K + N_EXPERTS * (rows - 1)) // rows + 2
    n_rows = n_blocks * rows
    n_slots = TOP_K * t_total
    tok = jnp.repeat(jnp.arange(t_total, dtype=jnp.int32), TOP_K)
    slot = jnp.tile(jnp.arange(TOP_K, dtype=jnp.int32), t_total)
    landing = n_slots + jnp.arange(n_rows, dtype=jnp.int32) % rows
    row_dst = landing.at[dest].set(slot * t_total + tok, unique_indices=True)
    row_src = jnp.where(row_dst < n_slots, row_dst % t_total, 0)
    dst_prev = jnp.concatenate([landing[:rows], row_dst[:-rows]])
    block_start = jnp.arange(n_blocks, dtype=jnp.int32) * rows
    block_expert = jnp.minimum(jnp.searchsorted(ends, block_start, side='right'), N_EXPERTS - 1).astype(jnp.int32)
    has_rows = block_start < ends[-1]
    active = jnp.logical_or(has_rows, jnp.concatenate([has_rows[:1], has_rows[:-1]])).astype(jnp.int32)
    return block_expert, active, row_src * SUBLANES, dst_prev * SUBLANES


def _combine_kernel(x1_ref, g2_ref, gate_ref, *refs):
    y_refs, o_ref = refs[:TOP_K], refs[TOP_K]
    gates = gate_ref[...]
    moe = gates[:, 0:1] * _load_token_tiles(y_refs[0])
    for kk in range(1, TOP_K):
        moe += gates[:, kk:kk + 1] * _load_token_tiles(y_refs[kk])
    o_ref[...] = x1_ref[...] + g2_ref[...] * moe


def _combine(x1, gate2, gates, y4, seq_len, tt):
    t_total, d = x1.shape
    nps = seq_len // tt
    n_tiles = t_total // tt
    slot_specs = [pl.BlockSpec((tt * SUBLANES, LANES), functools.partial(lambda kk, i: (kk * n_tiles + i, 0), kk))
                  for kk in range(TOP_K)]
    return pl.pallas_call(
        _combine_kernel,
        grid=(n_tiles,),
        in_specs=[pl.BlockSpec((tt, d), lambda i: (i, 0)),
                  pl.BlockSpec((None, 1, d), lambda i: (i // nps, 0, 0)),
                  pl.BlockSpec((tt, LANES), lambda i: (i, 0))] + slot_specs,
        out_specs=pl.BlockSpec((tt, d), lambda i: (i, 0)),
        out_shape=jax.ShapeDtypeStruct((t_total, d), F32),
        compiler_params=_cparams(("parallel",)),
    )(x1, gate2, gates, *([y4] * TOP_K))


def _tiles(seq_len):
    token_tile = min(512, seq_len)
    ret_chunk = min(256, seq_len)
    attn_tile = min(512, seq_len)
    expert_rows = 256
    combine_tile = min(256, seq_len)
    return token_tile, ret_chunk, attn_tile, expert_rows, combine_tile


def _trunk(x, mod, lam_init, norm1_w, w_in_b, ret_a_fwd, ret_a_bwd, ret_norm_w, diff_q_norm_w, diff_k_norm_w,
           lam_q1, lam_k1, lam_q2, lam_k2, diff_subln_w, rel_bias, w_out_b, norm2_w, rw_pad, rb_pad,
           wgu_b, b_gate_up, wd_b, b_down):
    batch, seq_len, d = x.shape
    tt, c, t, rows, ct = _tiles(seq_len)
    x2 = x.reshape(batch * seq_len, d)
    shift1, scale1, gate1, shift2, scale2, gate2 = [mod[:, n][:, None, :] for n in range(6)]

    pos = jnp.arange(seq_len, dtype=F32)
    inv_freq = ROPE_BASE ** (-jnp.arange(0, HEAD_DIM, 2, dtype=F32) / HEAD_DIM)
    ang = pos[:, None] * inv_freq[None, :]
    cos_t = jnp.tile(jnp.cos(ang), (1, LANES // (HEAD_DIM // 2)))
    sin_h = jnp.sin(ang)
    sin_t = jnp.tile(jnp.concatenate([-sin_h, sin_h], axis=1), (1, LANES // HEAD_DIM))
    qnw = jnp.tile(diff_q_norm_w.astype(F32), HALF_WIDTH // HEAD_DIM)[None, :]
    knw = jnp.tile(diff_k_norm_w.astype(F32), HALF_WIDTH // HEAD_DIM)[None, :]
    r = jnp.arange(HALF_WIDTH)
    bd = jnp.where((r[:, None] // HEAD_DIM) == (r[None, :] // HEAD_DIM), 1.0 / HEAD_DIM, 0.0).astype(BF16)

    rq, rk, rv, rg, dq, dk, dv = _inproj(x2, scale1, shift1, norm1_w[None, :], w_in_b, cos_t, sin_t, qnw, knw, bd,
                                         seq_len, tt)

    tabs = _retention_tables(ret_a_fwd, ret_a_bwd, c)
    ret_out = _retention(rq, rk, rv, rg, tabs, ret_norm_w.reshape(1, HALF_WIDTH).astype(F32), batch, seq_len, c)

    lam = (jnp.exp(jnp.sum(lam_q1.astype(F32) * lam_k1.astype(F32)))
           - jnp.exp(jnp.sum(lam_q2.astype(F32) * lam_k2.astype(F32))) + lam_init)
    subln = (diff_subln_w.astype(F32) * (1.0 - lam_init))[None, :]
    diff_out = _attention(dq, dk, dv, _bias_tables(rel_bias, t), lam[None].astype(F32), subln, batch, seq_len, t)

    x1, h2, idx_pad, gates_pad = _outproj(ret_out, diff_out, x2, gate1, scale2, shift2, norm2_w[None, :], w_out_b,
                                          rw_pad, rb_pad, seq_len, tt)
    block_expert, active, row_src, dst_prev = _routing(idx_pad[:, :TOP_K], rows)
    y4 = _experts(h2, block_expert, active, row_src, dst_prev, wgu_b, b_gate_up, wd_b, b_down, rows)
    y = _combine(x1, gate2, gates_pad, y4, seq_len, ct)
    return y.reshape(batch, seq_len, d)


def kernel(x_prompt, x_sample, c_prompt, c_sample, w_ada, b_ada, norm1_w, w_in, ret_a_fwd, ret_a_bwd, ret_norm_w,
           diff_q_norm_w, diff_k_norm_w, lam_q1, lam_k1, lam_q2, lam_k2, diff_subln_w, rel_bias, w_out, norm2_w,
           router_w, router_b, w_gate_up, b_gate_up, w_down, b_down):
    depth = w_ada.shape[0]
    d = x_prompt.shape[-1]
    n_prompt = c_prompt.shape[0]
    n_cond = n_prompt + c_sample.shape[0]
    c_all = jnp.concatenate([c_prompt, c_sample], axis=0).astype(F32)
    c_pad = jnp.pad(c_all, ((0, -n_cond % 8), (0, 0)))
    xs = [x_prompt, x_sample]
    for l in range(depth):
        lam_init = 0.8 - 0.6 * math.exp(-0.3 * l)
        mod = _ada(c_pad, w_ada[l], b_ada[l])[:n_cond].reshape(n_cond, 6, d)
        rw_pad = jnp.pad(router_w[l].astype(F32), ((0, 0), (0, LANES - N_EXPERTS)))
        rb_pad = jnp.pad(router_b[l].astype(F32), (0, LANES - N_EXPERTS), constant_values=-jnp.inf)[None, :]
        shared = (norm1_w[l], w_in[l].astype(BF16), ret_a_fwd[l], ret_a_bwd[l], ret_norm_w[l], diff_q_norm_w[l],
                  diff_k_norm_w[l], lam_q1[l], lam_k1[l], lam_q2[l], lam_k2[l], diff_subln_w[l], rel_bias,
                  w_out[l].astype(BF16), norm2_w[l], rw_pad, rb_pad,
                  w_gate_up[l].astype(BF16), b_gate_up[l][:, None, :], w_down[l].astype(BF16), b_down[l][:, None, :])
        xs = [_trunk(xs[0], mod[:n_prompt], lam_init, *shared),
              _trunk(xs[1], mod[n_prompt:], lam_init, *shared)]
    return (xs[0], xs[1])
```

```python
import functools
import math

import jax
import jax.numpy as jnp
import numpy as np
from jax import lax
from jax.experimental import pallas as pl
from jax.experimental.pallas import tpu as pltpu

F32 = jnp.float32
BF16 = jnp.bfloat16
HIGHEST = lax.Precision.HIGHEST

LANES = 128
SUBLANES = 8
HEAD_DIM = 64
RET_HEADS = 8
DIFF_HEADS = 4
HALF_WIDTH = 512
N_PAIRS = HALF_WIDTH // LANES
ROPE_BASE = 10000.0
NUM_BUCKETS = 32
MAX_DISTANCE = 128
N_EXPERTS = 32
TOP_K = 4
SWIGLU_LIMIT = 7.0
SWIGLU_ALPHA = 1.702
EPS = 1e-6
LOG2E = math.log2(math.e)
ATTN_ROWS = 128
ATTN_UNROLL = 8
VMEM_LIMIT = 56 * 1024 * 1024


def _cparams(sem):
    return pltpu.CompilerParams(dimension_semantics=sem, vmem_limit_bytes=VMEM_LIMIT)


def _ada_kernel(c_ref, w_ref, b_ref, o_ref):
    c = c_ref[...]
    a = c * jax.nn.sigmoid(c)
    o_ref[...] = jnp.dot(a, w_ref[...], preferred_element_type=F32, precision=HIGHEST) + b_ref[...]


def _ada(c_pad, w_ada, b_ada):
    rows, d = c_pad.shape
    n = w_ada.shape[1]
    tn = d
    return pl.pallas_call(
        _ada_kernel,
        grid=(n // tn,),
        in_specs=[pl.BlockSpec((rows, d), lambda j: (0, 0)),
                  pl.BlockSpec((d, tn), lambda j: (0, j)),
                  pl.BlockSpec((1, tn), lambda j: (0, j))],
        out_specs=pl.BlockSpec((rows, tn), lambda j: (0, j)),
        out_shape=jax.ShapeDtypeStruct((rows, n), F32),
        compiler_params=_cparams(("parallel",)),
    )(c_pad, w_ada, b_ada.reshape(1, n))


def _rotate_half(xg):
    lane = lax.broadcasted_iota(jnp.int32, xg.shape, 1)
    first = (lane % HEAD_DIM) < (HEAD_DIM // 2)
    return jnp.where(first, pltpu.roll(xg, LANES - HEAD_DIM // 2, 1), pltpu.roll(xg, HEAD_DIM // 2, 1))


def _inproj_kernel(x_ref, sc_ref, sh_ref, nw_ref, w_ref, cos_ref, sin_ref, qnw_ref, knw_ref, bd_ref,
                   rq_ref, rk_ref, rv_ref, rg_ref, dq_ref, dk_ref, dv_ref):
    x = x_ref[...]
    ms = jnp.mean(x * x, axis=-1, keepdims=True)
    h = x * lax.rsqrt(ms + EPS) * nw_ref[...]
    h = (h * (1.0 + sc_ref[...]) + sh_ref[...]).astype(BF16)

    def piece(n):
        return jnp.dot(h, w_ref[:, n * HALF_WIDTH:(n + 1) * HALF_WIDTH], preferred_element_type=F32)

    cos = cos_ref[...]
    sin = sin_ref[...]

    def rotary(p, out_ref, scale):
        for g in range(N_PAIRS):
            xg = p[:, g * LANES:(g + 1) * LANES]
            y = xg * cos + _rotate_half(xg) * sin
            out_ref[:, g * LANES:(g + 1) * LANES] = (y * scale).astype(BF16)

    def head_norm(p, w, out_ref, scale):
        msq = jnp.dot((p * p).astype(BF16), bd_ref[...], preferred_element_type=F32)
        out_ref[...] = (p * lax.rsqrt(msq + EPS) * w * scale).astype(BF16)

    rotary(piece(0), rq_ref, 1.0)
    rotary(piece(1), rk_ref, HEAD_DIM ** -0.5)
    rv_ref[...] = piece(2).astype(BF16)
    rg_ref[...] = piece(3).astype(BF16)
    head_norm(piece(4), qnw_ref[...], dq_ref, HEAD_DIM ** -0.5 * LOG2E)
    head_norm(piece(5), knw_ref[...], dk_ref, 1.0)
    dv = piece(6).astype(BF16)
    ones = jnp.ones((dv.shape[0], LANES), BF16)
    for hd in range(DIFF_HEADS):
        dv_ref[:, 2 * hd * LANES:(2 * hd + 1) * LANES] = dv[:, hd * LANES:(hd + 1) * LANES]
        dv_ref[:, (2 * hd + 1) * LANES:(2 * hd + 2) * LANES] = ones


def _inproj(x2, scale1, shift1, norm_w, w_in, cos_t, sin_t, qnw, knw, bd, seq_len, tt):
    t_total, d = x2.shape
    nps = seq_len // tt
    n_in = w_in.shape[1]
    row = lambda i: (i, 0)
    mod = lambda i: (i // nps, 0, 0)
    pos = lambda i: (i % nps, 0)
    fixed = lambda i: (0, 0)
    out_sd = jax.ShapeDtypeStruct((t_total, HALF_WIDTH), BF16)
    return pl.pallas_call(
        _inproj_kernel,
        grid=(t_total // tt,),
        in_specs=[pl.BlockSpec((tt, d), row),
                  pl.BlockSpec((None, 1, d), mod),
                  pl.BlockSpec((None, 1, d), mod),
                  pl.BlockSpec((1, d), fixed),
                  pl.BlockSpec((d, n_in), fixed),
                  pl.BlockSpec((tt, LANES), pos),
                  pl.BlockSpec((tt, LANES), pos),
                  pl.BlockSpec((1, HALF_WIDTH), fixed),
                  pl.BlockSpec((1, HALF_WIDTH), fixed),
                  pl.BlockSpec((HALF_WIDTH, HALF_WIDTH), fixed)],
        out_specs=[pl.BlockSpec((tt, HALF_WIDTH), row)] * 6 + [pl.BlockSpec((tt, 2 * HALF_WIDTH), row)],
        out_shape=[out_sd] * 6 + [jax.ShapeDtypeStruct((t_total, 2 * HALF_WIDTH), BF16)],
        compiler_params=_cparams(("parallel",)),
    )(x2, scale1, shift1, norm_w, w_in, cos_t, sin_t, qnw, knw, bd)


def _kv_update(state_ref, p, k, v, kdec, cdec, bdmask):
    kd = (k.astype(F32) * kdec).astype(BF16)
    kv = lax.dot_general(kd, v, (((0,), (0,)), ((), ())), preferred_element_type=F32)
    state_ref[p] = state_ref[p] * cdec + kv * bdmask


def _ret_state_kernel(k_ref, v_ref, kdec_ref, cdec_ref, bdmask_ref, sb_ref, state_ref):
    @pl.when(pl.program_id(1) == 0)
    def _():
        state_ref[...] = jnp.zeros_like(state_ref)

    sb_ref[...] = state_ref[...]
    for p in range(N_PAIRS):
        sl = slice(p * LANES, (p + 1) * LANES)
        _kv_update(state_ref, p, k_ref[:, sl], v_ref[:, sl], kdec_ref[:, sl], cdec_ref[:, sl], bdmask_ref[...])


def _ret_main_kernel(q_ref, k_ref, v_ref, g_ref, dmat_ref, qdf_ref, qdb_ref, kdf_ref, cdf_ref, bdmask_ref,
                     nw_ref, sb_ref, o_ref, state_ref):
    @pl.when(pl.program_id(1) == 0)
    def _():
        state_ref[...] = jnp.zeros_like(state_ref)

    c = q_ref.shape[0]
    lane = lax.broadcasted_iota(jnp.int32, (c, LANES), 1)
    lo = lane < HEAD_DIM
    for p in range(N_PAIRS):
        sl = slice(p * LANES, (p + 1) * LANES)
        q = q_ref[:, sl]
        k = k_ref[:, sl]
        v = v_ref[:, sl]
        qf = q.astype(F32)
        acc = jnp.dot((qf * qdf_ref[:, sl]).astype(BF16), state_ref[p].astype(BF16), preferred_element_type=F32)
        acc += jnp.dot((qf * qdb_ref[:, sl]).astype(BF16), sb_ref[p].astype(BF16), preferred_element_type=F32)
        for hh in range(2):
            sel = lo if hh == 0 else jnp.logical_not(lo)
            qm = jnp.where(sel, q, jnp.zeros_like(q))
            vm = jnp.where(sel, v, jnp.zeros_like(v))
            s = lax.dot_general(qm, k, (((1,), (1,)), ((), ())), preferred_element_type=F32)
            w = (s * dmat_ref[2 * p + hh]).astype(BF16)
            acc += jnp.dot(w, vm, preferred_element_type=F32)
        _kv_update(state_ref, p, k, v, kdf_ref[:, sl], cdf_ref[:, sl], bdmask_ref[...])
        sq = acc * acc
        ms_lo = jnp.sum(jnp.where(lo, sq, 0.0), axis=-1, keepdims=True)
        ms_hi = jnp.sum(jnp.where(lo, 0.0, sq), axis=-1, keepdims=True)
        ms = jnp.where(lo, ms_lo, ms_hi) * (1.0 / HEAD_DIM)
        y = acc * lax.rsqrt(ms + EPS) * nw_ref[:, sl]
        gf = g_ref[:, sl].astype(F32)
        o_ref[:, sl] = (gf * jax.nn.sigmoid(gf) * y).astype(BF16)


def _retention(rq, rk, rv, rg, tabs, ret_nw, batch, seq_len, c):
    t_total = rq.shape[0]
    nc = seq_len // c
    dmat, qdf, qdb, kdf, kdb, cdf, cdb, bdmask = tabs
    fixed2 = lambda b, i: (0, 0)
    rev = lambda b, i: (b * nc + nc - 1 - i, 0)
    fwd = lambda b, i: (b * nc + i, 0)
    tile = pl.BlockSpec((c, HALF_WIDTH), fwd)
    tile_rev = pl.BlockSpec((c, HALF_WIDTH), rev)
    tab = pl.BlockSpec((c, HALF_WIDTH), fixed2)
    vec = pl.BlockSpec((1, HALF_WIDTH), fixed2)
    mask = pl.BlockSpec((LANES, LANES), fixed2)
    state = pltpu.VMEM((N_PAIRS, LANES, LANES), F32)
    sb = pl.pallas_call(
        _ret_state_kernel,
        grid=(batch, nc),
        in_specs=[tile_rev, tile_rev, tab, vec, mask],
        out_specs=pl.BlockSpec((None, None, N_PAIRS, LANES, LANES), lambda b, i: (b, nc - 1 - i, 0, 0, 0)),
        out_shape=jax.ShapeDtypeStruct((batch, nc, N_PAIRS, LANES, LANES), F32),
        scratch_shapes=[state],
        compiler_params=_cparams(("parallel", "arbitrary")),
    )(rk, rv, kdb, cdb, bdmask)
    return pl.pallas_call(
        _ret_main_kernel,
        grid=(batch, nc),
        in_specs=[tile, tile, tile, tile,
                  pl.BlockSpec((RET_HEADS, c, c), lambda b, i: (0, 0, 0)),
                  tab, tab, tab, vec, mask, vec,
                  pl.BlockSpec((None, None, N_PAIRS, LANES, LANES), lambda b, i: (b, i, 0, 0, 0))],
        out_specs=tile,
        out_shape=jax.ShapeDtypeStruct((t_total, HALF_WIDTH), BF16),
        scratch_shapes=[state],
        compiler_params=_cparams(("parallel", "arbitrary")),
    )(rq, rk, rv, rg, dmat, qdf, qdb, kdf, cdf, bdmask, ret_nw, sb)


def _retention_tables(a_fwd, a_bwd, c):
    lg_f = jnp.log1p(-jnp.exp(a_fwd.astype(F32)))
    lg_b = jnp.log1p(-jnp.exp(a_bwd.astype(F32)))
    pos = jnp.arange(c, dtype=F32)
    diff = pos[:, None] - pos[None, :]
    dmat = jnp.where(diff[None] >= 0,
                     jnp.exp(jnp.maximum(diff, 0.0)[None] * lg_f[:, None, None]),
                     jnp.exp(jnp.maximum(-diff, 0.0)[None] * lg_b[:, None, None]))
    lane_f = jnp.repeat(lg_f, HEAD_DIM)[None, :]
    lane_b = jnp.repeat(lg_b, HEAD_DIM)[None, :]
    qdf = jnp.exp((pos + 1.0)[:, None] * lane_f)
    kdf = jnp.exp((c - 1.0 - pos)[:, None] * lane_f)
    qdb = jnp.exp((c - pos)[:, None] * lane_b)
    kdb = jnp.exp(pos[:, None] * lane_b)
    cdf = jnp.exp(c * lane_f)
    cdb = jnp.exp(c * lane_b)
    r = jnp.arange(LANES)
    bdmask = ((r[:, None] // HEAD_DIM) == (r[None, :] // HEAD_DIM)).astype(F32)
    return dmat, qdf, qdb, kdf, kdb, cdf, cdb, bdmask


def _attn_kernel(lam_ref, q_ref, k_ref, v_ref, bias_ref, sw_ref, o_ref, qq_ref, m_ref, acc_ref):
    i = pl.program_id(2)
    t = q_ref.shape[0]
    nk = k_ref.shape[0] // t
    q = q_ref[...]
    lane = lax.broadcasted_iota(jnp.int32, q.shape, 1)
    zero = jnp.zeros_like(q)
    qq_ref[...] = jnp.concatenate([jnp.where(lane < HEAD_DIM, q, zero), jnp.where(lane < HEAD_DIM, zero, q)], axis=0)
    m_ref[...] = jnp.full_like(m_ref, -jnp.inf)
    acc_ref[...] = jnp.zeros_like(acc_ref)

    def tile(j):
        keys = pl.ds(pl.multiple_of(j * t, t), t)
        which = jnp.clip(j - i, -2, 2) + 2
        for r in range(0, 2 * t, ATTN_ROWS):
            rows = slice(r, r + ATTN_ROWS)
            s = lax.dot_general(qq_ref[rows, :], k_ref[keys, :], (((1,), (1,)), ((), ())),
                                preferred_element_type=F32)
            s = s + bias_ref[which, r % t:r % t + ATTN_ROWS, :]
            m_prev = m_ref[rows, :]
            m_new = jnp.maximum(m_prev, jnp.max(s, axis=1, keepdims=True))
            alpha = jnp.exp2(m_prev - m_new)
            p = jnp.exp2(s - jnp.concatenate([m_new] * (t // LANES), axis=1)).astype(BF16)
            pv = jnp.dot(p, v_ref[keys, :], preferred_element_type=F32)
            acc_ref[rows, :] = jnp.concatenate([alpha, alpha], axis=1) * acc_ref[rows, :] + pv
            m_ref[rows, :] = m_new

    def step(jj, carry):
        for u in range(ATTN_UNROLL):
            tile(jj * ATTN_UNROLL + u)
        return carry

    lax.fori_loop(0, nk // ATTN_UNROLL, step, 0)

    acc = acc_ref[...]
    o = acc[:, :LANES] / acc[:, LANES:]
    att = o[:t] - lam_ref[0] * o[t:]
    msq = jnp.mean(att * att, axis=-1, keepdims=True)
    o_ref[...] = (att * lax.rsqrt(msq + EPS) * sw_ref[...]).astype(BF16)


def _attention(dq, dk, dv, bias_tiles, lam, subln_w, batch, seq_len, t):
    t_total = dq.shape[0]
    nq = seq_len // t
    assert nq % ATTN_UNROLL == 0 and t % ATTN_ROWS == 0
    grid_spec = pltpu.PrefetchScalarGridSpec(
        num_scalar_prefetch=1,
        grid=(batch, DIFF_HEADS, nq),
        in_specs=[pl.BlockSpec((t, LANES), lambda b, h, i, s: (b * nq + i, h)),
                  pl.BlockSpec((seq_len, LANES), lambda b, h, i, s: (b, h)),
                  pl.BlockSpec((seq_len, 2 * LANES), lambda b, h, i, s: (b, h)),
                  pl.BlockSpec((None, 5, t, t), lambda b, h, i, s: (h, 0, 0, 0)),
                  pl.BlockSpec((1, LANES), lambda b, h, i, s: (0, 0))],
        out_specs=pl.BlockSpec((t, LANES), lambda b, h, i, s: (b * nq + i, h)),
        scratch_shapes=[pltpu.VMEM((2 * t, LANES), BF16), pltpu.VMEM((2 * t, LANES), F32),
                        pltpu.VMEM((2 * t, 2 * LANES), F32)],
    )
    return pl.pallas_call(
        _attn_kernel,
        grid_spec=grid_spec,
        out_shape=jax.ShapeDtypeStruct((t_total, HALF_WIDTH), BF16),
        compiler_params=_cparams(("parallel", "parallel", "parallel")),
    )(lam, dq, dk, dv, bias_tiles, subln_w)


def _t5_bucket(rel):
    nb = NUM_BUCKETS // 2
    max_exact = nb // 2
    n = jnp.abs(rel)
    base = jnp.where(rel > 0, nb, 0)
    nf = jnp.maximum(n, 1).astype(F32)
    large = max_exact + (jnp.log(nf / max_exact) / math.log(MAX_DISTANCE / max_exact)
                         * (nb - max_exact)).astype(jnp.int32)
    large = jnp.minimum(large, nb - 1)
    return base + jnp.where(n < max_exact, n, large)


def _bias_tables(rel_bias, t):
    assert t >= MAX_DISTANCE
    table = rel_bias.astype(F32) * LOG2E
    qq = jnp.arange(t, dtype=jnp.int32)[:, None]
    kk = jnp.arange(t, dtype=jnp.int32)[None, :]
    bucket = jnp.stack([_t5_bucket(d * t + kk - qq) for d in (-2, -1, 0, 1, 2)], axis=0)
    onehot = (bucket[..., None] == jnp.arange(NUM_BUCKETS, dtype=jnp.int32)).astype(F32)
    return jnp.einsum('dqkn,nh->hdqk', onehot, table, precision=HIGHEST)


def _store_token_tiles(ref, value):
    n = value.shape[0]
    for s in range(SUBLANES):
        ref[pl.ds(s, n, stride=SUBLANES), :] = value[:, s * LANES:(s + 1) * LANES]


def _load_token_tiles(ref):
    n = ref.shape[0] // SUBLANES
    return jnp.concatenate([ref[pl.ds(s, n, stride=SUBLANES), :] for s in range(SUBLANES)], axis=1)


def _outproj_kernel(ret_ref, dif_ref, x_ref, g1_ref, sc2_ref, sh2_ref, nw2_ref, wo_ref, rw_ref, rb_ref, tri_ref,
                    x1_ref, h2_ref, idx_ref, gate_ref, count_ref, cnt_ref):
    mix = jnp.dot(ret_ref[...], wo_ref[:HALF_WIDTH, :], preferred_element_type=F32)
    mix += jnp.dot(dif_ref[...], wo_ref[HALF_WIDTH:, :], preferred_element_type=F32)
    x1 = x_ref[...] + g1_ref[...] * mix
    x1_ref[...] = x1
    ms = jnp.mean(x1 * x1, axis=-1, keepdims=True)
    h2 = x1 * lax.rsqrt(ms + EPS) * nw2_ref[...]
    h2 = h2 * (1.0 + sc2_ref[...]) + sh2_ref[...]
    _store_token_tiles(h2_ref, h2)
    logits = jnp.dot(h2, rw_ref[...], preferred_element_type=F32, precision=HIGHEST) + rb_ref[...]
    lane = lax.broadcasted_iota(jnp.int32, logits.shape, 1)
    vals, idxs = [], []
    for _ in range(TOP_K):
        m = jnp.max(logits, axis=-1, keepdims=True)
        am = jnp.min(jnp.where(logits == m, lane, LANES), axis=-1, keepdims=True)
        vals.append(m)
        idxs.append(am)
        logits = jnp.where(lane == am, -jnp.inf, logits)
    es = [jnp.exp(v - vals[0]) for v in vals]
    den = es[0] + es[1] + es[2] + es[3]
    @pl.when(pl.program_id(0) == 0)
    def _():
        cnt_ref[...] = jnp.zeros_like(cnt_ref)

    member = jnp.zeros(lane.shape, F32)
    for kk in range(TOP_K):
        member = jnp.where(lane == idxs[kk], 1.0, member)
    before = jnp.dot(tri_ref[...], member.astype(BF16), preferred_element_type=F32) + cnt_ref[...]
    cnt_ref[...] += jnp.sum(member, axis=0, keepdims=True)
    count_ref[...] = jnp.broadcast_to(cnt_ref[...], count_ref.shape).astype(jnp.int32)
    idx_out = jnp.zeros(lane.shape, jnp.int32)
    gate_out = jnp.zeros(lane.shape, F32)
    for kk in range(TOP_K):
        rank = jnp.sum(jnp.where(lane == idxs[kk], before, 0.0), axis=-1, keepdims=True).astype(jnp.int32)
        idx_out = jnp.where(lane == kk, idxs[kk], idx_out)
        idx_out = jnp.where(lane == TOP_K + kk, rank, idx_out)
        gate_out = jnp.where(lane == kk, es[kk] / den, gate_out)
    idx_ref[...] = idx_out
    gate_ref[...] = gate_out


def _outproj(ret_out, diff_out, x2, gate1, scale2, shift2, norm2_w, w_out, rw_pad, rb_pad, seq_len, tt):
    t_total, d = x2.shape
    nps = seq_len // tt
    row = lambda i: (i, 0)
    mod = lambda i: (i // nps, 0, 0)
    fixed = lambda i: (0, 0)
    pos = jnp.arange(tt)
    tri = (pos[None, :] < pos[:, None]).astype(BF16)
    return pl.pallas_call(
        _outproj_kernel,
        grid=(t_total // tt,),
        in_specs=[pl.BlockSpec((tt, HALF_WIDTH), row),
                  pl.BlockSpec((tt, HALF_WIDTH), row),
                  pl.BlockSpec((tt, d), row),
                  pl.BlockSpec((None, 1, d), mod),
                  pl.BlockSpec((None, 1, d), mod),
                  pl.BlockSpec((None, 1, d), mod),
                  pl.BlockSpec((1, d), fixed),
                  pl.BlockSpec((2 * HALF_WIDTH, d), fixed),
                  pl.BlockSpec((d, LANES), fixed),
                  pl.BlockSpec((1, LANES), fixed),
                  pl.BlockSpec((tt, tt), fixed)],
        out_specs=[pl.BlockSpec((tt, d), row), pl.BlockSpec((tt * SUBLANES, LANES), row),
                   pl.BlockSpec((tt, LANES), row), pl.BlockSpec((tt, LANES), row),
                   pl.BlockSpec((SUBLANES, LANES), fixed)],
        out_shape=[jax.ShapeDtypeStruct((t_total, d), F32), jax.ShapeDtypeStruct((t_total * SUBLANES, LANES), F32),
                   jax.ShapeDtypeStruct((t_total, LANES), jnp.int32),
                   jax.ShapeDtypeStruct((t_total, LANES), F32),
                   jax.ShapeDtypeStruct((SUBLANES, LANES), jnp.int32)],
        scratch_shapes=[pltpu.VMEM((1, LANES), F32)],
        compiler_params=_cparams(("arbitrary",)),
    )(ret_out, diff_out, x2, gate1, scale2, shift2, norm2_w, w_out, rw_pad, rb_pad, tri)


def _expert_kernel(be_ref, act_ref, src_ref, src_next_ref, dst_prev_ref, h2_hbm, wgu_ref, bgu_ref, wd_ref, bd_ref,
                   out_hbm, x0, x1, y0, y1, sem_in, sem_out):
    i = pl.program_id(0)
    rows = x0.shape[0] // SUBLANES
    d_ff = wd_ref.shape[1]
    prev_active = jnp.logical_and(i > 0, act_ref[jnp.maximum(i - 1, 0)] == 1)

    def gather_copy(src, r, xbuf, sem):
        return pltpu.make_async_copy(h2_hbm.at[pl.ds(pl.multiple_of(src[r], SUBLANES), SUBLANES)],
                                     xbuf.at[pl.ds(r * SUBLANES, SUBLANES)], sem)

    @pl.when(i == 0)
    def _():
        y1[...] = jnp.zeros_like(y1)
        for r in range(rows):
            gather_copy(src_ref, r, x0, sem_in.at[0]).start()

    def body(xcur, xnext, ycur, yprev, sem_cur, sem_next):
        pltpu.make_async_copy(h2_hbm.at[pl.ds(0, rows * SUBLANES)], xcur, sem_cur).wait()

        @pl.when(prev_active)
        def _():
            pltpu.make_async_copy(ycur, out_hbm.at[pl.ds(0, rows * SUBLANES)], sem_out).wait()

        @pl.when(act_ref[i] == 1)
        def _():
            for r in range(rows):
                gather_copy(src_next_ref, r, xnext, sem_next).start()
            for r in range(rows):
                pltpu.async_copy(yprev.at[pl.ds(r * SUBLANES, SUBLANES)],
                                 out_hbm.at[pl.ds(pl.multiple_of(dst_prev_ref[r], SUBLANES), SUBLANES)],
                                 sem_out, priority=1)
            x = _load_token_tiles(xcur).astype(BF16)
            gu = jnp.dot(x, wgu_ref[0], preferred_element_type=F32) + bgu_ref[0]
            glu = jnp.minimum(gu[:, :d_ff], SWIGLU_LIMIT)
            lin = jnp.clip(gu[:, d_ff:], -SWIGLU_LIMIT, SWIGLU_LIMIT)
            hidden = glu * jax.nn.sigmoid(SWIGLU_ALPHA * glu) * (lin + 1.0)
            _store_token_tiles(ycur, jnp.dot(hidden.astype(BF16), wd_ref[0], preferred_element_type=F32) + bd_ref[0])

    rows_requested = jnp.logical_or(i == 0, prev_active)

    @pl.when(jnp.logical_and(rows_requested, i % 2 == 0))
    def _():
        body(x0, x1, y0, y1, sem_in.at[0], sem_in.at[1])

    @pl.when(jnp.logical_and(rows_requested, i % 2 == 1))
    def _():
        body(x1, x0, y1, y0, sem_in.at[1], sem_in.at[0])


def _experts(h2, block_expert, active, row_src, dst_prev, wgu, bgu, wd, bd, rows):
    t_total = h2.shape[0] // SUBLANES
    n_blocks = block_expert.shape[0]
    d, d_ff2 = wgu.shape[1:]
    assert d == SUBLANES * LANES
    last = n_blocks - 1
    buf = pltpu.VMEM((rows * SUBLANES, LANES), F32)
    grid_spec = pltpu.PrefetchScalarGridSpec(
        num_scalar_prefetch=2,
        grid=(n_blocks,),
        in_specs=[pl.BlockSpec((rows,), lambda i, be, act: (i,), memory_space=pltpu.SMEM),
                  pl.BlockSpec((rows,), lambda i, be, act: (jnp.minimum(i + 1, last),), memory_space=pltpu.SMEM),
                  pl.BlockSpec((rows,), lambda i, be, act: (i,), memory_space=pltpu.SMEM),
                  pl.BlockSpec(memory_space=pl.ANY),
                  pl.BlockSpec((1, d, d_ff2), lambda i, be, act: (be[i], 0, 0)),
                  pl.BlockSpec((1, 1, d_ff2), lambda i, be, act: (be[i], 0, 0)),
                  pl.BlockSpec((1, d_ff2 // 2, d), lambda i, be, act: (be[i], 0, 0)),
                  pl.BlockSpec((1, 1, d), lambda i, be, act: (be[i], 0, 0))],
        out_specs=pl.BlockSpec(memory_space=pl.ANY),
        scratch_shapes=[buf, buf, buf, buf, pltpu.SemaphoreType.DMA((2,)), pltpu.SemaphoreType.DMA],
    )
    return pl.pallas_call(
        _expert_kernel,
        grid_spec=grid_spec,
        out_shape=jax.ShapeDtypeStruct(((TOP_K * t_total + rows) * SUBLANES, LANES), F32),
        compiler_params=_cparams(("arbitrary",)),
    )(block_expert, active, row_src, row_src, dst_prev, h2, wgu, bgu, wd, bd)


def _routing(top_idx, rank, counts, rows):
    t_total = top_idx.shape[0]
    padded = (counts + rows - 1) // rows * rows
    ends = jnp.cumsum(padded)
    starts = ends - padded
    dest = (starts[top_idx] + rank).reshape(-1)
    n_blocks = (t_total * TOP_K + N_EXPERTS * (rows - 1)) // rows + 2
    n_rows = n_blocks * rows
    n_slots = TOP_K * t_total
    tok = jnp.repeat(jnp.arange(t_total, dtype=jnp.int32), TOP_K)
    slot = jnp.tile(jnp.arange(TOP_K, dtype=jnp.int32), t_total)
    landing = n_slots + jnp.arange(n_rows, dtype=jnp.int32) % rows
    row_dst = landing.at[dest].set(slot * t_total + tok, unique_indices=True)
    row_src = jnp.where(row_dst < n_slots, row_dst % t_total, 0)
    dst_prev = jnp.concatenate([landing[:rows], row_dst[:-rows]])
    block_start = jnp.arange(n_blocks, dtype=jnp.int32) * rows
    block_expert = jnp.minimum(jnp.searchsorted(ends, block_start, side='right'), N_EXPERTS - 1).astype(jnp.int32)
    has_rows = block_start < ends[-1]
    active = jnp.logical_or(has_rows, jnp.concatenate([has_rows[:1], has_rows[:-1]])).astype(jnp.int32)
    return block_expert, active, row_src * SUBLANES, dst_prev * SUBLANES


def _combine_kernel(x1_ref, g2_ref, gate_ref, *refs):
    y_refs, o_ref = refs[:TOP_K], refs[TOP_K]
    gates = gate_ref[...]
    moe = gates[:, 0:1] * _load_token_tiles(y_refs[0])
    for kk in range(1, TOP_K):
        moe += gates[:, kk:kk + 1] * _load_token_tiles(y_refs[kk])
    o_ref[...] = x1_ref[...] + g2_ref[...] * moe


def _combine(x1, gate2, gates, y4, seq_len, tt):
    t_total, d = x1.shape
    nps = seq_len // tt
    n_tiles = t_total // tt
    slot_specs = [pl.BlockSpec((tt * SUBLANES, LANES), functools.partial(lambda kk, i: (kk * n_tiles + i, 0), kk))
                  for kk in range(TOP_K)]
    return pl.pallas_call(
        _combine_kernel,
        grid=(n_tiles,),
        in_specs=[pl.BlockSpec((tt, d), lambda i: (i, 0)),
                  pl.BlockSpec((None, 1, d), lambda i: (i // nps, 0, 0)),
                  pl.BlockSpec((tt, LANES), lambda i: (i, 0))] + slot_specs,
        out_specs=pl.BlockSpec((tt, d), lambda i: (i, 0)),
        out_shape=jax.ShapeDtypeStruct((t_total, d), F32),
        compiler_params=_cparams(("parallel",)),
    )(x1, gate2, gates, *([y4] * TOP_K))


def _tiles(seq_len):
    token_tile = min(512, seq_len)
    ret_chunk = min(256, seq_len)
    attn_tile = min(512, seq_len)
    expert_rows = 256
    combine_tile = min(256, seq_len)
    return token_tile, ret_chunk, attn_tile, expert_rows, combine_tile


def _trunk(x, mod, lam_init, norm1_w, w_in_b, ret_a_fwd, ret_a_bwd, ret_norm_w, diff_q_norm_w, diff_k_norm_w,
           lam_q1, lam_k1, lam_q2, lam_k2, diff_subln_w, rel_bias, w_out_b, norm2_w, rw_pad, rb_pad,
           wgu_b, b_gate_up, wd_b, b_down):
    batch, seq_len, d = x.shape
    tt, c, t, rows, ct = _tiles(seq_len)
    x2 = x.reshape(batch * seq_len, d)
    shift1, scale1, gate1, shift2, scale2, gate2 = [mod[:, n][:, None, :] for n in range(6)]

    pos = jnp.arange(seq_len, dtype=F32)
    inv_freq = ROPE_BASE ** (-jnp.arange(0, HEAD_DIM, 2, dtype=F32) / HEAD_DIM)
    ang = pos[:, None] * inv_freq[None, :]
    cos_t = jnp.tile(jnp.cos(ang), (1, LANES // (HEAD_DIM // 2)))
    sin_h = jnp.sin(ang)
    sin_t = jnp.tile(jnp.concatenate([-sin_h, sin_h], axis=1), (1, LANES // HEAD_DIM))
    qnw = jnp.tile(diff_q_norm_w.astype(F32), HALF_WIDTH // HEAD_DIM)[None, :]
    knw = jnp.tile(diff_k_norm_w.astype(F32), HALF_WIDTH // HEAD_DIM)[None, :]
    r = jnp.arange(HALF_WIDTH)
    bd = jnp.where((r[:, None] // HEAD_DIM) == (r[None, :] // HEAD_DIM), 1.0 / HEAD_DIM, 0.0).astype(BF16)

    rq, rk, rv, rg, dq, dk, dv = _inproj(x2, scale1, shift1, norm1_w[None, :], w_in_b, cos_t, sin_t, qnw, knw, bd,
                                         seq_len, tt)

    tabs = _retention_tables(ret_a_fwd, ret_a_bwd, c)
    ret_out = _retention(rq, rk, rv, rg, tabs, ret_norm_w.reshape(1, HALF_WIDTH).astype(F32), batch, seq_len, c)

    lam = (jnp.exp(jnp.sum(lam_q1.astype(F32) * lam_k1.astype(F32)))
           - jnp.exp(jnp.sum(lam_q2.astype(F32) * lam_k2.astype(F32))) + lam_init)
    subln = (diff_subln_w.astype(F32) * (1.0 - lam_init))[None, :]
    diff_out = _attention(dq, dk, dv, _bias_tables(rel_bias, t), lam[None].astype(F32), subln, batch, seq_len, t)

    x1, h2, idx_pad, gates_pad, counts = _outproj(ret_out, diff_out, x2, gate1, scale2, shift2, norm2_w[None, :],
                                                  w_out_b, rw_pad, rb_pad, seq_len, tt)
    block_expert, active, row_src, dst_prev = _routing(idx_pad[:, :TOP_K], idx_pad[:, TOP_K:2 * TOP_K],
                                                       counts[0, :N_EXPERTS], rows)
    y4 = _experts(h2, block_expert, active, row_src, dst_prev, wgu_b, b_gate_up, wd_b, b_down, rows)
    y = _combine(x1, gate2, gates_pad, y4, seq_len, ct)
    return y.reshape(batch, seq_len, d)


def kernel(x_prompt, x_sample, c_prompt, c_sample, w_ada, b_ada, norm1_w, w_in, ret_a_fwd, ret_a_bwd, ret_norm_w,
           diff_q_norm_w, diff_k_norm_w, lam_q1, lam_k1, lam_q2, lam_k2, diff_subln_w, rel_bias, w_out, norm2_w,
           router_w, router_b, w_gate_up, b_gate_up, w_down, b_down):
    depth = w_ada.shape[0]
    d = x_prompt.shape[-1]
    n_prompt = c_prompt.shape[0]
    n_cond = n_prompt + c_sample.shape[0]
    c_all = jnp.concatenate([c_prompt, c_sample], axis=0).astype(F32)
    c_pad = jnp.pad(c_all, ((0, -n_cond % 8), (0, 0)))
    xs = [x_prompt, x_sample]
    for l in range(depth):
        lam_init = 0.8 - 0.6 * math.exp(-0.3 * l)
        mod = _ada(c_pad, w_ada[l], b_ada[l])[:n_cond].reshape(n_cond, 6, d)
        rw_pad = jnp.pad(router_w[l].astype(F32), ((0, 0), (0, LANES - N_EXPERTS)))
        rb_pad = jnp.pad(router_b[l].astype(F32), (0, LANES - N_EXPERTS), constant_values=-jnp.inf)[None, :]
        shared = (norm1_w[l], w_in[l].astype(BF16), ret_a_fwd[l], ret_a_bwd[l], ret_norm_w[l], diff_q_norm_w[l],
                  diff_k_norm_w[l], lam_q1[l], lam_k1[l], lam_q2[l], lam_k2[l], diff_subln_w[l], rel_bias,
                  w_out[l].astype(BF16), norm2_w[l], rw_pad, rb_pad,
                  w_gate_up[l].astype(BF16), b_gate_up[l][:, None, :], w_down[l].astype(BF16), b_down[l][:, None, :])
        xs = [_trunk(xs[0], mod[:n_prompt], lam_init, *shared),
              _trunk(xs[1], mod[n_prompt:], lam_init, *shared)]
    return (xs[0], xs[1])
```

```python
import functools
import math

import jax
import jax.numpy as jnp
import numpy as np
from jax import lax
from jax.experimental import pallas as pl
from jax.experimental.pallas import tpu as pltpu

F32 = jnp.float32
BF16 = jnp.bfloat16
HIGHEST = lax.Precision.HIGHEST

LANES = 128
SUBLANES = 8
HEAD_DIM = 64
RET_HEADS = 8
DIFF_HEADS = 4
HALF_WIDTH = 512
N_PAIRS = HALF_WIDTH // LANES
ROPE_BASE = 10000.0
NUM_BUCKETS = 32
MAX_DISTANCE = 128
N_EXPERTS = 32
TOP_K = 4
SWIGLU_LIMIT = 7.0
SWIGLU_ALPHA = 1.702
EPS = 1e-6
LOG2E = math.log2(math.e)
ATTN_ROWS = 128
ATTN_UNROLL = 8
VMEM_LIMIT = 56 * 1024 * 1024


def _cparams(sem):
    return pltpu.CompilerParams(dimension_semantics=sem, vmem_limit_bytes=VMEM_LIMIT)


def _ada_kernel(c_ref, w_ref, b_ref, o_ref):
    c = c_ref[...]
    a = c * jax.nn.sigmoid(c)
    o_ref[...] = jnp.dot(a, w_ref[...], preferred_element_type=F32, precision=HIGHEST) + b_ref[...]


def _ada(c_pad, w_ada, b_ada):
    rows, d = c_pad.shape
    n = w_ada.shape[1]
    tn = d
    return pl.pallas_call(
        _ada_kernel,
        grid=(n // tn,),
        in_specs=[pl.BlockSpec((rows, d), lambda j: (0, 0)),
                  pl.BlockSpec((d, tn), lambda j: (0, j)),
                  pl.BlockSpec((1, tn), lambda j: (0, j))],
        out_specs=pl.BlockSpec((rows, tn), lambda j: (0, j)),
        out_shape=jax.ShapeDtypeStruct((rows, n), F32),
        compiler_params=_cparams(("parallel",)),
    )(c_pad, w_ada, b_ada.reshape(1, n))


def _rotate_half(xg):
    lane = lax.broadcasted_iota(jnp.int32, xg.shape, 1)
    first = (lane % HEAD_DIM) < (HEAD_DIM // 2)
    return jnp.where(first, pltpu.roll(xg, LANES - HEAD_DIM // 2, 1), pltpu.roll(xg, HEAD_DIM // 2, 1))


def _inproj_kernel(x_ref, sc_ref, sh_ref, nw_ref, w_ref, cos_ref, sin_ref, qnw_ref, knw_ref, bd_ref,
                   rq_ref, rk_ref, rv_ref, rg_ref, dq_ref, dk_ref, dv_ref):
    x = x_ref[...]
    ms = jnp.mean(x * x, axis=-1, keepdims=True)
    h = x * lax.rsqrt(ms + EPS) * nw_ref[...]
    h = (h * (1.0 + sc_ref[...]) + sh_ref[...]).astype(BF16)

    def piece(n):
        return jnp.dot(h, w_ref[:, n * HALF_WIDTH:(n + 1) * HALF_WIDTH], preferred_element_type=F32)

    cos = cos_ref[...]
    sin = sin_ref[...]

    def rotary(p, out_ref, scale):
        for g in range(N_PAIRS):
            xg = p[:, g * LANES:(g + 1) * LANES]
            y = xg * cos + _rotate_half(xg) * sin
            out_ref[:, g * LANES:(g + 1) * LANES] = (y * scale).astype(BF16)

    def head_norm(p, w, out_ref, scale):
        msq = jnp.dot((p * p).astype(BF16), bd_ref[...], preferred_element_type=F32)
        out_ref[...] = (p * lax.rsqrt(msq + EPS) * w * scale).astype(BF16)

    rotary(piece(0), rq_ref, 1.0)
    rotary(piece(1), rk_ref, HEAD_DIM ** -0.5)
    rv_ref[...] = piece(2).astype(BF16)
    rg_ref[...] = piece(3).astype(BF16)
    head_norm(piece(4), qnw_ref[...], dq_ref, HEAD_DIM ** -0.5 * LOG2E)
    head_norm(piece(5), knw_ref[...], dk_ref, 1.0)
    dv = piece(6).astype(BF16)
    ones = jnp.ones((dv.shape[0], LANES), BF16)
    for hd in range(DIFF_HEADS):
        dv_ref[:, 2 * hd * LANES:(2 * hd + 1) * LANES] = dv[:, hd * LANES:(hd + 1) * LANES]
        dv_ref[:, (2 * hd + 1) * LANES:(2 * hd + 2) * LANES] = ones


def _inproj(x2, scale1, shift1, norm_w, w_in, cos_t, sin_t, qnw, knw, bd, seq_len, tt):
    t_total, d = x2.shape
    nps = seq_len // tt
    n_in = w_in.shape[1]
    row = lambda i: (i, 0)
    mod = lambda i: (i // nps, 0, 0)
    pos = lambda i: (i % nps, 0)
    fixed = lambda i: (0, 0)
    out_sd = jax.ShapeDtypeStruct((t_total, HALF_WIDTH), BF16)
    return pl.pallas_call(
        _inproj_kernel,
        grid=(t_total // tt,),
        in_specs=[pl.BlockSpec((tt, d), row),
                  pl.BlockSpec((None, 1, d), mod),
                  pl.BlockSpec((None, 1, d), mod),
                  pl.BlockSpec((1, d), fixed),
                  pl.BlockSpec((d, n_in), fixed),
                  pl.BlockSpec((tt, LANES), pos),
                  pl.BlockSpec((tt, LANES), pos),
                  pl.BlockSpec((1, HALF_WIDTH), fixed),
                  pl.BlockSpec((1, HALF_WIDTH), fixed),
                  pl.BlockSpec((HALF_WIDTH, HALF_WIDTH), fixed)],
        out_specs=[pl.BlockSpec((tt, HALF_WIDTH), row)] * 6 + [pl.BlockSpec((tt, 2 * HALF_WIDTH), row)],
        out_shape=[out_sd] * 6 + [jax.ShapeDtypeStruct((t_total, 2 * HALF_WIDTH), BF16)],
        compiler_params=_cparams(("parallel",)),
    )(x2, scale1, shift1, norm_w, w_in, cos_t, sin_t, qnw, knw, bd)


def _kv_update(state_ref, p, k, v, kdec, cdec, bdmask):
    kd = (k.astype(F32) * kdec).astype(BF16)
    kv = lax.dot_general(kd, v, (((0,), (0,)), ((), ())), preferred_element_type=F32)
    state_ref[p] = state_ref[p] * cdec + kv * bdmask


def _ret_state_kernel(k_ref, v_ref, kdec_ref, cdec_ref, bdmask_ref, sb_ref, state_ref):
    @pl.when(pl.program_id(1) == 0)
    def _():
        state_ref[...] = jnp.zeros_like(state_ref)

    sb_ref[...] = state_ref[...]
    for p in range(N_PAIRS):
        sl = slice(p * LANES, (p + 1) * LANES)
        _kv_update(state_ref, p, k_ref[:, sl], v_ref[:, sl], kdec_ref[:, sl], cdec_ref[:, sl], bdmask_ref[...])


def _ret_main_kernel(q_ref, k_ref, v_ref, g_ref, dmat_ref, qdf_ref, qdb_ref, kdf_ref, cdf_ref, bdmask_ref,
                     nw_ref, sb_ref, o_ref, state_ref):
    @pl.when(pl.program_id(1) == 0)
    def _():
        state_ref[...] = jnp.zeros_like(state_ref)

    c = q_ref.shape[0]
    lane = lax.broadcasted_iota(jnp.int32, (c, LANES), 1)
    lo = lane < HEAD_DIM
    for p in range(N_PAIRS):
        sl = slice(p * LANES, (p + 1) * LANES)
        q = q_ref[:, sl]
        k = k_ref[:, sl]
        v = v_ref[:, sl]
        qf = q.astype(F32)
        acc = jnp.dot((qf * qdf_ref[:, sl]).astype(BF16), state_ref[p].astype(BF16), preferred_element_type=F32)
        acc += jnp.dot((qf * qdb_ref[:, sl]).astype(BF16), sb_ref[p].astype(BF16), preferred_element_type=F32)
        for hh in range(2):
            sel = lo if hh == 0 else jnp.logical_not(lo)
            qm = jnp.where(sel, q, jnp.zeros_like(q))
            vm = jnp.where(sel, v, jnp.zeros_like(v))
            s = lax.dot_general(qm, k, (((1,), (1,)), ((), ())), preferred_element_type=F32)
            w = (s * dmat_ref[2 * p + hh]).astype(BF16)
            acc += jnp.dot(w, vm, preferred_element_type=F32)
        _kv_update(state_ref, p, k, v, kdf_ref[:, sl], cdf_ref[:, sl], bdmask_ref[...])
        sq = acc * acc
        ms_lo = jnp.sum(jnp.where(lo, sq, 0.0), axis=-1, keepdims=True)
        ms_hi = jnp.sum(jnp.where(lo, 0.0, sq), axis=-1, keepdims=True)
        ms = jnp.where(lo, ms_lo, ms_hi) * (1.0 / HEAD_DIM)
        y = acc * lax.rsqrt(ms + EPS) * nw_ref[:, sl]
        gf = g_ref[:, sl].astype(F32)
        o_ref[:, sl] = (gf * jax.nn.sigmoid(gf) * y).astype(BF16)


def _retention(rq, rk, rv, rg, tabs, ret_nw, batch, seq_len, c):
    t_total = rq.shape[0]
    nc = seq_len // c
    dmat, qdf, qdb, kdf, kdb, cdf, cdb, bdmask = tabs
    fixed2 = lambda b, i: (0, 0)
    rev = lambda b, i: (b * nc + nc - 1 - i, 0)
    fwd = lambda b, i: (b * nc + i, 0)
    tile = pl.BlockSpec((c, HALF_WIDTH), fwd)
    tile_rev = pl.BlockSpec((c, HALF_WIDTH), rev)
    tab = pl.BlockSpec((c, HALF_WIDTH), fixed2)
    vec = pl.BlockSpec((1, HALF_WIDTH), fixed2)
    mask = pl.BlockSpec((LANES, LANES), fixed2)
    state = pltpu.VMEM((N_PAIRS, LANES, LANES), F32)
    sb = pl.pallas_call(
        _ret_state_kernel,
        grid=(batch, nc),
        in_specs=[tile_rev, tile_rev, tab, vec, mask],
        out_specs=pl.BlockSpec((None, None, N_PAIRS, LANES, LANES), lambda b, i: (b, nc - 1 - i, 0, 0, 0)),
        out_shape=jax.ShapeDtypeStruct((batch, nc, N_PAIRS, LANES, LANES), F32),
        scratch_shapes=[state],
        compiler_params=_cparams(("parallel", "arbitrary")),
    )(rk, rv, kdb, cdb, bdmask)
    return pl.pallas_call(
        _ret_main_kernel,
        grid=(batch, nc),
        in_specs=[tile, tile, tile, tile,
                  pl.BlockSpec((RET_HEADS, c, c), lambda b, i: (0, 0, 0)),
                  tab, tab, tab, vec, mask, vec,
                  pl.BlockSpec((None, None, N_PAIRS, LANES, LANES), lambda b, i: (b, i, 0, 0, 0))],
        out_specs=tile,
        out_shape=jax.ShapeDtypeStruct((t_total, HALF_WIDTH), BF16),
        scratch_shapes=[state],
        compiler_params=_cparams(("parallel", "arbitrary")),
    )(rq, rk, rv, rg, dmat, qdf, qdb, kdf, cdf, bdmask, ret_nw, sb)


def _retention_tables(a_fwd, a_bwd, c):
    lg_f = jnp.log1p(-jnp.exp(a_fwd.astype(F32)))
    lg_b = jnp.log1p(-jnp.exp(a_bwd.astype(F32)))
    pos = jnp.arange(c, dtype=F32)
    diff = pos[:, None] - pos[None, :]
    dmat = jnp.where(diff[None] >= 0,
                     jnp.exp(jnp.maximum(diff, 0.0)[None] * lg_f[:, None, None]),
                     jnp.exp(jnp.maximum(-diff, 0.0)[None] * lg_b[:, None, None]))
    lane_f = jnp.repeat(lg_f, HEAD_DIM)[None, :]
    lane_b = jnp.repeat(lg_b, HEAD_DIM)[None, :]
    qdf = jnp.exp((pos + 1.0)[:, None] * lane_f)
    kdf = jnp.exp((c - 1.0 - pos)[:, None] * lane_f)
    qdb = jnp.exp((c - pos)[:, None] * lane_b)
    kdb = jnp.exp(pos[:, None] * lane_b)
    cdf = jnp.exp(c * lane_f)
    cdb = jnp.exp(c * lane_b)
    r = jnp.arange(LANES)
    bdmask = ((r[:, None] // HEAD_DIM) == (r[None, :] // HEAD_DIM)).astype(F32)
    return dmat, qdf, qdb, kdf, kdb, cdf, cdb, bdmask


def _attn_kernel(lam_ref, q_ref, k_ref, v_ref, bias_ref, sw_ref, o_ref, qq_ref, m_ref, acc_ref):
    i = pl.program_id(2)
    t = q_ref.shape[0]
    nk = k_ref.shape[0] // t
    q = q_ref[...]
    lane = lax.broadcasted_iota(jnp.int32, q.shape, 1)
    zero = jnp.zeros_like(q)
    qq_ref[...] = jnp.concatenate([jnp.where(lane < HEAD_DIM, q, zero), jnp.where(lane < HEAD_DIM, zero, q)], axis=0)
    m_ref[...] = jnp.full_like(m_ref, -jnp.inf)
    acc_ref[...] = jnp.zeros_like(acc_ref)

    def tile(j):
        keys = pl.ds(pl.multiple_of(j * t, t), t)
        which = jnp.clip(j - i, -2, 2) + 2
        for r in range(0, 2 * t, ATTN_ROWS):
            rows = slice(r, r + ATTN_ROWS)
            s = lax.dot_general(qq_ref[rows, :], k_ref[keys, :], (((1,), (1,)), ((), ())),
                                preferred_element_type=F32)
            s = s + bias_ref[which, r % t:r % t + ATTN_ROWS, :]
            m_prev = m_ref[rows, :]
            m_new = jnp.maximum(m_prev, jnp.max(s, axis=1, keepdims=True))
            alpha = jnp.exp2(m_prev - m_new)
            p = jnp.exp2(s - jnp.concatenate([m_new] * (t // LANES), axis=1)).astype(BF16)
            pv = jnp.dot(p, v_ref[keys, :], preferred_element_type=F32)
            acc_ref[rows, :] = jnp.concatenate([alpha, alpha], axis=1) * acc_ref[rows, :] + pv
            m_ref[rows, :] = m_new

    def step(jj, carry):
        for u in range(ATTN_UNROLL):
            tile(jj * ATTN_UNROLL + u)
        return carry

    lax.fori_loop(0, nk // ATTN_UNROLL, step, 0)

    acc = acc_ref[...]
    o = acc[:, :LANES] / acc[:, LANES:]
    att = o[:t] - lam_ref[0] * o[t:]
    msq = jnp.mean(att * att, axis=-1, keepdims=True)
    o_ref[...] = (att * lax.rsqrt(msq + EPS) * sw_ref[...]).astype(BF16)


def _attention(dq, dk, dv, bias_tiles, lam, subln_w, batch, seq_len, t):
    t_total = dq.shape[0]
    nq = seq_len // t
    assert nq % ATTN_UNROLL == 0 and t % ATTN_ROWS == 0
    grid_spec = pltpu.PrefetchScalarGridSpec(
        num_scalar_prefetch=1,
        grid=(batch, DIFF_HEADS, nq),
        in_specs=[pl.BlockSpec((t, LANES), lambda b, h, i, s: (b * nq + i, h)),
                  pl.BlockSpec((seq_len, LANES), lambda b, h, i, s: (b, h)),
                  pl.BlockSpec((seq_len, 2 * LANES), lambda b, h, i, s: (b, h)),
                  pl.BlockSpec((None, 5, t, t), lambda b, h, i, s: (h, 0, 0, 0)),
                  pl.BlockSpec((1, LANES), lambda b, h, i, s: (0, 0))],
        out_specs=pl.BlockSpec((t, LANES), lambda b, h, i, s: (b * nq + i, h)),
        scratch_shapes=[pltpu.VMEM((2 * t, LANES), BF16), pltpu.VMEM((2 * t, LANES), F32),
                        pltpu.VMEM((2 * t, 2 * LANES), F32)],
    )
    return pl.pallas_call(
        _attn_kernel,
        grid_spec=grid_spec,
        out_shape=jax.ShapeDtypeStruct((t_total, HALF_WIDTH), BF16),
        compiler_params=_cparams(("parallel", "parallel", "parallel")),
    )(lam, dq, dk, dv, bias_tiles, subln_w)


def _t5_bucket(rel):
    nb = NUM_BUCKETS // 2
    max_exact = nb // 2
    n = jnp.abs(rel)
    base = jnp.where(rel > 0, nb, 0)
    nf = jnp.maximum(n, 1).astype(F32)
    large = max_exact + (jnp.log(nf / max_exact) / math.log(MAX_DISTANCE / max_exact)
                         * (nb - max_exact)).astype(jnp.int32)
    large = jnp.minimum(large, nb - 1)
    return base + jnp.where(n < max_exact, n, large)


def _bias_tables(rel_bias, t):
    assert t >= MAX_DISTANCE
    table = rel_bias.astype(F32) * LOG2E
    qq = jnp.arange(t, dtype=jnp.int32)[:, None]
    kk = jnp.arange(t, dtype=jnp.int32)[None, :]
    bucket = jnp.stack([_t5_bucket(d * t + kk - qq) for d in (-2, -1, 0, 1, 2)], axis=0)
    onehot = (bucket[..., None] == jnp.arange(NUM_BUCKETS, dtype=jnp.int32)).astype(F32)
    return jnp.einsum('dqkn,nh->hdqk', onehot, table, precision=HIGHEST)


def _store_token_tiles(ref, value):
    n = value.shape[0]
    for s in range(SUBLANES):
        ref[pl.ds(s, n, stride=SUBLANES), :] = value[:, s * LANES:(s + 1) * LANES]


def _load_token_tiles(ref):
    n = ref.shape[0] // SUBLANES
    return jnp.concatenate([ref[pl.ds(s, n, stride=SUBLANES), :] for s in range(SUBLANES)], axis=1)


def _outproj_kernel(ret_ref, dif_ref, x_ref, g1_ref, sc2_ref, sh2_ref, nw2_ref, wo_ref, rw_ref, rb_ref, tri_ref,
                    x1_ref, h2_ref, idx_ref, gate_ref, count_ref, cnt_ref):
    mix = jnp.dot(ret_ref[...], wo_ref[:HALF_WIDTH, :], preferred_element_type=F32)
    mix += jnp.dot(dif_ref[...], wo_ref[HALF_WIDTH:, :], preferred_element_type=F32)
    x1 = x_ref[...] + g1_ref[...] * mix
    x1_ref[...] = x1
    ms = jnp.mean(x1 * x1, axis=-1, keepdims=True)
    h2 = x1 * lax.rsqrt(ms + EPS) * nw2_ref[...]
    h2 = h2 * (1.0 + sc2_ref[...]) + sh2_ref[...]
    _store_token_tiles(h2_ref, h2)
    h2_hi = h2.astype(BF16)
    h2_lo = (h2 - h2_hi.astype(F32)).astype(BF16)
    logits = (jnp.dot(h2_hi, rw_ref[0], preferred_element_type=F32)
              + jnp.dot(h2_lo, rw_ref[0], preferred_element_type=F32)
              + jnp.dot(h2_hi, rw_ref[1], preferred_element_type=F32)) + rb_ref[...]
    lane = lax.broadcasted_iota(jnp.int32, logits.shape, 1)
    vals, idxs = [], []
    for _ in range(TOP_K):
        m = jnp.max(logits, axis=-1, keepdims=True)
        am = jnp.min(jnp.where(logits == m, lane, LANES), axis=-1, keepdims=True)
        vals.append(m)
        idxs.append(am)
        logits = jnp.where(lane == am, -jnp.inf, logits)
    es = [jnp.exp(v - vals[0]) for v in vals]
    den = es[0] + es[1] + es[2] + es[3]
    @pl.when(pl.program_id(0) == 0)
    def _():
        cnt_ref[...] = jnp.zeros_like(cnt_ref)

    member = jnp.zeros(lane.shape, F32)
    for kk in range(TOP_K):
        member = jnp.where(lane == idxs[kk], 1.0, member)
    before = jnp.dot(tri_ref[...], member.astype(BF16), preferred_element_type=F32) + cnt_ref[...]
    cnt_ref[...] += jnp.sum(member, axis=0, keepdims=True)
    count_ref[...] = jnp.broadcast_to(cnt_ref[...], count_ref.shape).astype(jnp.int32)
    idx_out = jnp.zeros(lane.shape, jnp.int32)
    gate_out = jnp.zeros(lane.shape, F32)
    for kk in range(TOP_K):
        rank = jnp.sum(jnp.where(lane == idxs[kk], before, 0.0), axis=-1, keepdims=True).astype(jnp.int32)
        idx_out = jnp.where(lane == kk, idxs[kk], idx_out)
        idx_out = jnp.where(lane == TOP_K + kk, rank, idx_out)
        gate_out = jnp.where(lane == kk, es[kk] / den, gate_out)
    idx_ref[...] = idx_out
    gate_ref[...] = gate_out


def _outproj(ret_out, diff_out, x2, gate1, scale2, shift2, norm2_w, w_out, rw_pad, rb_pad, seq_len, tt):
    t_total, d = x2.shape
    nps = seq_len // tt
    row = lambda i: (i, 0)
    mod = lambda i: (i // nps, 0, 0)
    fixed = lambda i: (0, 0)
    pos = jnp.arange(tt)
    tri = (pos[None, :] < pos[:, None]).astype(BF16)
    return pl.pallas_call(
        _outproj_kernel,
        grid=(t_total // tt,),
        in_specs=[pl.BlockSpec((tt, HALF_WIDTH), row),
                  pl.BlockSpec((tt, HALF_WIDTH), row),
                  pl.BlockSpec((tt, d), row),
                  pl.BlockSpec((None, 1, d), mod),
                  pl.BlockSpec((None, 1, d), mod),
                  pl.BlockSpec((None, 1, d), mod),
                  pl.BlockSpec((1, d), fixed),
                  pl.BlockSpec((2 * HALF_WIDTH, d), fixed),
                  pl.BlockSpec((2, d, LANES), lambda i: (0, 0, 0)),
                  pl.BlockSpec((1, LANES), fixed),
                  pl.BlockSpec((tt, tt), fixed)],
        out_specs=[pl.BlockSpec((tt, d), row), pl.BlockSpec((tt * SUBLANES, LANES), row),
                   pl.BlockSpec((tt, LANES), row), pl.BlockSpec((tt, LANES), row),
                   pl.BlockSpec((SUBLANES, LANES), fixed)],
        out_shape=[jax.ShapeDtypeStruct((t_total, d), F32), jax.ShapeDtypeStruct((t_total * SUBLANES, LANES), F32),
                   jax.ShapeDtypeStruct((t_total, LANES), jnp.int32),
                   jax.ShapeDtypeStruct((t_total, LANES), F32),
                   jax.ShapeDtypeStruct((SUBLANES, LANES), jnp.int32)],
        scratch_shapes=[pltpu.VMEM((1, LANES), F32)],
        compiler_params=_cparams(("arbitrary",)),
    )(ret_out, diff_out, x2, gate1, scale2, shift2, norm2_w, w_out, rw_pad, rb_pad, tri)


def _expert_kernel(be_ref, act_ref, src_ref, src_next_ref, dst_prev_ref, h2_hbm, wgu_ref, bgu_ref, wd_ref, bd_ref,
                   out_hbm, x0, x1, y0, y1, sem_in, sem_out):
    i = pl.program_id(0)
    rows = x0.shape[0] // SUBLANES
    d_ff = wd_ref.shape[1]
    prev_active = jnp.logical_and(i > 0, act_ref[jnp.maximum(i - 1, 0)] == 1)

    def gather_copy(src, r, xbuf, sem):
        return pltpu.make_async_copy(h2_hbm.at[pl.ds(pl.multiple_of(src[r], SUBLANES), SUBLANES)],
                                     xbuf.at[pl.ds(r * SUBLANES, SUBLANES)], sem)

    @pl.when(i == 0)
    def _():
        y1[...] = jnp.zeros_like(y1)
        for r in range(rows):
            gather_copy(src_ref, r, x0, sem_in.at[0]).start()

    def body(xcur, xnext, ycur, yprev, sem_cur, sem_next):
        pltpu.make_async_copy(h2_hbm.at[pl.ds(0, rows * SUBLANES)], xcur, sem_cur).wait()

        @pl.when(prev_active)
        def _():
            pltpu.make_async_copy(ycur, out_hbm.at[pl.ds(0, rows * SUBLANES)], sem_out).wait()

        @pl.when(act_ref[i] == 1)
        def _():
            for r in range(rows):
                gather_copy(src_next_ref, r, xnext, sem_next).start()
            for r in range(rows):
                pltpu.async_copy(yprev.at[pl.ds(r * SUBLANES, SUBLANES)],
                                 out_hbm.at[pl.ds(pl.multiple_of(dst_prev_ref[r], SUBLANES), SUBLANES)],
                                 sem_out, priority=1)
            x = _load_token_tiles(xcur).astype(BF16)
            gu = jnp.dot(x, wgu_ref[0], preferred_element_type=F32) + bgu_ref[0]
            glu = jnp.minimum(gu[:, :d_ff], SWIGLU_LIMIT)
            lin = jnp.clip(gu[:, d_ff:], -SWIGLU_LIMIT, SWIGLU_LIMIT)
            hidden = glu * jax.nn.sigmoid(SWIGLU_ALPHA * glu) * (lin + 1.0)
            _store_token_tiles(ycur, jnp.dot(hidden.astype(BF16), wd_ref[0], preferred_element_type=F32) + bd_ref[0])

    rows_requested = jnp.logical_or(i == 0, prev_active)

    @pl.when(jnp.logical_and(rows_requested, i % 2 == 0))
    def _():
        body(x0, x1, y0, y1, sem_in.at[0], sem_in.at[1])

    @pl.when(jnp.logical_and(rows_requested, i % 2 == 1))
    def _():
        body(x1, x0, y1, y0, sem_in.at[1], sem_in.at[0])


def _experts(h2, block_expert, active, row_src, dst_prev, wgu, bgu, wd, bd, rows):
    t_total = h2.shape[0] // SUBLANES
    n_blocks = block_expert.shape[0]
    d, d_ff2 = wgu.shape[1:]
    assert d == SUBLANES * LANES
    last = n_blocks - 1
    buf = pltpu.VMEM((rows * SUBLANES, LANES), F32)
    grid_spec = pltpu.PrefetchScalarGridSpec(
        num_scalar_prefetch=2,
        grid=(n_blocks,),
        in_specs=[pl.BlockSpec((rows,), lambda i, be, act: (i,), memory_space=pltpu.SMEM),
                  pl.BlockSpec((rows,), lambda i, be, act: (jnp.minimum(i + 1, last),), memory_space=pltpu.SMEM),
                  pl.BlockSpec((rows,), lambda i, be, act: (i,), memory_space=pltpu.SMEM),
                  pl.BlockSpec(memory_space=pl.ANY),
                  pl.BlockSpec((1, d, d_ff2), lambda i, be, act: (be[i], 0, 0)),
                  pl.BlockSpec((1, 1, d_ff2), lambda i, be, act: (be[i], 0, 0)),
                  pl.BlockSpec((1, d_ff2 // 2, d), lambda i, be, act: (be[i], 0, 0)),
                  pl.BlockSpec((1, 1, d), lambda i, be, act: (be[i], 0, 0))],
        out_specs=pl.BlockSpec(memory_space=pl.ANY),
        scratch_shapes=[buf, buf, buf, buf, pltpu.SemaphoreType.DMA((2,)), pltpu.SemaphoreType.DMA],
    )
    return pl.pallas_call(
        _expert_kernel,
        grid_spec=grid_spec,
        out_shape=jax.ShapeDtypeStruct(((TOP_K * t_total + rows) * SUBLANES, LANES), F32),
        compiler_params=_cparams(("arbitrary",)),
    )(block_expert, active, row_src, row_src, dst_prev, h2, wgu, bgu, wd, bd)


def _routing(top_idx, rank, counts, rows):
    t_total = top_idx.shape[0]
    padded = (counts + rows - 1) // rows * rows
    ends = jnp.cumsum(padded)
    starts = ends - padded
    dest = (starts[top_idx] + rank).reshape(-1)
    n_blocks = (t_total * TOP_K + N_EXPERTS * (rows - 1)) // rows + 2
    n_rows = n_blocks * rows
    n_slots = TOP_K * t_total
    tok = jnp.repeat(jnp.arange(t_total, dtype=jnp.int32), TOP_K)
    slot = jnp.tile(jnp.arange(TOP_K, dtype=jnp.int32), t_total)
    landing = n_slots + jnp.arange(n_rows, dtype=jnp.int32) % rows
    row_dst = landing.at[dest].set(slot * t_total + tok, unique_indices=True)
    row_src = jnp.where(row_dst < n_slots, row_dst % t_total, 0)
    dst_prev = jnp.concatenate([landing[:rows], row_dst[:-rows]])
    block_start = jnp.arange(n_blocks, dtype=jnp.int32) * rows
    block_expert = jnp.sum((ends[None, :] <= block_start[:, None]).astype(jnp.int32), axis=1)
    block_expert = jnp.minimum(block_expert, N_EXPERTS - 1)
    has_rows = block_start < ends[-1]
    active = jnp.logical_or(has_rows, jnp.concatenate([has_rows[:1], has_rows[:-1]])).astype(jnp.int32)
    return block_expert, active, row_src * SUBLANES, dst_prev * SUBLANES


def _combine_kernel(x1_ref, g2_ref, gate_ref, *refs):
    y_refs, o_ref = refs[:TOP_K], refs[TOP_K]
    gates = gate_ref[...]
    moe = gates[:, 0:1] * _load_token_tiles(y_refs[0])
    for kk in range(1, TOP_K):
        moe += gates[:, kk:kk + 1] * _load_token_tiles(y_refs[kk])
    o_ref[...] = x1_ref[...] + g2_ref[...] * moe


def _combine(x1, gate2, gates, y4, seq_len, tt):
    t_total, d = x1.shape
    nps = seq_len // tt
    n_tiles = t_total // tt
    slot_specs = [pl.BlockSpec((tt * SUBLANES, LANES), functools.partial(lambda kk, i: (kk * n_tiles + i, 0), kk))
                  for kk in range(TOP_K)]
    return pl.pallas_call(
        _combine_kernel,
        grid=(n_tiles,),
        in_specs=[pl.BlockSpec((tt, d), lambda i: (i, 0)),
                  pl.BlockSpec((None, 1, d), lambda i: (i // nps, 0, 0)),
                  pl.BlockSpec((tt, LANES), lambda i: (i, 0))] + slot_specs,
        out_specs=pl.BlockSpec((tt, d), lambda i: (i, 0)),
        out_shape=jax.ShapeDtypeStruct((t_total, d), F32),
        compiler_params=_cparams(("parallel",)),
    )(x1, gate2, gates, *([y4] * TOP_K))


def _tiles(seq_len):
    token_tile = min(512, seq_len)
    ret_chunk = min(256, seq_len)
    attn_tile = min(512, seq_len)
    expert_rows = 256
    combine_tile = min(256, seq_len)
    return token_tile, ret_chunk, attn_tile, expert_rows, combine_tile


def _trunk(x, mod, lam_init, norm1_w, w_in_b, ret_a_fwd, ret_a_bwd, ret_norm_w, diff_q_norm_w, diff_k_norm_w,
           lam_q1, lam_k1, lam_q2, lam_k2, diff_subln_w, rel_bias, w_out_b, norm2_w, rw_pad, rb_pad,
           wgu_b, b_gate_up, wd_b, b_down):
    batch, seq_len, d = x.shape
    tt, c, t, rows, ct = _tiles(seq_len)
    x2 = x.reshape(batch * seq_len, d)
    shift1, scale1, gate1, shift2, scale2, gate2 = [mod[:, n][:, None, :] for n in range(6)]

    pos = jnp.arange(seq_len, dtype=F32)
    inv_freq = ROPE_BASE ** (-jnp.arange(0, HEAD_DIM, 2, dtype=F32) / HEAD_DIM)
    ang = pos[:, None] * inv_freq[None, :]
    cos_t = jnp.tile(jnp.cos(ang), (1, LANES // (HEAD_DIM // 2)))
    sin_h = jnp.sin(ang)
    sin_t = jnp.tile(jnp.concatenate([-sin_h, sin_h], axis=1), (1, LANES // HEAD_DIM))
    qnw = jnp.tile(diff_q_norm_w.astype(F32), HALF_WIDTH // HEAD_DIM)[None, :]
    knw = jnp.tile(diff_k_norm_w.astype(F32), HALF_WIDTH // HEAD_DIM)[None, :]
    r = jnp.arange(HALF_WIDTH)
    bd = jnp.where((r[:, None] // HEAD_DIM) == (r[None, :] // HEAD_DIM), 1.0 / HEAD_DIM, 0.0).astype(BF16)

    rq, rk, rv, rg, dq, dk, dv = _inproj(x2, scale1, shift1, norm1_w[None, :], w_in_b, cos_t, sin_t, qnw, knw, bd,
                                         seq_len, tt)

    tabs = _retention_tables(ret_a_fwd, ret_a_bwd, c)
    ret_out = _retention(rq, rk, rv, rg, tabs, ret_norm_w.reshape(1, HALF_WIDTH).astype(F32), batch, seq_len, c)

    lam = (jnp.exp(jnp.sum(lam_q1.astype(F32) * lam_k1.astype(F32)))
           - jnp.exp(jnp.sum(lam_q2.astype(F32) * lam_k2.astype(F32))) + lam_init)
    subln = (diff_subln_w.astype(F32) * (1.0 - lam_init))[None, :]
    diff_out = _attention(dq, dk, dv, _bias_tables(rel_bias, t), lam[None].astype(F32), subln, batch, seq_len, t)

    x1, h2, idx_pad, gates_pad, counts = _outproj(ret_out, diff_out, x2, gate1, scale2, shift2, norm2_w[None, :],
                                                  w_out_b, rw_pad, rb_pad, seq_len, tt)
    block_expert, active, row_src, dst_prev = _routing(idx_pad[:, :TOP_K], idx_pad[:, TOP_K:2 * TOP_K],
                                                       counts[0, :N_EXPERTS], rows)
    y4 = _experts(h2, block_expert, active, row_src, dst_prev, wgu_b, b_gate_up, wd_b, b_down, rows)
    y = _combine(x1, gate2, gates_pad, y4, seq_len, ct)
    return y.reshape(batch, seq_len, d)


def kernel(x_prompt, x_sample, c_prompt, c_sample, w_ada, b_ada, norm1_w, w_in, ret_a_fwd, ret_a_bwd, ret_norm_w,
           diff_q_norm_w, diff_k_norm_w, lam_q1, lam_k1, lam_q2, lam_k2, diff_subln_w, rel_bias, w_out, norm2_w,
           router_w, router_b, w_gate_up, b_gate_up, w_down, b_down):
    depth = w_ada.shape[0]
    d = x_prompt.shape[-1]
    n_prompt = c_prompt.shape[0]
    n_cond = n_prompt + c_sample.shape[0]
    c_all = jnp.concatenate([c_prompt, c_sample], axis=0).astype(F32)
    c_pad = jnp.pad(c_all, ((0, -n_cond % 8), (0, 0)))
    xs = [x_prompt, x_sample]
    for l in range(depth):
        lam_init = 0.8 - 0.6 * math.exp(-0.3 * l)
        mod = _ada(c_pad, w_ada[l], b_ada[l])[:n_cond].reshape(n_cond, 6, d)
        rw = jnp.pad(router_w[l].astype(F32), ((0, 0), (0, LANES - N_EXPERTS)))
        rw_hi = rw.astype(BF16)
        rw_pad = jnp.stack([rw_hi, (rw - rw_hi.astype(F32)).astype(BF16)])
        rb_pad = jnp.pad(router_b[l].astype(F32), (0, LANES - N_EXPERTS), constant_values=-jnp.inf)[None, :]
        shared = (norm1_w[l], w_in[l].astype(BF16), ret_a_fwd[l], ret_a_bwd[l], ret_norm_w[l], diff_q_norm_w[l],
                  diff_k_norm_w[l], lam_q1[l], lam_k1[l], lam_q2[l], lam_k2[l], diff_subln_w[l], rel_bias,
                  w_out[l].astype(BF16), norm2_w[l], rw_pad, rb_pad,
                  w_gate_up[l].astype(BF16), b_gate_up[l][:, None, :], w_down[l].astype(BF16), b_down[l][:, None, :])
        xs = [_trunk(xs[0], mod[:n_prompt], lam_init, *shared),
              _trunk(xs[1], mod[n_prompt:], lam_init, *shared)]
    return (xs[0], xs[1])
```

```python
import functools
import math

import jax
import jax.numpy as jnp
import numpy as np
from jax import lax
from jax.experimental import pallas as pl
from jax.experimental.pallas import tpu as pltpu

F32 = jnp.float32
BF16 = jnp.bfloat16
HIGHEST = lax.Precision.HIGHEST

LANES = 128
SUBLANES = 8
HEAD_DIM = 64
RET_HEADS = 8
DIFF_HEADS = 4
HALF_WIDTH = 512
N_PAIRS = HALF_WIDTH // LANES
ROPE_BASE = 10000.0
NUM_BUCKETS = 32
MAX_DISTANCE = 128
N_EXPERTS = 32
TOP_K = 4
SWIGLU_LIMIT = 7.0
SWIGLU_ALPHA = 1.702
EPS = 1e-6
LOG2E = math.log2(math.e)
ATTN_ROWS = 128
ATTN_UNROLL = 8
VMEM_LIMIT = 56 * 1024 * 1024
DMA_QUEUES = 2


def _cparams(sem):
    return pltpu.CompilerParams(dimension_semantics=sem, vmem_limit_bytes=VMEM_LIMIT)


def _ada_kernel(c_ref, w_ref, b_ref, o_ref):
    c = c_ref[...]
    a = c * jax.nn.sigmoid(c)
    o_ref[...] = jnp.dot(a, w_ref[...], preferred_element_type=F32, precision=HIGHEST) + b_ref[...]


def _ada(c_pad, w_ada, b_ada):
    rows, d = c_pad.shape
    n = w_ada.shape[1]
    tn = d
    return pl.pallas_call(
        _ada_kernel,
        grid=(n // tn,),
        in_specs=[pl.BlockSpec((rows, d), lambda j: (0, 0)),
                  pl.BlockSpec((d, tn), lambda j: (0, j)),
                  pl.BlockSpec((1, tn), lambda j: (0, j))],
        out_specs=pl.BlockSpec((rows, tn), lambda j: (0, j)),
        out_shape=jax.ShapeDtypeStruct((rows, n), F32),
        compiler_params=_cparams(("parallel",)),
    )(c_pad, w_ada, b_ada.reshape(1, n))


def _rotate_half(xg):
    lane = lax.broadcasted_iota(jnp.int32, xg.shape, 1)
    first = (lane % HEAD_DIM) < (HEAD_DIM // 2)
    return jnp.where(first, pltpu.roll(xg, LANES - HEAD_DIM // 2, 1), pltpu.roll(xg, HEAD_DIM // 2, 1))


def _inproj_kernel(x_ref, sc_ref, sh_ref, nw_ref, w_ref, cos_ref, sin_ref, qnw_ref, knw_ref, bd_ref,
                   rq_ref, rk_ref, rv_ref, rg_ref, dq_ref, dk_ref, dv_ref):
    x = x_ref[...]
    ms = jnp.mean(x * x, axis=-1, keepdims=True)
    h = x * lax.rsqrt(ms + EPS) * nw_ref[...]
    h = (h * (1.0 + sc_ref[...]) + sh_ref[...]).astype(BF16)

    def piece(n):
        return jnp.dot(h, w_ref[:, n * HALF_WIDTH:(n + 1) * HALF_WIDTH], preferred_element_type=F32)

    cos = cos_ref[...]
    sin = sin_ref[...]

    def rotary(p, out_ref, scale):
        for g in range(N_PAIRS):
            xg = p[:, g * LANES:(g + 1) * LANES]
            y = xg * cos + _rotate_half(xg) * sin
            out_ref[:, g * LANES:(g + 1) * LANES] = (y * scale).astype(BF16)

    def head_norm(p, w, out_ref, scale):
        msq = jnp.dot((p * p).astype(BF16), bd_ref[...], preferred_element_type=F32)
        out_ref[...] = (p * lax.rsqrt(msq + EPS) * w * scale).astype(BF16)

    rotary(piece(0), rq_ref, 1.0)
    rotary(piece(1), rk_ref, HEAD_DIM ** -0.5)
    rv_ref[...] = piece(2).astype(BF16)
    rg_ref[...] = piece(3).astype(BF16)
    head_norm(piece(4), qnw_ref[...], dq_ref, HEAD_DIM ** -0.5 * LOG2E)
    head_norm(piece(5), knw_ref[...], dk_ref, 1.0)
    dv = piece(6).astype(BF16)
    ones = jnp.ones((dv.shape[0], LANES), BF16)
    for hd in range(DIFF_HEADS):
        dv_ref[:, 2 * hd * LANES:(2 * hd + 1) * LANES] = dv[:, hd * LANES:(hd + 1) * LANES]
        dv_ref[:, (2 * hd + 1) * LANES:(2 * hd + 2) * LANES] = ones


def _inproj(x2, scale1, shift1, norm_w, w_in, cos_t, sin_t, qnw, knw, bd, seq_len, tt):
    t_total, d = x2.shape
    nps = seq_len // tt
    n_in = w_in.shape[1]
    row = lambda i: (i, 0)
    mod = lambda i: (i // nps, 0, 0)
    pos = lambda i: (i % nps, 0)
    fixed = lambda i: (0, 0)
    out_sd = jax.ShapeDtypeStruct((t_total, HALF_WIDTH), BF16)
    return pl.pallas_call(
        _inproj_kernel,
        grid=(t_total // tt,),
        in_specs=[pl.BlockSpec((tt, d), row),
                  pl.BlockSpec((None, 1, d), mod),
                  pl.BlockSpec((None, 1, d), mod),
                  pl.BlockSpec((1, d), fixed),
                  pl.BlockSpec((d, n_in), fixed),
                  pl.BlockSpec((tt, LANES), pos),
                  pl.BlockSpec((tt, LANES), pos),
                  pl.BlockSpec((1, HALF_WIDTH), fixed),
                  pl.BlockSpec((1, HALF_WIDTH), fixed),
                  pl.BlockSpec((HALF_WIDTH, HALF_WIDTH), fixed)],
        out_specs=[pl.BlockSpec((tt, HALF_WIDTH), row)] * 6 + [pl.BlockSpec((tt, 2 * HALF_WIDTH), row)],
        out_shape=[out_sd] * 6 + [jax.ShapeDtypeStruct((t_total, 2 * HALF_WIDTH), BF16)],
        compiler_params=_cparams(("parallel",)),
    )(x2, scale1, shift1, norm_w, w_in, cos_t, sin_t, qnw, knw, bd)


def _kv_update(state_ref, p, k, v, kdec, cdec, bdmask):
    kd = (k.astype(F32) * kdec).astype(BF16)
    kv = lax.dot_general(kd, v, (((0,), (0,)), ((), ())), preferred_element_type=F32)
    state_ref[p] = state_ref[p] * cdec + kv * bdmask


def _ret_state_kernel(k_ref, v_ref, kdec_ref, cdec_ref, bdmask_ref, sb_ref, state_ref):
    @pl.when(pl.program_id(1) == 0)
    def _():
        state_ref[...] = jnp.zeros_like(state_ref)

    sb_ref[...] = state_ref[...]
    for p in range(N_PAIRS):
        sl = slice(p * LANES, (p + 1) * LANES)
        _kv_update(state_ref, p, k_ref[:, sl], v_ref[:, sl], kdec_ref[:, sl], cdec_ref[:, sl], bdmask_ref[...])


def _ret_main_kernel(q_ref, k_ref, v_ref, g_ref, dmat_ref, qdf_ref, qdb_ref, kdf_ref, cdf_ref, bdmask_ref,
                     nw_ref, sb_ref, o_ref, state_ref):
    @pl.when(pl.program_id(1) == 0)
    def _():
        state_ref[...] = jnp.zeros_like(state_ref)

    c = q_ref.shape[0]
    lane = lax.broadcasted_iota(jnp.int32, (c, LANES), 1)
    lo = lane < HEAD_DIM
    for p in range(N_PAIRS):
        sl = slice(p * LANES, (p + 1) * LANES)
        q = q_ref[:, sl]
        k = k_ref[:, sl]
        v = v_ref[:, sl]
        qf = q.astype(F32)
        acc = jnp.dot((qf * qdf_ref[:, sl]).astype(BF16), state_ref[p].astype(BF16), preferred_element_type=F32)
        acc += jnp.dot((qf * qdb_ref[:, sl]).astype(BF16), sb_ref[p].astype(BF16), preferred_element_type=F32)
        for hh in range(2):
            sel = lo if hh == 0 else jnp.logical_not(lo)
            qm = jnp.where(sel, q, jnp.zeros_like(q))
            vm = jnp.where(sel, v, jnp.zeros_like(v))
            s = lax.dot_general(qm, k, (((1,), (1,)), ((), ())), preferred_element_type=F32)
            w = (s * dmat_ref[2 * p + hh]).astype(BF16)
            acc += jnp.dot(w, vm, preferred_element_type=F32)
        _kv_update(state_ref, p, k, v, kdf_ref[:, sl], cdf_ref[:, sl], bdmask_ref[...])
        sq = acc * acc
        ms_lo = jnp.sum(jnp.where(lo, sq, 0.0), axis=-1, keepdims=True)
        ms_hi = jnp.sum(jnp.where(lo, 0.0, sq), axis=-1, keepdims=True)
        ms = jnp.where(lo, ms_lo, ms_hi) * (1.0 / HEAD_DIM)
        y = acc * lax.rsqrt(ms + EPS) * nw_ref[:, sl]
        gf = g_ref[:, sl].astype(F32)
        o_ref[:, sl] = (gf * jax.nn.sigmoid(gf) * y).astype(BF16)


def _retention(rq, rk, rv, rg, tabs, ret_nw, batch, seq_len, c):
    t_total = rq.shape[0]
    nc = seq_len // c
    dmat, qdf, qdb, kdf, kdb, cdf, cdb, bdmask = tabs
    fixed2 = lambda b, i: (0, 0)
    rev = lambda b, i: (b * nc + nc - 1 - i, 0)
    fwd = lambda b, i: (b * nc + i, 0)
    tile = pl.BlockSpec((c, HALF_WIDTH), fwd)
    tile_rev = pl.BlockSpec((c, HALF_WIDTH), rev)
    tab = pl.BlockSpec((c, HALF_WIDTH), fixed2)
    vec = pl.BlockSpec((1, HALF_WIDTH), fixed2)
    mask = pl.BlockSpec((LANES, LANES), fixed2)
    state = pltpu.VMEM((N_PAIRS, LANES, LANES), F32)
    sb = pl.pallas_call(
        _ret_state_kernel,
        grid=(batch, nc),
        in_specs=[tile_rev, tile_rev, tab, vec, mask],
        out_specs=pl.BlockSpec((None, None, N_PAIRS, LANES, LANES), lambda b, i: (b, nc - 1 - i, 0, 0, 0)),
        out_shape=jax.ShapeDtypeStruct((batch, nc, N_PAIRS, LANES, LANES), F32),
        scratch_shapes=[state],
        compiler_params=_cparams(("parallel", "arbitrary")),
    )(rk, rv, kdb, cdb, bdmask)
    return pl.pallas_call(
        _ret_main_kernel,
        grid=(batch, nc),
        in_specs=[tile, tile, tile, tile,
                  pl.BlockSpec((RET_HEADS, c, c), lambda b, i: (0, 0, 0)),
                  tab, tab, tab, vec, mask, vec,
                  pl.BlockSpec((None, None, N_PAIRS, LANES, LANES), lambda b, i: (b, i, 0, 0, 0))],
        out_specs=tile,
        out_shape=jax.ShapeDtypeStruct((t_total, HALF_WIDTH), BF16),
        scratch_shapes=[state],
        compiler_params=_cparams(("parallel", "arbitrary")),
    )(rq, rk, rv, rg, dmat, qdf, qdb, kdf, cdf, bdmask, ret_nw, sb)


def _retention_tables(a_fwd, a_bwd, c):
    lg_f = jnp.log1p(-jnp.exp(a_fwd.astype(F32)))
    lg_b = jnp.log1p(-jnp.exp(a_bwd.astype(F32)))
    pos = jnp.arange(c, dtype=F32)
    diff = pos[:, None] - pos[None, :]
    dmat = jnp.where(diff[None] >= 0,
                     jnp.exp(jnp.maximum(diff, 0.0)[None] * lg_f[:, None, None]),
                     jnp.exp(jnp.maximum(-diff, 0.0)[None] * lg_b[:, None, None]))
    lane_f = jnp.repeat(lg_f, HEAD_DIM)[None, :]
    lane_b = jnp.repeat(lg_b, HEAD_DIM)[None, :]
    qdf = jnp.exp((pos + 1.0)[:, None] * lane_f)
    kdf = jnp.exp((c - 1.0 - pos)[:, None] * lane_f)
    qdb = jnp.exp((c - pos)[:, None] * lane_b)
    kdb = jnp.exp(pos[:, None] * lane_b)
    cdf = jnp.exp(c * lane_f)
    cdb = jnp.exp(c * lane_b)
    r = jnp.arange(LANES)
    bdmask = ((r[:, None] // HEAD_DIM) == (r[None, :] // HEAD_DIM)).astype(F32)
    return dmat, qdf, qdb, kdf, kdb, cdf, cdb, bdmask


def _attn_kernel(lam_ref, q_ref, k_ref, v_ref, bias_ref, sw_ref, o_ref, qq_ref, m_ref, acc_ref):
    i = pl.program_id(2)
    t = q_ref.shape[0]
    nk = k_ref.shape[0] // t
    q = q_ref[...]
    lane = lax.broadcasted_iota(jnp.int32, q.shape, 1)
    zero = jnp.zeros_like(q)
    qq_ref[...] = jnp.concatenate([jnp.where(lane < HEAD_DIM, q, zero), jnp.where(lane < HEAD_DIM, zero, q)], axis=0)
    m_ref[...] = jnp.full_like(m_ref, -jnp.inf)
    acc_ref[...] = jnp.zeros_like(acc_ref)

    def tile(j):
        keys = pl.ds(pl.multiple_of(j * t, t), t)
        which = jnp.clip(j - i, -2, 2) + 2
        for r in range(0, 2 * t, ATTN_ROWS):
            rows = slice(r, r + ATTN_ROWS)
            s = lax.dot_general(qq_ref[rows, :], k_ref[keys, :], (((1,), (1,)), ((), ())),
                                preferred_element_type=F32)
            s = s + bias_ref[which, r % t:r % t + ATTN_ROWS, :]
            m_prev = m_ref[rows, :]
            m_new = jnp.maximum(m_prev, jnp.max(s, axis=1, keepdims=True))
            alpha = jnp.exp2(m_prev - m_new)
            p = jnp.exp2(s - jnp.concatenate([m_new] * (t // LANES), axis=1)).astype(BF16)
            pv = jnp.dot(p, v_ref[keys, :], preferred_element_type=F32)
            acc_ref[rows, :] = jnp.concatenate([alpha, alpha], axis=1) * acc_ref[rows, :] + pv
            m_ref[rows, :] = m_new

    def step(jj, carry):
        for u in range(ATTN_UNROLL):
            tile(jj * ATTN_UNROLL + u)
        return carry

    lax.fori_loop(0, nk // ATTN_UNROLL, step, 0)

    acc = acc_ref[...]
    o = acc[:, :LANES] / acc[:, LANES:]
    att = o[:t] - lam_ref[0] * o[t:]
    msq = jnp.mean(att * att, axis=-1, keepdims=True)
    o_ref[...] = (att * lax.rsqrt(msq + EPS) * sw_ref[...]).astype(BF16)


def _attention(dq, dk, dv, bias_tiles, lam, subln_w, batch, seq_len, t):
    t_total = dq.shape[0]
    nq = seq_len // t
    assert nq % ATTN_UNROLL == 0 and t % ATTN_ROWS == 0
    grid_spec = pltpu.PrefetchScalarGridSpec(
        num_scalar_prefetch=1,
        grid=(batch, DIFF_HEADS, nq),
        in_specs=[pl.BlockSpec((t, LANES), lambda b, h, i, s: (b * nq + i, h)),
                  pl.BlockSpec((seq_len, LANES), lambda b, h, i, s: (b, h)),
                  pl.BlockSpec((seq_len, 2 * LANES), lambda b, h, i, s: (b, h)),
                  pl.BlockSpec((None, 5, t, t), lambda b, h, i, s: (h, 0, 0, 0)),
                  pl.BlockSpec((1, LANES), lambda b, h, i, s: (0, 0))],
        out_specs=pl.BlockSpec((t, LANES), lambda b, h, i, s: (b * nq + i, h)),
        scratch_shapes=[pltpu.VMEM((2 * t, LANES), BF16), pltpu.VMEM((2 * t, LANES), F32),
                        pltpu.VMEM((2 * t, 2 * LANES), F32)],
    )
    return pl.pallas_call(
        _attn_kernel,
        grid_spec=grid_spec,
        out_shape=jax.ShapeDtypeStruct((t_total, HALF_WIDTH), BF16),
        compiler_params=_cparams(("parallel", "parallel", "parallel")),
    )(lam, dq, dk, dv, bias_tiles, subln_w)


def _t5_bucket(rel):
    nb = NUM_BUCKETS // 2
    max_exact = nb // 2
    n = jnp.abs(rel)
    base = jnp.where(rel > 0, nb, 0)
    nf = jnp.maximum(n, 1).astype(F32)
    large = max_exact + (jnp.log(nf / max_exact) / math.log(MAX_DISTANCE / max_exact)
                         * (nb - max_exact)).astype(jnp.int32)
    large = jnp.minimum(large, nb - 1)
    return base + jnp.where(n < max_exact, n, large)


def _bias_tables(rel_bias, t):
    assert t >= MAX_DISTANCE
    table = rel_bias.astype(F32) * LOG2E
    qq = jnp.arange(t, dtype=jnp.int32)[:, None]
    kk = jnp.arange(t, dtype=jnp.int32)[None, :]
    bucket = jnp.stack([_t5_bucket(d * t + kk - qq) for d in (-2, -1, 0, 1, 2)], axis=0)
    onehot = (bucket[..., None] == jnp.arange(NUM_BUCKETS, dtype=jnp.int32)).astype(F32)
    return jnp.einsum('dqkn,nh->hdqk', onehot, table, precision=HIGHEST)


def _store_token_tiles(ref, value):
    n = value.shape[0]
    for s in range(SUBLANES):
        ref[pl.ds(s, n, stride=SUBLANES), :] = value[:, s * LANES:(s + 1) * LANES]


def _load_token_tiles(ref):
    n = ref.shape[0] // SUBLANES
    return jnp.concatenate([ref[pl.ds(s, n, stride=SUBLANES), :] for s in range(SUBLANES)], axis=1)


def _outproj_kernel(ret_ref, dif_ref, x_ref, g1_ref, sc2_ref, sh2_ref, nw2_ref, wo_ref, rw_ref, rb_ref, tri_ref,
                    x1_ref, h2_ref, idx_ref, gate_ref, count_ref, cnt_ref):
    mix = jnp.dot(ret_ref[...], wo_ref[:HALF_WIDTH, :], preferred_element_type=F32)
    mix += jnp.dot(dif_ref[...], wo_ref[HALF_WIDTH:, :], preferred_element_type=F32)
    x1 = x_ref[...] + g1_ref[...] * mix
    x1_ref[...] = x1
    ms = jnp.mean(x1 * x1, axis=-1, keepdims=True)
    h2 = x1 * lax.rsqrt(ms + EPS) * nw2_ref[...]
    h2 = h2 * (1.0 + sc2_ref[...]) + sh2_ref[...]
    _store_token_tiles(h2_ref, h2)
    h2_hi = h2.astype(BF16)
    h2_lo = (h2 - h2_hi.astype(F32)).astype(BF16)
    logits = (jnp.dot(h2_hi, rw_ref[0], preferred_element_type=F32)
              + jnp.dot(h2_lo, rw_ref[0], preferred_element_type=F32)
              + jnp.dot(h2_hi, rw_ref[1], preferred_element_type=F32)) + rb_ref[...]
    lane = lax.broadcasted_iota(jnp.int32, logits.shape, 1)
    vals, idxs = [], []
    for _ in range(TOP_K):
        m = jnp.max(logits, axis=-1, keepdims=True)
        am = jnp.min(jnp.where(logits == m, lane, LANES), axis=-1, keepdims=True)
        vals.append(m)
        idxs.append(am)
        logits = jnp.where(lane == am, -jnp.inf, logits)
    es = [jnp.exp(v - vals[0]) for v in vals]
    den = es[0] + es[1] + es[2] + es[3]
    @pl.when(pl.program_id(0) == 0)
    def _():
        cnt_ref[...] = jnp.zeros_like(cnt_ref)

    member = jnp.zeros(lane.shape, F32)
    for kk in range(TOP_K):
        member = jnp.where(lane == idxs[kk], 1.0, member)
    before = jnp.dot(tri_ref[...], member.astype(BF16), preferred_element_type=F32) + cnt_ref[...]
    cnt_ref[...] += jnp.sum(member, axis=0, keepdims=True)
    count_ref[...] = jnp.broadcast_to(cnt_ref[...], count_ref.shape).astype(jnp.int32)
    idx_out = jnp.zeros(lane.shape, jnp.int32)
    gate_out = jnp.zeros(lane.shape, F32)
    for kk in range(TOP_K):
        rank = jnp.sum(jnp.where(lane == idxs[kk], before, 0.0), axis=-1, keepdims=True).astype(jnp.int32)
        idx_out = jnp.where(lane == kk, idxs[kk], idx_out)
        idx_out = jnp.where(lane == TOP_K + kk, rank, idx_out)
        gate_out = jnp.where(lane == kk, es[kk] / den, gate_out)
    idx_ref[...] = idx_out
    gate_ref[...] = gate_out


def _outproj(ret_out, diff_out, x2, gate1, scale2, shift2, norm2_w, w_out, rw_pad, rb_pad, seq_len, tt):
    t_total, d = x2.shape
    nps = seq_len // tt
    row = lambda i: (i, 0)
    mod = lambda i: (i // nps, 0, 0)
    fixed = lambda i: (0, 0)
    pos = jnp.arange(tt)
    tri = (pos[None, :] < pos[:, None]).astype(BF16)
    return pl.pallas_call(
        _outproj_kernel,
        grid=(t_total // tt,),
        in_specs=[pl.BlockSpec((tt, HALF_WIDTH), row),
                  pl.BlockSpec((tt, HALF_WIDTH), row),
                  pl.BlockSpec((tt, d), row),
                  pl.BlockSpec((None, 1, d), mod),
                  pl.BlockSpec((None, 1, d), mod),
                  pl.BlockSpec((None, 1, d), mod),
                  pl.BlockSpec((1, d), fixed),
                  pl.BlockSpec((2 * HALF_WIDTH, d), fixed),
                  pl.BlockSpec((2, d, LANES), lambda i: (0, 0, 0)),
                  pl.BlockSpec((1, LANES), fixed),
                  pl.BlockSpec((tt, tt), fixed)],
        out_specs=[pl.BlockSpec((tt, d), row), pl.BlockSpec((tt * SUBLANES, LANES), row),
                   pl.BlockSpec((tt, LANES), row), pl.BlockSpec((tt, LANES), row),
                   pl.BlockSpec((SUBLANES, LANES), fixed)],
        out_shape=[jax.ShapeDtypeStruct((t_total, d), F32), jax.ShapeDtypeStruct((t_total * SUBLANES, LANES), F32),
                   jax.ShapeDtypeStruct((t_total, LANES), jnp.int32),
                   jax.ShapeDtypeStruct((t_total, LANES), F32),
                   jax.ShapeDtypeStruct((SUBLANES, LANES), jnp.int32)],
        scratch_shapes=[pltpu.VMEM((1, LANES), F32)],
        compiler_params=_cparams(("arbitrary",)),
    )(ret_out, diff_out, x2, gate1, scale2, shift2, norm2_w, w_out, rw_pad, rb_pad, tri)


def _expert_kernel(be_ref, act_ref, src_ref, src_next_ref, dst_prev_ref, h2_hbm, wgu_ref, bgu_ref, wd_ref, bd_ref,
                   out_hbm, x0, x1, y0, y1, sem_in, sem_out):
    i = pl.program_id(0)
    rows = x0.shape[0] // SUBLANES
    d_ff = wd_ref.shape[1]
    prev_active = jnp.logical_and(i > 0, act_ref[jnp.maximum(i - 1, 0)] == 1)

    def start_gather(src, r, xbuf, sem):
        pltpu.async_copy(h2_hbm.at[pl.ds(pl.multiple_of(src[r], SUBLANES), SUBLANES)],
                         xbuf.at[pl.ds(r * SUBLANES, SUBLANES)], sem, priority=r % DMA_QUEUES)

    @pl.when(i == 0)
    def _():
        y1[...] = jnp.zeros_like(y1)
        for r in range(rows):
            start_gather(src_ref, r, x0, sem_in.at[0])

    def body(xcur, xnext, ycur, yprev, sem_cur, sem_next):
        pltpu.make_async_copy(h2_hbm.at[pl.ds(0, rows * SUBLANES)], xcur, sem_cur).wait()

        @pl.when(prev_active)
        def _():
            pltpu.make_async_copy(ycur, out_hbm.at[pl.ds(0, rows * SUBLANES)], sem_out).wait()

        @pl.when(act_ref[i] == 1)
        def _():
            for r in range(rows):
                start_gather(src_next_ref, r, xnext, sem_next)
            for r in range(rows):
                pltpu.async_copy(yprev.at[pl.ds(r * SUBLANES, SUBLANES)],
                                 out_hbm.at[pl.ds(pl.multiple_of(dst_prev_ref[r], SUBLANES), SUBLANES)],
                                 sem_out, priority=(r + 1) % DMA_QUEUES)
            x = _load_token_tiles(xcur).astype(BF16)
            gu = jnp.dot(x, wgu_ref[0], preferred_element_type=F32) + bgu_ref[0]
            glu = jnp.minimum(gu[:, :d_ff], SWIGLU_LIMIT)
            lin = jnp.clip(gu[:, d_ff:], -SWIGLU_LIMIT, SWIGLU_LIMIT)
            hidden = glu * jax.nn.sigmoid(SWIGLU_ALPHA * glu) * (lin + 1.0)
            _store_token_tiles(ycur, jnp.dot(hidden.astype(BF16), wd_ref[0], preferred_element_type=F32) + bd_ref[0])

    rows_requested = jnp.logical_or(i == 0, prev_active)

    @pl.when(jnp.logical_and(rows_requested, i % 2 == 0))
    def _():
        body(x0, x1, y0, y1, sem_in.at[0], sem_in.at[1])

    @pl.when(jnp.logical_and(rows_requested, i % 2 == 1))
    def _():
        body(x1, x0, y1, y0, sem_in.at[1], sem_in.at[0])


def _experts(h2, block_expert, active, row_src, dst_prev, wgu, bgu, wd, bd, rows):
    t_total = h2.shape[0] // SUBLANES
    n_blocks = block_expert.shape[0]
    d, d_ff2 = wgu.shape[1:]
    assert d == SUBLANES * LANES
    last = n_blocks - 1
    buf = pltpu.VMEM((rows * SUBLANES, LANES), F32)
    grid_spec = pltpu.PrefetchScalarGridSpec(
        num_scalar_prefetch=2,
        grid=(n_blocks,),
        in_specs=[pl.BlockSpec((rows,), lambda i, be, act: (i,), memory_space=pltpu.SMEM),
                  pl.BlockSpec((rows,), lambda i, be, act: (jnp.minimum(i + 1, last),), memory_space=pltpu.SMEM),
                  pl.BlockSpec((rows,), lambda i, be, act: (i,), memory_space=pltpu.SMEM),
                  pl.BlockSpec(memory_space=pl.ANY),
                  pl.BlockSpec((1, d, d_ff2), lambda i, be, act: (be[i], 0, 0)),
                  pl.BlockSpec((1, 1, d_ff2), lambda i, be, act: (be[i], 0, 0)),
                  pl.BlockSpec((1, d_ff2 // 2, d), lambda i, be, act: (be[i], 0, 0)),
                  pl.BlockSpec((1, 1, d), lambda i, be, act: (be[i], 0, 0))],
        out_specs=pl.BlockSpec(memory_space=pl.ANY),
        scratch_shapes=[buf, buf, buf, buf, pltpu.SemaphoreType.DMA((2,)), pltpu.SemaphoreType.DMA],
    )
    return pl.pallas_call(
        _expert_kernel,
        grid_spec=grid_spec,
        out_shape=jax.ShapeDtypeStruct(((TOP_K * t_total + rows) * SUBLANES, LANES), F32),
        compiler_params=_cparams(("arbitrary",)),
    )(block_expert, active, row_src, row_src, dst_prev, h2, wgu, bgu, wd, bd)


def _routing(top_idx, rank, counts, rows):
    t_total = top_idx.shape[0]
    padded = (counts + rows - 1) // rows * rows
    ends = jnp.cumsum(padded)
    starts = ends - padded
    dest = (starts[top_idx] + rank).reshape(-1)
    n_blocks = (t_total * TOP_K + N_EXPERTS * (rows - 1)) // rows + 2
    n_rows = n_blocks * rows
    n_slots = TOP_K * t_total
    tok = jnp.repeat(jnp.arange(t_total, dtype=jnp.int32), TOP_K)
    slot = jnp.tile(jnp.arange(TOP_K, dtype=jnp.int32), t_total)
    landing = n_slots + jnp.arange(n_rows, dtype=jnp.int32) % rows
    row_dst = landing.at[dest].set(slot * t_total + tok, unique_indices=True)
    row_src = jnp.where(row_dst < n_slots, row_dst % t_total, 0)
    dst_prev = jnp.concatenate([landing[:rows], row_dst[:-rows]])
    block_start = jnp.arange(n_blocks, dtype=jnp.int32) * rows
    block_expert = jnp.sum((ends[None, :] <= block_start[:, None]).astype(jnp.int32), axis=1)
    block_expert = jnp.minimum(block_expert, N_EXPERTS - 1)
    has_rows = block_start < ends[-1]
    active = jnp.logical_or(has_rows, jnp.concatenate([has_rows[:1], has_rows[:-1]])).astype(jnp.int32)
    return block_expert, active, row_src * SUBLANES, dst_prev * SUBLANES


def _combine_kernel(x1_ref, g2_ref, gate_ref, *refs):
    y_refs, o_ref = refs[:TOP_K], refs[TOP_K]
    gates = gate_ref[...]
    moe = gates[:, 0:1] * _load_token_tiles(y_refs[0])
    for kk in range(1, TOP_K):
        moe += gates[:, kk:kk + 1] * _load_token_tiles(y_refs[kk])
    o_ref[...] = x1_ref[...] + g2_ref[...] * moe


def _combine(x1, gate2, gates, y4, seq_len, tt):
    t_total, d = x1.shape
    nps = seq_len // tt
    n_tiles = t_total // tt
    slot_specs = [pl.BlockSpec((tt * SUBLANES, LANES), functools.partial(lambda kk, i: (kk * n_tiles + i, 0), kk))
                  for kk in range(TOP_K)]
    return pl.pallas_call(
        _combine_kernel,
        grid=(n_tiles,),
        in_specs=[pl.BlockSpec((tt, d), lambda i: (i, 0)),
                  pl.BlockSpec((None, 1, d), lambda i: (i // nps, 0, 0)),
                  pl.BlockSpec((tt, LANES), lambda i: (i, 0))] + slot_specs,
        out_specs=pl.BlockSpec((tt, d), lambda i: (i, 0)),
        out_shape=jax.ShapeDtypeStruct((t_total, d), F32),
        compiler_params=_cparams(("parallel",)),
    )(x1, gate2, gates, *([y4] * TOP_K))


def _tiles(seq_len):
    token_tile = min(512, seq_len)
    ret_chunk = min(256, seq_len)
    attn_tile = min(512, seq_len)
    expert_rows = 256
    combine_tile = min(256, seq_len)
    return token_tile, ret_chunk, attn_tile, expert_rows, combine_tile


def _trunk(x, mod, lam_init, norm1_w, w_in_b, ret_a_fwd, ret_a_bwd, ret_norm_w, diff_q_norm_w, diff_k_norm_w,
           lam_q1, lam_k1, lam_q2, lam_k2, diff_subln_w, rel_bias, w_out_b, norm2_w, rw_pad, rb_pad,
           wgu_b, b_gate_up, wd_b, b_down):
    batch, seq_len, d = x.shape
    tt, c, t, rows, ct = _tiles(seq_len)
    x2 = x.reshape(batch * seq_len, d)
    shift1, scale1, gate1, shift2, scale2, gate2 = [mod[:, n][:, None, :] for n in range(6)]

    pos = jnp.arange(seq_len, dtype=F32)
    inv_freq = ROPE_BASE ** (-jnp.arange(0, HEAD_DIM, 2, dtype=F32) / HEAD_DIM)
    ang = pos[:, None] * inv_freq[None, :]
    cos_t = jnp.tile(jnp.cos(ang), (1, LANES // (HEAD_DIM // 2)))
    sin_h = jnp.sin(ang)
    sin_t = jnp.tile(jnp.concatenate([-sin_h, sin_h], axis=1), (1, LANES // HEAD_DIM))
    qnw = jnp.tile(diff_q_norm_w.astype(F32), HALF_WIDTH // HEAD_DIM)[None, :]
    knw = jnp.tile(diff_k_norm_w.astype(F32), HALF_WIDTH // HEAD_DIM)[None, :]
    r = jnp.arange(HALF_WIDTH)
    bd = jnp.where((r[:, None] // HEAD_DIM) == (r[None, :] // HEAD_DIM), 1.0 / HEAD_DIM, 0.0).astype(BF16)

    rq, rk, rv, rg, dq, dk, dv = _inproj(x2, scale1, shift1, norm1_w[None, :], w_in_b, cos_t, sin_t, qnw, knw, bd,
                                         seq_len, tt)

    tabs = _retention_tables(ret_a_fwd, ret_a_bwd, c)
    ret_out = _retention(rq, rk, rv, rg, tabs, ret_norm_w.reshape(1, HALF_WIDTH).astype(F32), batch, seq_len, c)

    lam = (jnp.exp(jnp.sum(lam_q1.astype(F32) * lam_k1.astype(F32)))
           - jnp.exp(jnp.sum(lam_q2.astype(F32) * lam_k2.astype(F32))) + lam_init)
    subln = (diff_subln_w.astype(F32) * (1.0 - lam_init))[None, :]
    diff_out = _attention(dq, dk, dv, _bias_tables(rel_bias, t), lam[None].astype(F32), subln, batch, seq_len, t)

    x1, h2, idx_pad, gates_pad, counts = _outproj(ret_out, diff_out, x2, gate1, scale2, shift2, norm2_w[None, :],
                                                  w_out_b, rw_pad, rb_pad, seq_len, tt)
    block_expert, active, row_src, dst_prev = _routing(idx_pad[:, :TOP_K], idx_pad[:, TOP_K:2 * TOP_K],
                                                       counts[0, :N_EXPERTS], rows)
    y4 = _experts(h2, block_expert, active, row_src, dst_prev, wgu_b, b_gate_up, wd_b, b_down, rows)
    y = _combine(x1, gate2, gates_pad, y4, seq_len, ct)
    return y.reshape(batch, seq_len, d)


def kernel(x_prompt, x_sample, c_prompt, c_sample, w_ada, b_ada, norm1_w, w_in, ret_a_fwd, ret_a_bwd, ret_norm_w,
           diff_q_norm_w, diff_k_norm_w, lam_q1, lam_k1, lam_q2, lam_k2, diff_subln_w, rel_bias, w_out, norm2_w,
           router_w, router_b, w_gate_up, b_gate_up, w_down, b_down):
    depth = w_ada.shape[0]
    d = x_prompt.shape[-1]
    n_prompt = c_prompt.shape[0]
    n_cond = n_prompt + c_sample.shape[0]
    c_all = jnp.concatenate([c_prompt, c_sample], axis=0).astype(F32)
    c_pad = jnp.pad(c_all, ((0, -n_cond % 8), (0, 0)))
    xs = [x_prompt, x_sample]
    for l in range(depth):
        lam_init = 0.8 - 0.6 * math.exp(-0.3 * l)
        mod = _ada(c_pad, w_ada[l], b_ada[l])[:n_cond].reshape(n_cond, 6, d)
        rw = jnp.pad(router_w[l].astype(F32), ((0, 0), (0, LANES - N_EXPERTS)))
        rw_hi = rw.astype(BF16)
        rw_pad = jnp.stack([rw_hi, (rw - rw_hi.astype(F32)).astype(BF16)])
        rb_pad = jnp.pad(router_b[l].astype(F32), (0, LANES - N_EXPERTS), constant_values=-jnp.inf)[None, :]
        shared = (norm1_w[l], w_in[l].astype(BF16), ret_a_fwd[l], ret_a_bwd[l], ret_norm_w[l], diff_q_norm_w[l],
                  diff_k_norm_w[l], lam_q1[l], lam_k1[l], lam_q2[l], lam_k2[l], diff_subln_w[l], rel_bias,
                  w_out[l].astype(BF16), norm2_w[l], rw_pad, rb_pad,
                  w_gate_up[l].astype(BF16), b_gate_up[l][:, None, :], w_down[l].astype(BF16), b_down[l][:, None, :])
        xs = [_trunk(xs[0], mod[:n_prompt], lam_init, *shared),
              _trunk(xs[1], mod[n_prompt:], lam_init, *shared)]
    return (xs[0], xs[1])
```

```python
import functools
import math

import jax
import jax.numpy as jnp
import numpy as np
from jax import lax
from jax.experimental import pallas as pl
from jax.experimental.pallas import tpu as pltpu

F32 = jnp.float32
BF16 = jnp.bfloat16
HIGHEST = lax.Precision.HIGHEST

LANES = 128
SUBLANES = 8
HEAD_DIM = 64
RET_HEADS = 8
DIFF_HEADS = 4
HALF_WIDTH = 512
N_PAIRS = HALF_WIDTH // LANES
ROPE_BASE = 10000.0
NUM_BUCKETS = 32
MAX_DISTANCE = 128
N_EXPERTS = 32
TOP_K = 4
SWIGLU_LIMIT = 7.0
SWIGLU_ALPHA = 1.702
EPS = 1e-6
LOG2E = math.log2(math.e)
ATTN_ROWS = 128
ATTN_UNROLL = 8
VMEM_LIMIT = 56 * 1024 * 1024
DMA_QUEUES = 2


def _cparams(sem):
    return pltpu.CompilerParams(dimension_semantics=sem, vmem_limit_bytes=VMEM_LIMIT)


def _ada_kernel(c_ref, w_ref, b_ref, o_ref):
    c = c_ref[...]
    a = c * jax.nn.sigmoid(c)
    o_ref[...] = jnp.dot(a, w_ref[...], preferred_element_type=F32, precision=HIGHEST) + b_ref[...]


def _ada(c_pad, w_ada, b_ada):
    rows, d = c_pad.shape
    n = w_ada.shape[1]
    tn = d
    return pl.pallas_call(
        _ada_kernel,
        grid=(n // tn,),
        in_specs=[pl.BlockSpec((rows, d), lambda j: (0, 0)),
                  pl.BlockSpec((d, tn), lambda j: (0, j)),
                  pl.BlockSpec((1, tn), lambda j: (0, j))],
        out_specs=pl.BlockSpec((rows, tn), lambda j: (0, j)),
        out_shape=jax.ShapeDtypeStruct((rows, n), F32),
        compiler_params=_cparams(("parallel",)),
    )(c_pad, w_ada, b_ada.reshape(1, n))


def _rotate_half(xg):
    lane = lax.broadcasted_iota(jnp.int32, xg.shape, 1)
    first = (lane % HEAD_DIM) < (HEAD_DIM // 2)
    return jnp.where(first, pltpu.roll(xg, LANES - HEAD_DIM // 2, 1), pltpu.roll(xg, HEAD_DIM // 2, 1))


def _inproj_kernel(x_ref, sc_ref, sh_ref, nw_ref, w_ref, cos_ref, sin_ref, qnw_ref, knw_ref, bd_ref,
                   rq_ref, rk_ref, rv_ref, rg_ref, dq_ref, dk_ref, dv_ref):
    x = x_ref[...]
    ms = jnp.mean(x * x, axis=-1, keepdims=True)
    h = x * lax.rsqrt(ms + EPS) * nw_ref[...]
    h = (h * (1.0 + sc_ref[...]) + sh_ref[...]).astype(BF16)

    def piece(n):
        return jnp.dot(h, w_ref[:, n * HALF_WIDTH:(n + 1) * HALF_WIDTH], preferred_element_type=F32)

    cos = cos_ref[...]
    sin = sin_ref[...]

    def rotary(p, out_ref, scale):
        for g in range(N_PAIRS):
            xg = p[:, g * LANES:(g + 1) * LANES]
            y = xg * cos + _rotate_half(xg) * sin
            out_ref[:, g * LANES:(g + 1) * LANES] = (y * scale).astype(BF16)

    def head_norm(p, w, out_ref, scale):
        msq = jnp.dot((p * p).astype(BF16), bd_ref[...], preferred_element_type=F32)
        out_ref[...] = (p * lax.rsqrt(msq + EPS) * w * scale).astype(BF16)

    rotary(piece(0), rq_ref, 1.0)
    rotary(piece(1), rk_ref, HEAD_DIM ** -0.5)
    rv_ref[...] = piece(2).astype(BF16)
    rg_ref[...] = piece(3).astype(BF16)
    head_norm(piece(4), qnw_ref[...], dq_ref, HEAD_DIM ** -0.5 * LOG2E)
    head_norm(piece(5), knw_ref[...], dk_ref, 1.0)
    dv = piece(6).astype(BF16)
    ones = jnp.ones((dv.shape[0], LANES), BF16)
    for hd in range(DIFF_HEADS):
        dv_ref[:, 2 * hd * LANES:(2 * hd + 1) * LANES] = dv[:, hd * LANES:(hd + 1) * LANES]
        dv_ref[:, (2 * hd + 1) * LANES:(2 * hd + 2) * LANES] = ones


def _inproj(x2, scale1, shift1, norm_w, w_in, cos_t, sin_t, qnw, knw, bd, seq_len, tt):
    t_total, d = x2.shape
    nps = seq_len // tt
    n_in = w_in.shape[1]
    row = lambda i: (i, 0)
    mod = lambda i: (i // nps, 0, 0)
    pos = lambda i: (i % nps, 0)
    fixed = lambda i: (0, 0)
    out_sd = jax.ShapeDtypeStruct((t_total, HALF_WIDTH), BF16)
    return pl.pallas_call(
        _inproj_kernel,
        grid=(t_total // tt,),
        in_specs=[pl.BlockSpec((tt, d), row),
                  pl.BlockSpec((None, 1, d), mod),
                  pl.BlockSpec((None, 1, d), mod),
                  pl.BlockSpec((1, d), fixed),
                  pl.BlockSpec((d, n_in), fixed),
                  pl.BlockSpec((tt, LANES), pos),
                  pl.BlockSpec((tt, LANES), pos),
                  pl.BlockSpec((1, HALF_WIDTH), fixed),
                  pl.BlockSpec((1, HALF_WIDTH), fixed),
                  pl.BlockSpec((HALF_WIDTH, HALF_WIDTH), fixed)],
        out_specs=[pl.BlockSpec((tt, HALF_WIDTH), row)] * 6 + [pl.BlockSpec((tt, 2 * HALF_WIDTH), row)],
        out_shape=[out_sd] * 6 + [jax.ShapeDtypeStruct((t_total, 2 * HALF_WIDTH), BF16)],
        compiler_params=_cparams(("parallel",)),
    )(x2, scale1, shift1, norm_w, w_in, cos_t, sin_t, qnw, knw, bd)


def _kv_update(state_ref, p, k, v, kdec, cdec, bdmask):
    kd = (k.astype(F32) * kdec).astype(BF16)
    kv = lax.dot_general(kd, v, (((0,), (0,)), ((), ())), preferred_element_type=F32)
    state_ref[p] = state_ref[p] * cdec + kv * bdmask


def _ret_state_kernel(k_ref, v_ref, kdec_ref, cdec_ref, bdmask_ref, sb_ref, state_ref):
    @pl.when(pl.program_id(1) == 0)
    def _():
        state_ref[...] = jnp.zeros_like(state_ref)

    sb_ref[...] = state_ref[...]
    for p in range(N_PAIRS):
        sl = slice(p * LANES, (p + 1) * LANES)
        _kv_update(state_ref, p, k_ref[:, sl], v_ref[:, sl], kdec_ref[:, sl], cdec_ref[:, sl], bdmask_ref[...])


def _ret_main_kernel(q_ref, k_ref, v_ref, g_ref, dmat_ref, qdf_ref, qdb_ref, kdf_ref, cdf_ref, bdmask_ref,
                     nw_ref, sb_ref, o_ref, state_ref):
    @pl.when(pl.program_id(1) == 0)
    def _():
        state_ref[...] = jnp.zeros_like(state_ref)

    c = q_ref.shape[0]
    lane = lax.broadcasted_iota(jnp.int32, (c, LANES), 1)
    lo = lane < HEAD_DIM
    for p in range(N_PAIRS):
        sl = slice(p * LANES, (p + 1) * LANES)
        q = q_ref[:, sl]
        k = k_ref[:, sl]
        v = v_ref[:, sl]
        qf = q.astype(F32)
        acc = jnp.dot((qf * qdf_ref[:, sl]).astype(BF16), state_ref[p].astype(BF16), preferred_element_type=F32)
        acc += jnp.dot((qf * qdb_ref[:, sl]).astype(BF16), sb_ref[p].astype(BF16), preferred_element_type=F32)
        for hh in range(2):
            sel = lo if hh == 0 else jnp.logical_not(lo)
            qm = jnp.where(sel, q, jnp.zeros_like(q))
            vm = jnp.where(sel, v, jnp.zeros_like(v))
            s = lax.dot_general(qm, k, (((1,), (1,)), ((), ())), preferred_element_type=F32)
            w = (s * dmat_ref[2 * p + hh]).astype(BF16)
            acc += jnp.dot(w, vm, preferred_element_type=F32)
        _kv_update(state_ref, p, k, v, kdf_ref[:, sl], cdf_ref[:, sl], bdmask_ref[...])
        sq = acc * acc
        ms_lo = jnp.sum(jnp.where(lo, sq, 0.0), axis=-1, keepdims=True)
        ms_hi = jnp.sum(jnp.where(lo, 0.0, sq), axis=-1, keepdims=True)
        ms = jnp.where(lo, ms_lo, ms_hi) * (1.0 / HEAD_DIM)
        y = acc * lax.rsqrt(ms + EPS) * nw_ref[:, sl]
        gf = g_ref[:, sl].astype(F32)
        o_ref[:, sl] = (gf * jax.nn.sigmoid(gf) * y).astype(BF16)


def _retention(rq, rk, rv, rg, tabs, ret_nw, batch, seq_len, c):
    t_total = rq.shape[0]
    nc = seq_len // c
    dmat, qdf, qdb, kdf, kdb, cdf, cdb, bdmask = tabs
    fixed2 = lambda b, i: (0, 0)
    rev = lambda b, i: (b * nc + nc - 1 - i, 0)
    fwd = lambda b, i: (b * nc + i, 0)
    tile = pl.BlockSpec((c, HALF_WIDTH), fwd)
    tile_rev = pl.BlockSpec((c, HALF_WIDTH), rev)
    tab = pl.BlockSpec((c, HALF_WIDTH), fixed2)
    vec = pl.BlockSpec((1, HALF_WIDTH), fixed2)
    mask = pl.BlockSpec((LANES, LANES), fixed2)
    state = pltpu.VMEM((N_PAIRS, LANES, LANES), F32)
    sb = pl.pallas_call(
        _ret_state_kernel,
        grid=(batch, nc),
        in_specs=[tile_rev, tile_rev, tab, vec, mask],
        out_specs=pl.BlockSpec((None, None, N_PAIRS, LANES, LANES), lambda b, i: (b, nc - 1 - i, 0, 0, 0)),
        out_shape=jax.ShapeDtypeStruct((batch, nc, N_PAIRS, LANES, LANES), F32),
        scratch_shapes=[state],
        compiler_params=_cparams(("parallel", "arbitrary")),
    )(rk, rv, kdb, cdb, bdmask)
    return pl.pallas_call(
        _ret_main_kernel,
        grid=(batch, nc),
        in_specs=[tile, tile, tile, tile,
                  pl.BlockSpec((RET_HEADS, c, c), lambda b, i: (0, 0, 0)),
                  tab, tab, tab, vec, mask, vec,
                  pl.BlockSpec((None, None, N_PAIRS, LANES, LANES), lambda b, i: (b, i, 0, 0, 0))],
        out_specs=tile,
        out_shape=jax.ShapeDtypeStruct((t_total, HALF_WIDTH), BF16),
        scratch_shapes=[state],
        compiler_params=_cparams(("parallel", "arbitrary")),
    )(rq, rk, rv, rg, dmat, qdf, qdb, kdf, cdf, bdmask, ret_nw, sb)


def _retention_tables(a_fwd, a_bwd, c):
    lg_f = jnp.log1p(-jnp.exp(a_fwd.astype(F32)))
    lg_b = jnp.log1p(-jnp.exp(a_bwd.astype(F32)))
    pos = jnp.arange(c, dtype=F32)
    diff = pos[:, None] - pos[None, :]
    dmat = jnp.where(diff[None] >= 0,
                     jnp.exp(jnp.maximum(diff, 0.0)[None] * lg_f[:, None, None]),
                     jnp.exp(jnp.maximum(-diff, 0.0)[None] * lg_b[:, None, None]))
    lane_f = jnp.repeat(lg_f, HEAD_DIM)[None, :]
    lane_b = jnp.repeat(lg_b, HEAD_DIM)[None, :]
    qdf = jnp.exp((pos + 1.0)[:, None] * lane_f)
    kdf = jnp.exp((c - 1.0 - pos)[:, None] * lane_f)
    qdb = jnp.exp((c - pos)[:, None] * lane_b)
    kdb = jnp.exp(pos[:, None] * lane_b)
    cdf = jnp.exp(c * lane_f)
    cdb = jnp.exp(c * lane_b)
    r = jnp.arange(LANES)
    bdmask = ((r[:, None] // HEAD_DIM) == (r[None, :] // HEAD_DIM)).astype(F32)
    return dmat, qdf, qdb, kdf, kdb, cdf, cdb, bdmask


def _attn_kernel(lam_ref, q_ref, k_ref, v_ref, bias_ref, sw_ref, o_ref, qq_ref, m_ref, acc_ref):
    i = pl.program_id(2)
    t = q_ref.shape[0]
    nk = k_ref.shape[0] // t
    q = q_ref[...]
    lane = lax.broadcasted_iota(jnp.int32, q.shape, 1)
    zero = jnp.zeros_like(q)
    qq_ref[...] = jnp.concatenate([jnp.where(lane < HEAD_DIM, q, zero), jnp.where(lane < HEAD_DIM, zero, q)], axis=0)
    m_ref[...] = jnp.full_like(m_ref, -jnp.inf)
    acc_ref[...] = jnp.zeros_like(acc_ref)

    def tile(j):
        keys = pl.ds(pl.multiple_of(j * t, t), t)
        which = jnp.clip(j - i, -2, 2) + 2
        for r in range(0, 2 * t, ATTN_ROWS):
            rows = slice(r, r + ATTN_ROWS)
            s = lax.dot_general(qq_ref[rows, :], k_ref[keys, :], (((1,), (1,)), ((), ())),
                                preferred_element_type=F32)
            s = s + bias_ref[which, r % t:r % t + ATTN_ROWS, :]
            m_prev = m_ref[rows, :]
            m_new = jnp.maximum(m_prev, jnp.max(s, axis=1, keepdims=True))
            alpha = jnp.exp2(m_prev - m_new)
            p = jnp.exp2(s - jnp.concatenate([m_new] * (t // LANES), axis=1)).astype(BF16)
            pv = jnp.dot(p, v_ref[keys, :], preferred_element_type=F32)
            acc_ref[rows, :] = jnp.concatenate([alpha, alpha], axis=1) * acc_ref[rows, :] + pv
            m_ref[rows, :] = m_new

    def step(jj, carry):
        for u in range(ATTN_UNROLL):
            tile(jj * ATTN_UNROLL + u)
        return carry

    lax.fori_loop(0, nk // ATTN_UNROLL, step, 0)

    acc = acc_ref[...]
    o = acc[:, :LANES] / acc[:, LANES:]
    att = o[:t] - lam_ref[0] * o[t:]
    msq = jnp.mean(att * att, axis=-1, keepdims=True)
    o_ref[...] = (att * lax.rsqrt(msq + EPS) * sw_ref[...]).astype(BF16)


def _attention(dq, dk, dv, bias_tiles, lam, subln_w, batch, seq_len, t):
    t_total = dq.shape[0]
    nq = seq_len // t
    assert nq % ATTN_UNROLL == 0 and t % ATTN_ROWS == 0
    grid_spec = pltpu.PrefetchScalarGridSpec(
        num_scalar_prefetch=1,
        grid=(batch, DIFF_HEADS, nq),
        in_specs=[pl.BlockSpec((t, LANES), lambda b, h, i, s: (b * nq + i, h)),
                  pl.BlockSpec((seq_len, LANES), lambda b, h, i, s: (b, h)),
                  pl.BlockSpec((seq_len, 2 * LANES), lambda b, h, i, s: (b, h)),
                  pl.BlockSpec((None, 5, t, t), lambda b, h, i, s: (h, 0, 0, 0)),
                  pl.BlockSpec((1, LANES), lambda b, h, i, s: (0, 0))],
        out_specs=pl.BlockSpec((t, LANES), lambda b, h, i, s: (b * nq + i, h)),
        scratch_shapes=[pltpu.VMEM((2 * t, LANES), BF16), pltpu.VMEM((2 * t, LANES), F32),
                        pltpu.VMEM((2 * t, 2 * LANES), F32)],
    )
    return pl.pallas_call(
        _attn_kernel,
        grid_spec=grid_spec,
        out_shape=jax.ShapeDtypeStruct((t_total, HALF_WIDTH), BF16),
        compiler_params=_cparams(("parallel", "parallel", "parallel")),
    )(lam, dq, dk, dv, bias_tiles, subln_w)


def _t5_bucket(rel):
    nb = NUM_BUCKETS // 2
    max_exact = nb // 2
    n = jnp.abs(rel)
    base = jnp.where(rel > 0, nb, 0)
    nf = jnp.maximum(n, 1).astype(F32)
    large = max_exact + (jnp.log(nf / max_exact) / math.log(MAX_DISTANCE / max_exact)
                         * (nb - max_exact)).astype(jnp.int32)
    large = jnp.minimum(large, nb - 1)
    return base + jnp.where(n < max_exact, n, large)


def _bias_tables(rel_bias, t):
    assert t >= MAX_DISTANCE
    table = rel_bias.astype(F32) * LOG2E
    qq = jnp.arange(t, dtype=jnp.int32)[:, None]
    kk = jnp.arange(t, dtype=jnp.int32)[None, :]
    bucket = jnp.stack([_t5_bucket(d * t + kk - qq) for d in (-2, -1, 0, 1, 2)], axis=0)
    onehot = (bucket[..., None] == jnp.arange(NUM_BUCKETS, dtype=jnp.int32)).astype(F32)
    return jnp.einsum('dqkn,nh->hdqk', onehot, table, precision=HIGHEST)


def _store_token_tiles(ref, value):
    n = value.shape[0]
    for s in range(SUBLANES):
        ref[pl.ds(s, n, stride=SUBLANES), :] = value[:, s * LANES:(s + 1) * LANES]


def _load_token_tiles(ref):
    n = ref.shape[0] // SUBLANES
    return jnp.concatenate([ref[pl.ds(s, n, stride=SUBLANES), :] for s in range(SUBLANES)], axis=1)


def _outproj_kernel(ret_ref, dif_ref, x_ref, g1_ref, sc2_ref, sh2_ref, nw2_ref, wo_ref, rw_ref, rb_ref, tri_ref,
                    x1_ref, h2_ref, idx_ref, gate_ref, count_ref, cnt_ref):
    mix = jnp.dot(ret_ref[...], wo_ref[:HALF_WIDTH, :], preferred_element_type=F32)
    mix += jnp.dot(dif_ref[...], wo_ref[HALF_WIDTH:, :], preferred_element_type=F32)
    x1 = x_ref[...] + g1_ref[...] * mix
    x1_ref[...] = x1
    ms = jnp.mean(x1 * x1, axis=-1, keepdims=True)
    h2 = x1 * lax.rsqrt(ms + EPS) * nw2_ref[...]
    h2 = h2 * (1.0 + sc2_ref[...]) + sh2_ref[...]
    _store_token_tiles(h2_ref, h2)
    h2_hi = h2.astype(BF16)
    h2_lo = (h2 - h2_hi.astype(F32)).astype(BF16)
    logits = (jnp.dot(h2_hi, rw_ref[0], preferred_element_type=F32)
              + jnp.dot(h2_lo, rw_ref[0], preferred_element_type=F32)
              + jnp.dot(h2_hi, rw_ref[1], preferred_element_type=F32)) + rb_ref[...]
    lane = lax.broadcasted_iota(jnp.int32, logits.shape, 1)
    vals, idxs = [], []
    for _ in range(TOP_K):
        m = jnp.max(logits, axis=-1, keepdims=True)
        am = jnp.min(jnp.where(logits == m, lane, LANES), axis=-1, keepdims=True)
        vals.append(m)
        idxs.append(am)
        logits = jnp.where(lane == am, -jnp.inf, logits)
    es = [jnp.exp(v - vals[0]) for v in vals]
    den = es[0] + es[1] + es[2] + es[3]
    @pl.when(pl.program_id(0) == 0)
    def _():
        cnt_ref[...] = jnp.zeros_like(cnt_ref)

    member = jnp.zeros(lane.shape, F32)
    for kk in range(TOP_K):
        member = jnp.where(lane == idxs[kk], 1.0, member)
    before = jnp.dot(tri_ref[...], member.astype(BF16), preferred_element_type=F32) + cnt_ref[...]
    cnt_ref[...] += jnp.sum(member, axis=0, keepdims=True)
    count_ref[...] = jnp.broadcast_to(cnt_ref[...], count_ref.shape).astype(jnp.int32)
    idx_out = jnp.zeros(lane.shape, jnp.int32)
    gate_out = jnp.zeros(lane.shape, F32)
    for kk in range(TOP_K):
        rank = jnp.sum(jnp.where(lane == idxs[kk], before, 0.0), axis=-1, keepdims=True).astype(jnp.int32)
        idx_out = jnp.where(lane == kk, idxs[kk], idx_out)
        idx_out = jnp.where(lane == TOP_K + kk, rank, idx_out)
        gate_out = jnp.where(lane == kk, es[kk] / den, gate_out)
    idx_ref[...] = idx_out
    gate_ref[...] = gate_out


def _outproj(ret_out, diff_out, x2, gate1, scale2, shift2, norm2_w, w_out, rw_pad, rb_pad, seq_len, tt):
    t_total, d = x2.shape
    nps = seq_len // tt
    row = lambda i: (i, 0)
    mod = lambda i: (i // nps, 0, 0)
    fixed = lambda i: (0, 0)
    pos = jnp.arange(tt)
    tri = (pos[None, :] < pos[:, None]).astype(BF16)
    return pl.pallas_call(
        _outproj_kernel,
        grid=(t_total // tt,),
        in_specs=[pl.BlockSpec((tt, HALF_WIDTH), row),
                  pl.BlockSpec((tt, HALF_WIDTH), row),
                  pl.BlockSpec((tt, d), row),
                  pl.BlockSpec((None, 1, d), mod),
                  pl.BlockSpec((None, 1, d), mod),
                  pl.BlockSpec((None, 1, d), mod),
                  pl.BlockSpec((1, d), fixed),
                  pl.BlockSpec((2 * HALF_WIDTH, d), fixed),
                  pl.BlockSpec((2, d, LANES), lambda i: (0, 0, 0)),
                  pl.BlockSpec((1, LANES), fixed),
                  pl.BlockSpec((tt, tt), fixed)],
        out_specs=[pl.BlockSpec((tt, d), row), pl.BlockSpec((tt * SUBLANES, LANES), row),
                   pl.BlockSpec((tt, LANES), row), pl.BlockSpec((tt, LANES), row),
                   pl.BlockSpec((SUBLANES, LANES), fixed)],
        out_shape=[jax.ShapeDtypeStruct((t_total, d), F32), jax.ShapeDtypeStruct((t_total * SUBLANES, LANES), F32),
                   jax.ShapeDtypeStruct((t_total, LANES), jnp.int32),
                   jax.ShapeDtypeStruct((t_total, LANES), F32),
                   jax.ShapeDtypeStruct((SUBLANES, LANES), jnp.int32)],
        scratch_shapes=[pltpu.VMEM((1, LANES), F32)],
        compiler_params=_cparams(("arbitrary",)),
    )(ret_out, diff_out, x2, gate1, scale2, shift2, norm2_w, w_out, rw_pad, rb_pad, tri)


def _dispatch_kernel(fill_ref, dest_ref, h2_ref, xs_hbm, zero_ref, sem):
    i = pl.program_id(0)
    tokens = h2_ref.shape[0] // SUBLANES
    for t in range(tokens):
        for kk in range(TOP_K):
            pltpu.async_copy(h2_ref.at[pl.ds(t * SUBLANES, SUBLANES)],
                             xs_hbm.at[pl.ds(pl.multiple_of(dest_ref[t * TOP_K + kk], SUBLANES), SUBLANES)],
                             sem, priority=kk % DMA_QUEUES)
    for kk in range(TOP_K):
        pltpu.make_async_copy(h2_ref, xs_hbm.at[pl.ds(0, tokens * SUBLANES)], sem).wait()

    @pl.when(i == pl.num_programs(0) - 1)
    def _():
        zero_ref[...] = jnp.zeros_like(zero_ref)

        def fill(n, carry):
            pltpu.make_async_copy(zero_ref, xs_hbm.at[pl.ds(pl.multiple_of(fill_ref[n], SUBLANES), SUBLANES)],
                                  sem).start()
            return carry

        lax.fori_loop(0, fill_ref.shape[0], fill, 0)
        for _ in range(fill_ref.shape[0] // tokens):
            pltpu.make_async_copy(h2_ref, xs_hbm.at[pl.ds(0, tokens * SUBLANES)], sem).wait()


def _dispatch(h2, dest, fill_rows, n_rows, tt):
    t_total = h2.shape[0] // SUBLANES
    assert fill_rows.shape[0] % tt == 0
    grid_spec = pltpu.PrefetchScalarGridSpec(
        num_scalar_prefetch=1,
        grid=(t_total // tt,),
        in_specs=[pl.BlockSpec((tt * TOP_K,), lambda i, fill: (i,), memory_space=pltpu.SMEM),
                  pl.BlockSpec((tt * SUBLANES, LANES), lambda i, fill: (i, 0))],
        out_specs=pl.BlockSpec(memory_space=pl.ANY),
        scratch_shapes=[pltpu.VMEM((SUBLANES, LANES), F32), pltpu.SemaphoreType.DMA],
    )
    return pl.pallas_call(
        _dispatch_kernel,
        grid_spec=grid_spec,
        out_shape=jax.ShapeDtypeStruct(((n_rows + fill_rows.shape[0]) * SUBLANES, LANES), F32),
        compiler_params=_cparams(("arbitrary",)),
    )(fill_rows, dest, h2)


def _expert_kernel(be_ref, act_ref, dst_prev_ref, x_ref, wgu_ref, bgu_ref, wd_ref, bd_ref, out_hbm, y0, y1, sem_out):
    i = pl.program_id(0)
    rows = y0.shape[0] // SUBLANES
    d_ff = wd_ref.shape[1]
    prev_active = jnp.logical_and(i > 0, act_ref[jnp.maximum(i - 1, 0)] == 1)

    @pl.when(i == 0)
    def _():
        y1[...] = jnp.zeros_like(y1)

    def body(ycur, yprev):
        @pl.when(prev_active)
        def _():
            pltpu.make_async_copy(ycur, out_hbm.at[pl.ds(0, rows * SUBLANES)], sem_out).wait()

        @pl.when(act_ref[i] == 1)
        def _():
            for r in range(rows):
                pltpu.async_copy(yprev.at[pl.ds(r * SUBLANES, SUBLANES)],
                                 out_hbm.at[pl.ds(pl.multiple_of(dst_prev_ref[r], SUBLANES), SUBLANES)],
                                 sem_out, priority=r % DMA_QUEUES)
            x = _load_token_tiles(x_ref).astype(BF16)
            gu = jnp.dot(x, wgu_ref[0], preferred_element_type=F32) + bgu_ref[0]
            glu = jnp.minimum(gu[:, :d_ff], SWIGLU_LIMIT)
            lin = jnp.clip(gu[:, d_ff:], -SWIGLU_LIMIT, SWIGLU_LIMIT)
            hidden = glu * jax.nn.sigmoid(SWIGLU_ALPHA * glu) * (lin + 1.0)
            _store_token_tiles(ycur, jnp.dot(hidden.astype(BF16), wd_ref[0], preferred_element_type=F32) + bd_ref[0])

    @pl.when(i % 2 == 0)
    def _():
        body(y0, y1)

    @pl.when(i % 2 == 1)
    def _():
        body(y1, y0)


def _experts(xs, block_expert, active, dst_prev, wgu, bgu, wd, bd, n_slots, rows):
    n_blocks = block_expert.shape[0]
    d, d_ff2 = wgu.shape[1:]
    assert d == SUBLANES * LANES
    buf = pltpu.VMEM((rows * SUBLANES, LANES), F32)
    grid_spec = pltpu.PrefetchScalarGridSpec(
        num_scalar_prefetch=2,
        grid=(n_blocks,),
        in_specs=[pl.BlockSpec((rows,), lambda i, be, act: (i,), memory_space=pltpu.SMEM),
                  pl.BlockSpec((rows * SUBLANES, LANES), lambda i, be, act: (i, 0)),
                  pl.BlockSpec((1, d, d_ff2), lambda i, be, act: (be[i], 0, 0)),
                  pl.BlockSpec((1, 1, d_ff2), lambda i, be, act: (be[i], 0, 0)),
                  pl.BlockSpec((1, d_ff2 // 2, d), lambda i, be, act: (be[i], 0, 0)),
                  pl.BlockSpec((1, 1, d), lambda i, be, act: (be[i], 0, 0))],
        out_specs=pl.BlockSpec(memory_space=pl.ANY),
        scratch_shapes=[buf, buf, pltpu.SemaphoreType.DMA],
    )
    return pl.pallas_call(
        _expert_kernel,
        grid_spec=grid_spec,
        out_shape=jax.ShapeDtypeStruct(((n_slots + rows) * SUBLANES, LANES), F32),
        compiler_params=_cparams(("arbitrary",)),
    )(block_expert, active, dst_prev, xs, wgu, bgu, wd, bd)


def _routing(top_idx, rank, counts, rows):
    t_total = top_idx.shape[0]
    padded = (counts + rows - 1) // rows * rows
    ends = jnp.cumsum(padded)
    starts = ends - padded
    dest = (starts[top_idx] + rank).reshape(-1)
    n_blocks = (t_total * TOP_K + N_EXPERTS * (rows - 1)) // rows + 2
    n_rows = n_blocks * rows
    n_slots = TOP_K * t_total
    tok = jnp.repeat(jnp.arange(t_total, dtype=jnp.int32), TOP_K)
    slot = jnp.tile(jnp.arange(TOP_K, dtype=jnp.int32), t_total)
    landing = n_slots + jnp.arange(n_rows, dtype=jnp.int32) % rows
    row_dst = landing.at[dest].set(slot * t_total + tok, unique_indices=True)
    dst_prev = jnp.concatenate([landing[:rows], row_dst[:-rows]])
    block_start = jnp.arange(n_blocks, dtype=jnp.int32) * rows
    block_expert = jnp.sum((ends[None, :] <= block_start[:, None]).astype(jnp.int32), axis=1)
    block_expert = jnp.minimum(block_expert, N_EXPERTS - 1)
    has_rows = block_start < ends[-1]
    active = jnp.logical_or(has_rows, jnp.concatenate([has_rows[:1], has_rows[:-1]])).astype(jnp.int32)
    q = jnp.arange(rows, dtype=jnp.int32)
    pad_row = (starts + counts)[:, None] + q[None, :]
    pad_row = jnp.where(pad_row < ends[:, None], pad_row, -1).reshape(-1)
    fill = jnp.concatenate([pad_row, ends[-1] + q])
    fill = jnp.where(fill >= 0, fill, n_rows + jnp.arange(fill.shape[0], dtype=jnp.int32))
    return block_expert, active, dest * SUBLANES, dst_prev * SUBLANES, fill * SUBLANES, n_rows


def _combine_kernel(x1_ref, g2_ref, gate_ref, *refs):
    y_refs, o_ref = refs[:TOP_K], refs[TOP_K]
    gates = gate_ref[...]
    moe = gates[:, 0:1] * _load_token_tiles(y_refs[0])
    for kk in range(1, TOP_K):
        moe += gates[:, kk:kk + 1] * _load_token_tiles(y_refs[kk])
    o_ref[...] = x1_ref[...] + g2_ref[...] * moe


def _combine(x1, gate2, gates, y4, seq_len, tt):
    t_total, d = x1.shape
    nps = seq_len // tt
    n_tiles = t_total // tt
    slot_specs = [pl.BlockSpec((tt * SUBLANES, LANES), functools.partial(lambda kk, i: (kk * n_tiles + i, 0), kk))
                  for kk in range(TOP_K)]
    return pl.pallas_call(
        _combine_kernel,
        grid=(n_tiles,),
        in_specs=[pl.BlockSpec((tt, d), lambda i: (i, 0)),
                  pl.BlockSpec((None, 1, d), lambda i: (i // nps, 0, 0)),
                  pl.BlockSpec((tt, LANES), lambda i: (i, 0))] + slot_specs,
        out_specs=pl.BlockSpec((tt, d), lambda i: (i, 0)),
        out_shape=jax.ShapeDtypeStruct((t_total, d), F32),
        compiler_params=_cparams(("parallel",)),
    )(x1, gate2, gates, *([y4] * TOP_K))


def _tiles(seq_len):
    token_tile = min(512, seq_len)
    ret_chunk = min(256, seq_len)
    attn_tile = min(512, seq_len)
    expert_rows = 256
    combine_tile = min(256, seq_len)
    return token_tile, ret_chunk, attn_tile, expert_rows, combine_tile


def _trunk(x, mod, lam_init, norm1_w, w_in_b, ret_a_fwd, ret_a_bwd, ret_norm_w, diff_q_norm_w, diff_k_norm_w,
           lam_q1, lam_k1, lam_q2, lam_k2, diff_subln_w, rel_bias, w_out_b, norm2_w, rw_pad, rb_pad,
           wgu_b, b_gate_up, wd_b, b_down):
    batch, seq_len, d = x.shape
    tt, c, t, rows, ct = _tiles(seq_len)
    x2 = x.reshape(batch * seq_len, d)
    shift1, scale1, gate1, shift2, scale2, gate2 = [mod[:, n][:, None, :] for n in range(6)]

    pos = jnp.arange(seq_len, dtype=F32)
    inv_freq = ROPE_BASE ** (-jnp.arange(0, HEAD_DIM, 2, dtype=F32) / HEAD_DIM)
    ang = pos[:, None] * inv_freq[None, :]
    cos_t = jnp.tile(jnp.cos(ang), (1, LANES // (HEAD_DIM // 2)))
    sin_h = jnp.sin(ang)
    sin_t = jnp.tile(jnp.concatenate([-sin_h, sin_h], axis=1), (1, LANES // HEAD_DIM))
    qnw = jnp.tile(diff_q_norm_w.astype(F32), HALF_WIDTH // HEAD_DIM)[None, :]
    knw = jnp.tile(diff_k_norm_w.astype(F32), HALF_WIDTH // HEAD_DIM)[None, :]
    r = jnp.arange(HALF_WIDTH)
    bd = jnp.where((r[:, None] // HEAD_DIM) == (r[None, :] // HEAD_DIM), 1.0 / HEAD_DIM, 0.0).astype(BF16)

    rq, rk, rv, rg, dq, dk, dv = _inproj(x2, scale1, shift1, norm1_w[None, :], w_in_b, cos_t, sin_t, qnw, knw, bd,
                                         seq_len, tt)

    tabs = _retention_tables(ret_a_fwd, ret_a_bwd, c)
    ret_out = _retention(rq, rk, rv, rg, tabs, ret_norm_w.reshape(1, HALF_WIDTH).astype(F32), batch, seq_len, c)

    lam = (jnp.exp(jnp.sum(lam_q1.astype(F32) * lam_k1.astype(F32)))
           - jnp.exp(jnp.sum(lam_q2.astype(F32) * lam_k2.astype(F32))) + lam_init)
    subln = (diff_subln_w.astype(F32) * (1.0 - lam_init))[None, :]
    diff_out = _attention(dq, dk, dv, _bias_tables(rel_bias, t), lam[None].astype(F32), subln, batch, seq_len, t)

    x1, h2, idx_pad, gates_pad, counts = _outproj(ret_out, diff_out, x2, gate1, scale2, shift2, norm2_w[None, :],
                                                  w_out_b, rw_pad, rb_pad, seq_len, tt)
    block_expert, active, dest, dst_prev, fill, n_rows = _routing(idx_pad[:, :TOP_K], idx_pad[:, TOP_K:2 * TOP_K],
                                                                  counts[0, :N_EXPERTS], rows)
    xs = _dispatch(h2, dest, fill, n_rows, ct)
    y4 = _experts(xs, block_expert, active, dst_prev, wgu_b, b_gate_up, wd_b, b_down, TOP_K * batch * seq_len, rows)
    y = _combine(x1, gate2, gates_pad, y4, seq_len, ct)
    return y.reshape(batch, seq_len, d)


def kernel(x_prompt, x_sample, c_prompt, c_sample, w_ada, b_ada, norm1_w, w_in, ret_a_fwd, ret_a_bwd, ret_norm_w,
           diff_q_norm_w, diff_k_norm_w, lam_q1, lam_k1, lam_q2, lam_k2, diff_subln_w, rel_bias, w_out, norm2_w,
           router_w, router_b, w_gate_up, b_gate_up, w_down, b_down):
    depth = w_ada.shape[0]
    d = x_prompt.shape[-1]
    n_prompt = c_prompt.shape[0]
    n_cond = n_prompt + c_sample.shape[0]
    c_all = jnp.concatenate([c_prompt, c_sample], axis=0).astype(F32)
    c_pad = jnp.pad(c_all, ((0, -n_cond % 8), (0, 0)))
    xs = [x_prompt, x_sample]
    for l in range(depth):
        lam_init = 0.8 - 0.6 * math.exp(-0.3 * l)
        mod = _ada(c_pad, w_ada[l], b_ada[l])[:n_cond].reshape(n_cond, 6, d)
        rw = jnp.pad(router_w[l].astype(F32), ((0, 0), (0, LANES - N_EXPERTS)))
        rw_hi = rw.astype(BF16)
        rw_pad = jnp.stack([rw_hi, (rw - rw_hi.astype(F32)).astype(BF16)])
        rb_pad = jnp.pad(router_b[l].astype(F32), (0, LANES - N_EXPERTS), constant_values=-jnp.inf)[None, :]
        shared = (norm1_w[l], w_in[l].astype(BF16), ret_a_fwd[l], ret_a_bwd[l], ret_norm_w[l], diff_q_norm_w[l],
                  diff_k_norm_w[l], lam_q1[l], lam_k1[l], lam_q2[l], lam_k2[l], diff_subln_w[l], rel_bias,
                  w_out[l].astype(BF16), norm2_w[l], rw_pad, rb_pad,
                  w_gate_up[l].astype(BF16), b_gate_up[l][:, None, :], w_down[l].astype(BF16), b_down[l][:, None, :])
        xs = [_trunk(xs[0], mod[:n_prompt], lam_init, *shared),
              _trunk(xs[1], mod[n_prompt:], lam_init, *shared)]
    return (xs[0], xs[1])
```

```python
import functools
import math

import jax
import jax.numpy as jnp
import numpy as np
from jax import lax
from jax.experimental import pallas as pl
from jax.experimental.pallas import tpu as pltpu

F32 = jnp.float32
BF16 = jnp.bfloat16
HIGHEST = lax.Precision.HIGHEST

LANES = 128
SUBLANES = 8
HEAD_DIM = 64
RET_HEADS = 8
DIFF_HEADS = 4
HALF_WIDTH = 512
N_PAIRS = HALF_WIDTH // LANES
ROPE_BASE = 10000.0
NUM_BUCKETS = 32
MAX_DISTANCE = 128
N_EXPERTS = 32
TOP_K = 4
SWIGLU_LIMIT = 7.0
SWIGLU_ALPHA = 1.702
EPS = 1e-6
LOG2E = math.log2(math.e)
ATTN_ROWS = 128
ATTN_UNROLL = 8
VMEM_LIMIT = 56 * 1024 * 1024
DMA_QUEUES = 2


def _cparams(sem):
    return pltpu.CompilerParams(dimension_semantics=sem, vmem_limit_bytes=VMEM_LIMIT)


def _ada_kernel(c_ref, w_ref, b_ref, o_ref):
    c = c_ref[...]
    a = c * jax.nn.sigmoid(c)
    o_ref[...] = jnp.dot(a, w_ref[...], preferred_element_type=F32, precision=HIGHEST) + b_ref[...]


def _ada(c_pad, w_ada, b_ada):
    rows, d = c_pad.shape
    n = w_ada.shape[1]
    tn = d
    return pl.pallas_call(
        _ada_kernel,
        grid=(n // tn,),
        in_specs=[pl.BlockSpec((rows, d), lambda j: (0, 0)),
                  pl.BlockSpec((d, tn), lambda j: (0, j)),
                  pl.BlockSpec((1, tn), lambda j: (0, j))],
        out_specs=pl.BlockSpec((rows, tn), lambda j: (0, j)),
        out_shape=jax.ShapeDtypeStruct((rows, n), F32),
        compiler_params=_cparams(("parallel",)),
    )(c_pad, w_ada, b_ada.reshape(1, n))


def _rotate_half(xg):
    lane = lax.broadcasted_iota(jnp.int32, xg.shape, 1)
    first = (lane % HEAD_DIM) < (HEAD_DIM // 2)
    return jnp.where(first, pltpu.roll(xg, LANES - HEAD_DIM // 2, 1), pltpu.roll(xg, HEAD_DIM // 2, 1))


def _inproj_kernel(x_ref, sc_ref, sh_ref, nw_ref, w_ref, cos_ref, sin_ref, qnw_ref, knw_ref, bd_ref,
                   rq_ref, rk_ref, rv_ref, rg_ref, dq_ref, dk_ref, dv_ref):
    x = x_ref[...]
    ms = jnp.mean(x * x, axis=-1, keepdims=True)
    h = x * lax.rsqrt(ms + EPS) * nw_ref[...]
    h = (h * (1.0 + sc_ref[...]) + sh_ref[...]).astype(BF16)

    def piece(n):
        return jnp.dot(h, w_ref[:, n * HALF_WIDTH:(n + 1) * HALF_WIDTH], preferred_element_type=F32)

    cos = cos_ref[...]
    sin = sin_ref[...]

    def rotary(p, out_ref, scale):
        for g in range(N_PAIRS):
            xg = p[:, g * LANES:(g + 1) * LANES]
            y = xg * cos + _rotate_half(xg) * sin
            out_ref[:, g * LANES:(g + 1) * LANES] = (y * scale).astype(BF16)

    def head_norm(p, w, out_ref, scale):
        msq = jnp.dot((p * p).astype(BF16), bd_ref[...], preferred_element_type=F32)
        out_ref[...] = (p * lax.rsqrt(msq + EPS) * w * scale).astype(BF16)

    rotary(piece(0), rq_ref, 1.0)
    rotary(piece(1), rk_ref, HEAD_DIM ** -0.5)
    rv_ref[...] = piece(2).astype(BF16)
    rg_ref[...] = piece(3).astype(BF16)
    head_norm(piece(4), qnw_ref[...], dq_ref, HEAD_DIM ** -0.5 * LOG2E)
    head_norm(piece(5), knw_ref[...], dk_ref, 1.0)
    dv = piece(6).astype(BF16)
    ones = jnp.ones((dv.shape[0], LANES), BF16)
    for hd in range(DIFF_HEADS):
        dv_ref[:, 2 * hd * LANES:(2 * hd + 1) * LANES] = dv[:, hd * LANES:(hd + 1) * LANES]
        dv_ref[:, (2 * hd + 1) * LANES:(2 * hd + 2) * LANES] = ones


def _inproj(x2, scale1, shift1, norm_w, w_in, cos_t, sin_t, qnw, knw, bd, seq_len, tt):
    t_total, d = x2.shape
    nps = seq_len // tt
    n_in = w_in.shape[1]
    row = lambda i: (i, 0)
    mod = lambda i: (i // nps, 0, 0)
    pos = lambda i: (i % nps, 0)
    fixed = lambda i: (0, 0)
    out_sd = jax.ShapeDtypeStruct((t_total, HALF_WIDTH), BF16)
    return pl.pallas_call(
        _inproj_kernel,
        grid=(t_total // tt,),
        in_specs=[pl.BlockSpec((tt, d), row),
                  pl.BlockSpec((None, 1, d), mod),
                  pl.BlockSpec((None, 1, d), mod),
                  pl.BlockSpec((1, d), fixed),
                  pl.BlockSpec((d, n_in), fixed),
                  pl.BlockSpec((tt, LANES), pos),
                  pl.BlockSpec((tt, LANES), pos),
                  pl.BlockSpec((1, HALF_WIDTH), fixed),
                  pl.BlockSpec((1, HALF_WIDTH), fixed),
                  pl.BlockSpec((HALF_WIDTH, HALF_WIDTH), fixed)],
        out_specs=[pl.BlockSpec((tt, HALF_WIDTH), row)] * 6 + [pl.BlockSpec((tt, 2 * HALF_WIDTH), row)],
        out_shape=[out_sd] * 6 + [jax.ShapeDtypeStruct((t_total, 2 * HALF_WIDTH), BF16)],
        compiler_params=_cparams(("parallel",)),
    )(x2, scale1, shift1, norm_w, w_in, cos_t, sin_t, qnw, knw, bd)


def _kv_update(state_ref, p, k, v, kdec, cdec, bdmask):
    kd = (k.astype(F32) * kdec).astype(BF16)
    kv = lax.dot_general(kd, v, (((0,), (0,)), ((), ())), preferred_element_type=F32)
    state_ref[p] = state_ref[p] * cdec + kv * bdmask


def _ret_state_kernel(k_ref, v_ref, kdec_ref, cdec_ref, bdmask_ref, sb_ref, state_ref):
    @pl.when(pl.program_id(1) == 0)
    def _():
        state_ref[...] = jnp.zeros_like(state_ref)

    sb_ref[...] = state_ref[...]
    for p in range(N_PAIRS):
        sl = slice(p * LANES, (p + 1) * LANES)
        _kv_update(state_ref, p, k_ref[:, sl], v_ref[:, sl], kdec_ref[:, sl], cdec_ref[:, sl], bdmask_ref[...])


def _ret_main_kernel(q_ref, k_ref, v_ref, g_ref, dmat_ref, qdf_ref, qdb_ref, kdf_ref, cdf_ref, bdmask_ref,
                     nw_ref, sb_ref, o_ref, state_ref):
    @pl.when(pl.program_id(1) == 0)
    def _():
        state_ref[...] = jnp.zeros_like(state_ref)

    c = q_ref.shape[0]
    lane = lax.broadcasted_iota(jnp.int32, (c, LANES), 1)
    lo = lane < HEAD_DIM
    for p in range(N_PAIRS):
        sl = slice(p * LANES, (p + 1) * LANES)
        q = q_ref[:, sl]
        k = k_ref[:, sl]
        v = v_ref[:, sl]
        qf = q.astype(F32)
        acc = jnp.dot((qf * qdf_ref[:, sl]).astype(BF16), state_ref[p].astype(BF16), preferred_element_type=F32)
        acc += jnp.dot((qf * qdb_ref[:, sl]).astype(BF16), sb_ref[p].astype(BF16), preferred_element_type=F32)
        for hh in range(2):
            sel = lo if hh == 0 else jnp.logical_not(lo)
            qm = jnp.where(sel, q, jnp.zeros_like(q))
            vm = jnp.where(sel, v, jnp.zeros_like(v))
            s = lax.dot_general(qm, k, (((1,), (1,)), ((), ())), preferred_element_type=F32)
            w = (s * dmat_ref[2 * p + hh]).astype(BF16)
            acc += jnp.dot(w, vm, preferred_element_type=F32)
        _kv_update(state_ref, p, k, v, kdf_ref[:, sl], cdf_ref[:, sl], bdmask_ref[...])
        sq = acc * acc
        ms_lo = jnp.sum(jnp.where(lo, sq, 0.0), axis=-1, keepdims=True)
        ms_hi = jnp.sum(jnp.where(lo, 0.0, sq), axis=-1, keepdims=True)
        ms = jnp.where(lo, ms_lo, ms_hi) * (1.0 / HEAD_DIM)
        y = acc * lax.rsqrt(ms + EPS) * nw_ref[:, sl]
        gf = g_ref[:, sl].astype(F32)
        o_ref[:, sl] = (gf * jax.nn.sigmoid(gf) * y).astype(BF16)


def _retention(rq, rk, rv, rg, tabs, ret_nw, batch, seq_len, c):
    t_total = rq.shape[0]
    nc = seq_len // c
    dmat, qdf, qdb, kdf, kdb, cdf, cdb, bdmask = tabs
    fixed2 = lambda b, i: (0, 0)
    rev = lambda b, i: (b * nc + nc - 1 - i, 0)
    fwd = lambda b, i: (b * nc + i, 0)
    tile = pl.BlockSpec((c, HALF_WIDTH), fwd)
    tile_rev = pl.BlockSpec((c, HALF_WIDTH), rev)
    tab = pl.BlockSpec((c, HALF_WIDTH), fixed2)
    vec = pl.BlockSpec((1, HALF_WIDTH), fixed2)
    mask = pl.BlockSpec((LANES, LANES), fixed2)
    state = pltpu.VMEM((N_PAIRS, LANES, LANES), F32)
    sb = pl.pallas_call(
        _ret_state_kernel,
        grid=(batch, nc),
        in_specs=[tile_rev, tile_rev, tab, vec, mask],
        out_specs=pl.BlockSpec((None, None, N_PAIRS, LANES, LANES), lambda b, i: (b, nc - 1 - i, 0, 0, 0)),
        out_shape=jax.ShapeDtypeStruct((batch, nc, N_PAIRS, LANES, LANES), F32),
        scratch_shapes=[state],
        compiler_params=_cparams(("parallel", "arbitrary")),
    )(rk, rv, kdb, cdb, bdmask)
    return pl.pallas_call(
        _ret_main_kernel,
        grid=(batch, nc),
        in_specs=[tile, tile, tile, tile,
                  pl.BlockSpec((RET_HEADS, c, c), lambda b, i: (0, 0, 0)),
                  tab, tab, tab, vec, mask, vec,
                  pl.BlockSpec((None, None, N_PAIRS, LANES, LANES), lambda b, i: (b, i, 0, 0, 0))],
        out_specs=tile,
        out_shape=jax.ShapeDtypeStruct((t_total, HALF_WIDTH), BF16),
        scratch_shapes=[state],
        compiler_params=_cparams(("parallel", "arbitrary")),
    )(rq, rk, rv, rg, dmat, qdf, qdb, kdf, cdf, bdmask, ret_nw, sb)


def _retention_tables(a_fwd, a_bwd, c):
    lg_f = jnp.log1p(-jnp.exp(a_fwd.astype(F32)))
    lg_b = jnp.log1p(-jnp.exp(a_bwd.astype(F32)))
    pos = jnp.arange(c, dtype=F32)
    diff = pos[:, None] - pos[None, :]
    dmat = jnp.where(diff[None] >= 0,
                     jnp.exp(jnp.maximum(diff, 0.0)[None] * lg_f[:, None, None]),
                     jnp.exp(jnp.maximum(-diff, 0.0)[None] * lg_b[:, None, None]))
    lane_f = jnp.repeat(lg_f, HEAD_DIM)[None, :]
    lane_b = jnp.repeat(lg_b, HEAD_DIM)[None, :]
    qdf = jnp.exp((pos + 1.0)[:, None] * lane_f)
    kdf = jnp.exp((c - 1.0 - pos)[:, None] * lane_f)
    qdb = jnp.exp((c - pos)[:, None] * lane_b)
    kdb = jnp.exp(pos[:, None] * lane_b)
    cdf = jnp.exp(c * lane_f)
    cdb = jnp.exp(c * lane_b)
    r = jnp.arange(LANES)
    bdmask = ((r[:, None] // HEAD_DIM) == (r[None, :] // HEAD_DIM)).astype(F32)
    return dmat, qdf, qdb, kdf, kdb, cdf, cdb, bdmask


def _attn_kernel(lam_ref, q_ref, k_ref, v_ref, bias_ref, sw_ref, o_ref, qq_ref, m_ref, acc_ref):
    i = pl.program_id(2)
    t = q_ref.shape[0]
    nk = k_ref.shape[0] // t
    q = q_ref[...]
    lane = lax.broadcasted_iota(jnp.int32, q.shape, 1)
    zero = jnp.zeros_like(q)
    qq_ref[...] = jnp.concatenate([jnp.where(lane < HEAD_DIM, q, zero), jnp.where(lane < HEAD_DIM, zero, q)], axis=0)
    m_ref[...] = jnp.full_like(m_ref, -jnp.inf)
    acc_ref[...] = jnp.zeros_like(acc_ref)

    def tile(j):
        keys = pl.ds(pl.multiple_of(j * t, t), t)
        which = jnp.clip(j - i, -2, 2) + 2
        for r in range(0, 2 * t, ATTN_ROWS):
            rows = slice(r, r + ATTN_ROWS)
            s = lax.dot_general(qq_ref[rows, :], k_ref[keys, :], (((1,), (1,)), ((), ())),
                                preferred_element_type=F32)
            s = s + bias_ref[which, r % t:r % t + ATTN_ROWS, :]
            m_prev = m_ref[rows, :]
            m_new = jnp.maximum(m_prev, jnp.max(s, axis=1, keepdims=True))
            alpha = jnp.exp2(m_prev - m_new)
            p = jnp.exp2(s - jnp.concatenate([m_new] * (t // LANES), axis=1)).astype(BF16)
            pv = jnp.dot(p, v_ref[keys, :], preferred_element_type=F32)
            acc_ref[rows, :] = jnp.concatenate([alpha, alpha], axis=1) * acc_ref[rows, :] + pv
            m_ref[rows, :] = m_new

    def step(jj, carry):
        for u in range(ATTN_UNROLL):
            tile(jj * ATTN_UNROLL + u)
        return carry

    lax.fori_loop(0, nk // ATTN_UNROLL, step, 0)

    acc = acc_ref[...]
    o = acc[:, :LANES] / acc[:, LANES:]
    att = o[:t] - lam_ref[0] * o[t:]
    msq = jnp.mean(att * att, axis=-1, keepdims=True)
    o_ref[...] = (att * lax.rsqrt(msq + EPS) * sw_ref[...]).astype(BF16)


def _attention(dq, dk, dv, bias_tiles, lam, subln_w, batch, seq_len, t):
    t_total = dq.shape[0]
    nq = seq_len // t
    assert nq % ATTN_UNROLL == 0 and t % ATTN_ROWS == 0
    grid_spec = pltpu.PrefetchScalarGridSpec(
        num_scalar_prefetch=1,
        grid=(batch, DIFF_HEADS, nq),
        in_specs=[pl.BlockSpec((t, LANES), lambda b, h, i, s: (b * nq + i, h)),
                  pl.BlockSpec((seq_len, LANES), lambda b, h, i, s: (b, h)),
                  pl.BlockSpec((seq_len, 2 * LANES), lambda b, h, i, s: (b, h)),
                  pl.BlockSpec((None, 5, t, t), lambda b, h, i, s: (h, 0, 0, 0)),
                  pl.BlockSpec((1, LANES), lambda b, h, i, s: (0, 0))],
        out_specs=pl.BlockSpec((t, LANES), lambda b, h, i, s: (b * nq + i, h)),
        scratch_shapes=[pltpu.VMEM((2 * t, LANES), BF16), pltpu.VMEM((2 * t, LANES), F32),
                        pltpu.VMEM((2 * t, 2 * LANES), F32)],
    )
    return pl.pallas_call(
        _attn_kernel,
        grid_spec=grid_spec,
        out_shape=jax.ShapeDtypeStruct((t_total, HALF_WIDTH), BF16),
        compiler_params=_cparams(("parallel", "parallel", "parallel")),
    )(lam, dq, dk, dv, bias_tiles, subln_w)


def _t5_bucket(rel):
    nb = NUM_BUCKETS // 2
    max_exact = nb // 2
    n = jnp.abs(rel)
    base = jnp.where(rel > 0, nb, 0)
    nf = jnp.maximum(n, 1).astype(F32)
    large = max_exact + (jnp.log(nf / max_exact) / math.log(MAX_DISTANCE / max_exact)
                         * (nb - max_exact)).astype(jnp.int32)
    large = jnp.minimum(large, nb - 1)
    return base + jnp.where(n < max_exact, n, large)


def _bias_tables(rel_bias, t):
    assert t >= MAX_DISTANCE
    table = rel_bias.astype(F32) * LOG2E
    qq = jnp.arange(t, dtype=jnp.int32)[:, None]
    kk = jnp.arange(t, dtype=jnp.int32)[None, :]
    bucket = jnp.stack([_t5_bucket(d * t + kk - qq) for d in (-2, -1, 0, 1, 2)], axis=0)
    onehot = (bucket[..., None] == jnp.arange(NUM_BUCKETS, dtype=jnp.int32)).astype(F32)
    return jnp.einsum('dqkn,nh->hdqk', onehot, table, precision=HIGHEST)


def _store_token_tiles(ref, value):
    n = value.shape[0]
    for s in range(SUBLANES):
        ref[pl.ds(s, n, stride=SUBLANES), :] = value[:, s * LANES:(s + 1) * LANES]


def _load_token_tiles(ref):
    n = ref.shape[0] // SUBLANES
    return jnp.concatenate([ref[pl.ds(s, n, stride=SUBLANES), :] for s in range(SUBLANES)], axis=1)


def _outproj_kernel(ret_ref, dif_ref, x_ref, g1_ref, sc2_ref, sh2_ref, nw2_ref, wo_ref, rw_ref, rb_ref, tri_ref,
                    x1_ref, h2_ref, idx_ref, gate_ref, count_ref, cnt_ref):
    mix = jnp.dot(ret_ref[...], wo_ref[:HALF_WIDTH, :], preferred_element_type=F32)
    mix += jnp.dot(dif_ref[...], wo_ref[HALF_WIDTH:, :], preferred_element_type=F32)
    x1 = x_ref[...] + g1_ref[...] * mix
    x1_ref[...] = x1
    ms = jnp.mean(x1 * x1, axis=-1, keepdims=True)
    h2 = x1 * lax.rsqrt(ms + EPS) * nw2_ref[...]
    h2 = h2 * (1.0 + sc2_ref[...]) + sh2_ref[...]
    _store_token_tiles(h2_ref, h2)
    h2_hi = h2.astype(BF16)
    h2_lo = (h2 - h2_hi.astype(F32)).astype(BF16)
    logits = (jnp.dot(h2_hi, rw_ref[0], preferred_element_type=F32)
              + jnp.dot(h2_lo, rw_ref[0], preferred_element_type=F32)
              + jnp.dot(h2_hi, rw_ref[1], preferred_element_type=F32)) + rb_ref[...]
    lane = lax.broadcasted_iota(jnp.int32, logits.shape, 1)
    vals, idxs = [], []
    for _ in range(TOP_K):
        m = jnp.max(logits, axis=-1, keepdims=True)
        am = jnp.min(jnp.where(logits == m, lane, LANES), axis=-1, keepdims=True)
        vals.append(m)
        idxs.append(am)
        logits = jnp.where(lane == am, -jnp.inf, logits)
    es = [jnp.exp(v - vals[0]) for v in vals]
    den = es[0] + es[1] + es[2] + es[3]
    @pl.when(pl.program_id(0) == 0)
    def _():
        cnt_ref[...] = jnp.zeros_like(cnt_ref)

    member = jnp.zeros(lane.shape, F32)
    for kk in range(TOP_K):
        member = jnp.where(lane == idxs[kk], 1.0, member)
    before = jnp.dot(tri_ref[...], member.astype(BF16), preferred_element_type=F32) + cnt_ref[...]
    cnt_ref[...] += jnp.sum(member, axis=0, keepdims=True)
    count_ref[...] = jnp.broadcast_to(cnt_ref[...], count_ref.shape).astype(jnp.int32)
    idx_out = jnp.zeros(lane.shape, jnp.int32)
    gate_out = jnp.zeros(lane.shape, F32)
    for kk in range(TOP_K):
        rank = jnp.sum(jnp.where(lane == idxs[kk], before, 0.0), axis=-1, keepdims=True).astype(jnp.int32)
        idx_out = jnp.where(lane == kk, idxs[kk], idx_out)
        idx_out = jnp.where(lane == TOP_K + kk, rank, idx_out)
        gate_out = jnp.where(lane == kk, es[kk] / den, gate_out)
    idx_ref[...] = idx_out
    gate_ref[...] = gate_out


def _outproj(ret_out, diff_out, x2, gate1, scale2, shift2, norm2_w, w_out, rw_pad, rb_pad, seq_len, tt):
    t_total, d = x2.shape
    nps = seq_len // tt
    row = lambda i: (i, 0)
    mod = lambda i: (i // nps, 0, 0)
    fixed = lambda i: (0, 0)
    pos = jnp.arange(tt)
    tri = (pos[None, :] < pos[:, None]).astype(BF16)
    return pl.pallas_call(
        _outproj_kernel,
        grid=(t_total // tt,),
        in_specs=[pl.BlockSpec((tt, HALF_WIDTH), row),
                  pl.BlockSpec((tt, HALF_WIDTH), row),
                  pl.BlockSpec((tt, d), row),
                  pl.BlockSpec((None, 1, d), mod),
                  pl.BlockSpec((None, 1, d), mod),
                  pl.BlockSpec((None, 1, d), mod),
                  pl.BlockSpec((1, d), fixed),
                  pl.BlockSpec((2 * HALF_WIDTH, d), fixed),
                  pl.BlockSpec((2, d, LANES), lambda i: (0, 0, 0)),
                  pl.BlockSpec((1, LANES), fixed),
                  pl.BlockSpec((tt, tt), fixed)],
        out_specs=[pl.BlockSpec((tt, d), row), pl.BlockSpec((tt * SUBLANES, LANES), row),
                   pl.BlockSpec((tt, LANES), row), pl.BlockSpec((tt, LANES), row),
                   pl.BlockSpec((SUBLANES, LANES), fixed)],
        out_shape=[jax.ShapeDtypeStruct((t_total, d), F32), jax.ShapeDtypeStruct((t_total * SUBLANES, LANES), F32),
                   jax.ShapeDtypeStruct((t_total, LANES), jnp.int32),
                   jax.ShapeDtypeStruct((t_total, LANES), F32),
                   jax.ShapeDtypeStruct((SUBLANES, LANES), jnp.int32)],
        scratch_shapes=[pltpu.VMEM((1, LANES), F32)],
        compiler_params=_cparams(("arbitrary",)),
    )(ret_out, diff_out, x2, gate1, scale2, shift2, norm2_w, w_out, rw_pad, rb_pad, tri)


def _dispatch_kernel(fill_ref, dest_ref, h2_ref, xs_hbm, zero_ref, sem):
    i = pl.program_id(0)
    tokens = h2_ref.shape[0] // SUBLANES
    for t in range(tokens):
        for kk in range(TOP_K):
            pltpu.async_copy(h2_ref.at[pl.ds(t * SUBLANES, SUBLANES)],
                             xs_hbm.at[pl.ds(pl.multiple_of(dest_ref[t * TOP_K + kk], SUBLANES), SUBLANES)],
                             sem, priority=kk % DMA_QUEUES)
    for kk in range(TOP_K):
        pltpu.make_async_copy(h2_ref, xs_hbm.at[pl.ds(0, tokens * SUBLANES)], sem).wait()

    @pl.when(i == pl.num_programs(0) - 1)
    def _():
        zero_ref[...] = jnp.zeros_like(zero_ref)

        def fill(n, carry):
            pltpu.make_async_copy(zero_ref, xs_hbm.at[pl.ds(pl.multiple_of(fill_ref[n], SUBLANES), SUBLANES)],
                                  sem).start()
            return carry

        lax.fori_loop(0, fill_ref.shape[0], fill, 0)
        for _ in range(fill_ref.shape[0] // tokens):
            pltpu.make_async_copy(h2_ref, xs_hbm.at[pl.ds(0, tokens * SUBLANES)], sem).wait()


def _dispatch(h2, dest, fill_rows, n_rows, tt):
    t_total = h2.shape[0] // SUBLANES
    assert fill_rows.shape[0] % tt == 0
    grid_spec = pltpu.PrefetchScalarGridSpec(
        num_scalar_prefetch=1,
        grid=(t_total // tt,),
        in_specs=[pl.BlockSpec((tt * TOP_K,), lambda i, fill: (i,), memory_space=pltpu.SMEM),
                  pl.BlockSpec((tt * SUBLANES, LANES), lambda i, fill: (i, 0))],
        out_specs=pl.BlockSpec(memory_space=pl.ANY),
        scratch_shapes=[pltpu.VMEM((SUBLANES, LANES), F32), pltpu.SemaphoreType.DMA],
    )
    return pl.pallas_call(
        _dispatch_kernel,
        grid_spec=grid_spec,
        out_shape=jax.ShapeDtypeStruct((n_rows * SUBLANES, LANES), F32),
        compiler_params=_cparams(("arbitrary",)),
    )(fill_rows, dest, h2)


def _expert_kernel(be_ref, act_ref, dst_prev_ref, x_ref, wgu_ref, bgu_ref, wd_ref, bd_ref, out_hbm, y0, y1, sem_out):
    i = pl.program_id(0)
    rows = y0.shape[0] // SUBLANES
    d_ff = wd_ref.shape[1]
    prev_active = jnp.logical_and(i > 0, act_ref[jnp.maximum(i - 1, 0)] == 1)

    @pl.when(i == 0)
    def _():
        y1[...] = jnp.zeros_like(y1)

    def body(ycur, yprev):
        @pl.when(prev_active)
        def _():
            pltpu.make_async_copy(ycur, out_hbm.at[pl.ds(0, rows * SUBLANES)], sem_out).wait()

        @pl.when(act_ref[i] == 1)
        def _():
            for r in range(rows):
                pltpu.async_copy(yprev.at[pl.ds(r * SUBLANES, SUBLANES)],
                                 out_hbm.at[pl.ds(pl.multiple_of(dst_prev_ref[r], SUBLANES), SUBLANES)],
                                 sem_out, priority=r % DMA_QUEUES)
            x = _load_token_tiles(x_ref).astype(BF16)
            gu = jnp.dot(x, wgu_ref[0], preferred_element_type=F32) + bgu_ref[0]
            glu = jnp.minimum(gu[:, :d_ff], SWIGLU_LIMIT)
            lin = jnp.clip(gu[:, d_ff:], -SWIGLU_LIMIT, SWIGLU_LIMIT)
            hidden = glu * jax.nn.sigmoid(SWIGLU_ALPHA * glu) * (lin + 1.0)
            _store_token_tiles(ycur, jnp.dot(hidden.astype(BF16), wd_ref[0], preferred_element_type=F32) + bd_ref[0])

    @pl.when(i % 2 == 0)
    def _():
        body(y0, y1)

    @pl.when(i % 2 == 1)
    def _():
        body(y1, y0)


def _experts(xs, block_expert, active, dst_prev, wgu, bgu, wd, bd, n_slots, rows):
    n_blocks = block_expert.shape[0]
    d, d_ff2 = wgu.shape[1:]
    assert d == SUBLANES * LANES
    buf = pltpu.VMEM((rows * SUBLANES, LANES), F32)
    grid_spec = pltpu.PrefetchScalarGridSpec(
        num_scalar_prefetch=2,
        grid=(n_blocks,),
        in_specs=[pl.BlockSpec((rows,), lambda i, be, act: (i,), memory_space=pltpu.SMEM),
                  pl.BlockSpec((rows * SUBLANES, LANES), lambda i, be, act: (i, 0)),
                  pl.BlockSpec((1, d, d_ff2), lambda i, be, act: (be[i], 0, 0)),
                  pl.BlockSpec((1, 1, d_ff2), lambda i, be, act: (be[i], 0, 0)),
                  pl.BlockSpec((1, d_ff2 // 2, d), lambda i, be, act: (be[i], 0, 0)),
                  pl.BlockSpec((1, 1, d), lambda i, be, act: (be[i], 0, 0))],
        out_specs=pl.BlockSpec(memory_space=pl.ANY),
        scratch_shapes=[buf, buf, pltpu.SemaphoreType.DMA],
    )
    return pl.pallas_call(
        _expert_kernel,
        grid_spec=grid_spec,
        out_shape=jax.ShapeDtypeStruct(((n_slots + rows) * SUBLANES, LANES), F32),
        compiler_params=_cparams(("arbitrary",)),
    )(block_expert, active, dst_prev, xs, wgu, bgu, wd, bd)


def _routing(top_idx, rank, counts, rows):
    t_total = top_idx.shape[0]
    padded = (counts + rows - 1) // rows * rows
    ends = jnp.cumsum(padded)
    starts = ends - padded
    dest = (starts[top_idx] + rank).reshape(-1)
    n_blocks = (t_total * TOP_K + N_EXPERTS * (rows - 1)) // rows + 2
    n_rows = n_blocks * rows
    n_slots = TOP_K * t_total
    tok = jnp.repeat(jnp.arange(t_total, dtype=jnp.int32), TOP_K)
    slot = jnp.tile(jnp.arange(TOP_K, dtype=jnp.int32), t_total)
    landing = n_slots + jnp.arange(n_rows, dtype=jnp.int32) % rows
    row_dst = landing.at[dest].set(slot * t_total + tok, unique_indices=True)
    dst_prev = jnp.concatenate([landing[:rows], row_dst[:-rows]])
    block_start = jnp.arange(n_blocks, dtype=jnp.int32) * rows
    block_expert = jnp.sum((ends[None, :] <= block_start[:, None]).astype(jnp.int32), axis=1)
    block_expert = jnp.minimum(block_expert, N_EXPERTS - 1)
    has_rows = block_start < ends[-1]
    active = jnp.logical_or(has_rows, jnp.concatenate([has_rows[:1], has_rows[:-1]])).astype(jnp.int32)
    pads = padded - counts
    pad_end = jnp.cumsum(pads)
    seg_first = jnp.concatenate([pad_end - pads, pad_end[-1:]])
    seg_row = jnp.concatenate([starts + counts, ends[-1:]])
    j = jnp.arange(n_rows - n_slots, dtype=jnp.int32)
    seg = jnp.sum((j[:, None] >= pad_end[None, :]).astype(jnp.int32), axis=1)
    pick = seg[:, None] == jnp.arange(N_EXPERTS + 1, dtype=jnp.int32)[None, :]
    fill = j + jnp.sum(jnp.where(pick, (seg_row - seg_first)[None, :], 0), axis=1)
    return block_expert, active, dest * SUBLANES, dst_prev * SUBLANES, fill * SUBLANES, n_rows


def _combine_kernel(x1_ref, g2_ref, gate_ref, *refs):
    y_refs, o_ref = refs[:TOP_K], refs[TOP_K]
    gates = gate_ref[...]
    moe = gates[:, 0:1] * _load_token_tiles(y_refs[0])
    for kk in range(1, TOP_K):
        moe += gates[:, kk:kk + 1] * _load_token_tiles(y_refs[kk])
    o_ref[...] = x1_ref[...] + g2_ref[...] * moe


def _combine(x1, gate2, gates, y4, seq_len, tt):
    t_total, d = x1.shape
    nps = seq_len // tt
    n_tiles = t_total // tt
    slot_specs = [pl.BlockSpec((tt * SUBLANES, LANES), functools.partial(lambda kk, i: (kk * n_tiles + i, 0), kk))
                  for kk in range(TOP_K)]
    return pl.pallas_call(
        _combine_kernel,
        grid=(n_tiles,),
        in_specs=[pl.BlockSpec((tt, d), lambda i: (i, 0)),
                  pl.BlockSpec((None, 1, d), lambda i: (i // nps, 0, 0)),
                  pl.BlockSpec((tt, LANES), lambda i: (i, 0))] + slot_specs,
        out_specs=pl.BlockSpec((tt, d), lambda i: (i, 0)),
        out_shape=jax.ShapeDtypeStruct((t_total, d), F32),
        compiler_params=_cparams(("parallel",)),
    )(x1, gate2, gates, *([y4] * TOP_K))


def _tiles(seq_len):
    token_tile = min(512, seq_len)
    ret_chunk = min(256, seq_len)
    attn_tile = min(512, seq_len)
    expert_rows = 256
    combine_tile = min(256, seq_len)
    return token_tile, ret_chunk, attn_tile, expert_rows, combine_tile


def _trunk(x, mod, lam_init, norm1_w, w_in_b, ret_a_fwd, ret_a_bwd, ret_norm_w, diff_q_norm_w, diff_k_norm_w,
           lam_q1, lam_k1, lam_q2, lam_k2, diff_subln_w, rel_bias, w_out_b, norm2_w, rw_pad, rb_pad,
           wgu_b, b_gate_up, wd_b, b_down):
    batch, seq_len, d = x.shape
    tt, c, t, rows, ct = _tiles(seq_len)
    x2 = x.reshape(batch * seq_len, d)
    shift1, scale1, gate1, shift2, scale2, gate2 = [mod[:, n][:, None, :] for n in range(6)]

    pos = jnp.arange(seq_len, dtype=F32)
    inv_freq = ROPE_BASE ** (-jnp.arange(0, HEAD_DIM, 2, dtype=F32) / HEAD_DIM)
    ang = pos[:, None] * inv_freq[None, :]
    cos_t = jnp.tile(jnp.cos(ang), (1, LANES // (HEAD_DIM // 2)))
    sin_h = jnp.sin(ang)
    sin_t = jnp.tile(jnp.concatenate([-sin_h, sin_h], axis=1), (1, LANES // HEAD_DIM))
    qnw = jnp.tile(diff_q_norm_w.astype(F32), HALF_WIDTH // HEAD_DIM)[None, :]
    knw = jnp.tile(diff_k_norm_w.astype(F32), HALF_WIDTH // HEAD_DIM)[None, :]
    r = jnp.arange(HALF_WIDTH)
    bd = jnp.where((r[:, None] // HEAD_DIM) == (r[None, :] // HEAD_DIM), 1.0 / HEAD_DIM, 0.0).astype(BF16)

    rq, rk, rv, rg, dq, dk, dv = _inproj(x2, scale1, shift1, norm1_w[None, :], w_in_b, cos_t, sin_t, qnw, knw, bd,
                                         seq_len, tt)

    tabs = _retention_tables(ret_a_fwd, ret_a_bwd, c)
    ret_out = _retention(rq, rk, rv, rg, tabs, ret_norm_w.reshape(1, HALF_WIDTH).astype(F32), batch, seq_len, c)

    lam = (jnp.exp(jnp.sum(lam_q1.astype(F32) * lam_k1.astype(F32)))
           - jnp.exp(jnp.sum(lam_q2.astype(F32) * lam_k2.astype(F32))) + lam_init)
    subln = (diff_subln_w.astype(F32) * (1.0 - lam_init))[None, :]
    diff_out = _attention(dq, dk, dv, _bias_tables(rel_bias, t), lam[None].astype(F32), subln, batch, seq_len, t)

    x1, h2, idx_pad, gates_pad, counts = _outproj(ret_out, diff_out, x2, gate1, scale2, shift2, norm2_w[None, :],
                                                  w_out_b, rw_pad, rb_pad, seq_len, tt)
    block_expert, active, dest, dst_prev, fill, n_rows = _routing(idx_pad[:, :TOP_K], idx_pad[:, TOP_K:2 * TOP_K],
                                                                  counts[0, :N_EXPERTS], rows)
    xs = _dispatch(h2, dest, fill, n_rows, ct)
    y4 = _experts(xs, block_expert, active, dst_prev, wgu_b, b_gate_up, wd_b, b_down, TOP_K * batch * seq_len, rows)
    y = _combine(x1, gate2, gates_pad, y4, seq_len, ct)
    return y.reshape(batch, seq_len, d)


def kernel(x_prompt, x_sample, c_prompt, c_sample, w_ada, b_ada, norm1_w, w_in, ret_a_fwd, ret_a_bwd, ret_norm_w,
           diff_q_norm_w, diff_k_norm_w, lam_q1, lam_k1, lam_q2, lam_k2, diff_subln_w, rel_bias, w_out, norm2_w,
           router_w, router_b, w_gate_up, b_gate_up, w_down, b_down):
    depth = w_ada.shape[0]
    d = x_prompt.shape[-1]
    n_prompt = c_prompt.shape[0]
    n_cond = n_prompt + c_sample.shape[0]
    c_all = jnp.concatenate([c_prompt, c_sample], axis=0).astype(F32)
    c_pad = jnp.pad(c_all, ((0, -n_cond % 8), (0, 0)))
    xs = [x_prompt, x_sample]
    for l in range(depth):
        lam_init = 0.8 - 0.6 * math.exp(-0.3 * l)
        mod = _ada(c_pad, w_ada[l], b_ada[l])[:n_cond].reshape(n_cond, 6, d)
        rw = jnp.pad(router_w[l].astype(F32), ((0, 0), (0, LANES - N_EXPERTS)))
        rw_hi = rw.astype(BF16)
        rw_pad = jnp.stack([rw_hi, (rw - rw_hi.astype(F32)).astype(BF16)])
        rb_pad = jnp.pad(router_b[l].astype(F32), (0, LANES - N_EXPERTS), constant_values=-jnp.inf)[None, :]
        shared = (norm1_w[l], w_in[l].astype(BF16), ret_a_fwd[l], ret_a_bwd[l], ret_norm_w[l], diff_q_norm_w[l],
                  diff_k_norm_w[l], lam_q1[l], lam_k1[l], lam_q2[l], lam_k2[l], diff_subln_w[l], rel_bias,
                  w_out[l].astype(BF16), norm2_w[l], rw_pad, rb_pad,
                  w_gate_up[l].astype(BF16), b_gate_up[l][:, None, :], w_down[l].astype(BF16), b_down[l][:, None, :])
        xs = [_trunk(xs[0], mod[:n_prompt], lam_init, *shared),
              _trunk(xs[1], mod[n_prompt:], lam_init, *shared)]
    return (xs[0], xs[1])
```

```python
import functools
import math

import jax
import jax.numpy as jnp
import numpy as np
from jax import lax
from jax.experimental import pallas as pl
from jax.experimental.pallas import tpu as pltpu

F32 = jnp.float32
BF16 = jnp.bfloat16
HIGHEST = lax.Precision.HIGHEST

LANES = 128
SUBLANES = 8
HEAD_DIM = 64
RET_HEADS = 8
DIFF_HEADS = 4
HALF_WIDTH = 512
N_PAIRS = HALF_WIDTH // LANES
ROPE_BASE = 10000.0
NUM_BUCKETS = 32
MAX_DISTANCE = 128
N_EXPERTS = 32
TOP_K = 4
SWIGLU_LIMIT = 7.0
SWIGLU_ALPHA = 1.702
EPS = 1e-6
LOG2E = math.log2(math.e)
ATTN_ROWS = 128
ATTN_UNROLL = 16
VMEM_LIMIT = 56 * 1024 * 1024
DMA_QUEUES = 2


def _cparams(sem):
    return pltpu.CompilerParams(dimension_semantics=sem, vmem_limit_bytes=VMEM_LIMIT)


def _ada_kernel(c_ref, w_ref, b_ref, o_ref):
    c = c_ref[...]
    a = c * jax.nn.sigmoid(c)
    o_ref[...] = jnp.dot(a, w_ref[...], preferred_element_type=F32, precision=HIGHEST) + b_ref[...]


def _ada(c_pad, w_ada, b_ada):
    rows, d = c_pad.shape
    n = w_ada.shape[1]
    tn = d
    return pl.pallas_call(
        _ada_kernel,
        grid=(n // tn,),
        in_specs=[pl.BlockSpec((rows, d), lambda j: (0, 0)),
                  pl.BlockSpec((d, tn), lambda j: (0, j)),
                  pl.BlockSpec((1, tn), lambda j: (0, j))],
        out_specs=pl.BlockSpec((rows, tn), lambda j: (0, j)),
        out_shape=jax.ShapeDtypeStruct((rows, n), F32),
        compiler_params=_cparams(("parallel",)),
    )(c_pad, w_ada, b_ada.reshape(1, n))


def _rotate_half(xg):
    lane = lax.broadcasted_iota(jnp.int32, xg.shape, 1)
    first = (lane % HEAD_DIM) < (HEAD_DIM // 2)
    return jnp.where(first, pltpu.roll(xg, LANES - HEAD_DIM // 2, 1), pltpu.roll(xg, HEAD_DIM // 2, 1))


def _inproj_kernel(x_ref, sc_ref, sh_ref, nw_ref, w_ref, cos_ref, sin_ref, qnw_ref, knw_ref, bd_ref,
                   rq_ref, rk_ref, rv_ref, rg_ref, dq_ref, dk_ref, dv_ref):
    x = x_ref[...]
    ms = jnp.mean(x * x, axis=-1, keepdims=True)
    h = x * lax.rsqrt(ms + EPS) * nw_ref[...]
    h = (h * (1.0 + sc_ref[...]) + sh_ref[...]).astype(BF16)

    def piece(n):
        return jnp.dot(h, w_ref[:, n * HALF_WIDTH:(n + 1) * HALF_WIDTH], preferred_element_type=F32)

    cos = cos_ref[...]
    sin = sin_ref[...]

    def rotary(p, out_ref, scale):
        for g in range(N_PAIRS):
            xg = p[:, g * LANES:(g + 1) * LANES]
            y = xg * cos + _rotate_half(xg) * sin
            out_ref[:, g * LANES:(g + 1) * LANES] = (y * scale).astype(BF16)

    def head_norm(p, w, out_ref, scale):
        msq = jnp.dot((p * p).astype(BF16), bd_ref[...], preferred_element_type=F32)
        out_ref[...] = (p * lax.rsqrt(msq + EPS) * w * scale).astype(BF16)

    rotary(piece(0), rq_ref, 1.0)
    rotary(piece(1), rk_ref, HEAD_DIM ** -0.5)
    rv_ref[...] = piece(2).astype(BF16)
    rg_ref[...] = piece(3).astype(BF16)
    head_norm(piece(4), qnw_ref[...], dq_ref, HEAD_DIM ** -0.5 * LOG2E)
    head_norm(piece(5), knw_ref[...], dk_ref, 1.0)
    dv = piece(6).astype(BF16)
    ones = jnp.ones((dv.shape[0], LANES), BF16)
    for hd in range(DIFF_HEADS):
        dv_ref[:, 2 * hd * LANES:(2 * hd + 1) * LANES] = dv[:, hd * LANES:(hd + 1) * LANES]
        dv_ref[:, (2 * hd + 1) * LANES:(2 * hd + 2) * LANES] = ones


def _inproj(x2, scale1, shift1, norm_w, w_in, cos_t, sin_t, qnw, knw, bd, seq_len, tt):
    t_total, d = x2.shape
    nps = seq_len // tt
    n_in = w_in.shape[1]
    row = lambda i: (i, 0)
    mod = lambda i: (i // nps, 0, 0)
    pos = lambda i: (i % nps, 0)
    fixed = lambda i: (0, 0)
    out_sd = jax.ShapeDtypeStruct((t_total, HALF_WIDTH), BF16)
    return pl.pallas_call(
        _inproj_kernel,
        grid=(t_total // tt,),
        in_specs=[pl.BlockSpec((tt, d), row),
                  pl.BlockSpec((None, 1, d), mod),
                  pl.BlockSpec((None, 1, d), mod),
                  pl.BlockSpec((1, d), fixed),
                  pl.BlockSpec((d, n_in), fixed),
                  pl.BlockSpec((tt, LANES), pos),
                  pl.BlockSpec((tt, LANES), pos),
                  pl.BlockSpec((1, HALF_WIDTH), fixed),
                  pl.BlockSpec((1, HALF_WIDTH), fixed),
                  pl.BlockSpec((HALF_WIDTH, HALF_WIDTH), fixed)],
        out_specs=[pl.BlockSpec((tt, HALF_WIDTH), row)] * 6 + [pl.BlockSpec((tt, 2 * HALF_WIDTH), row)],
        out_shape=[out_sd] * 6 + [jax.ShapeDtypeStruct((t_total, 2 * HALF_WIDTH), BF16)],
        compiler_params=_cparams(("parallel",)),
    )(x2, scale1, shift1, norm_w, w_in, cos_t, sin_t, qnw, knw, bd)


def _kv_update(state_ref, p, k, v, kdec, cdec, bdmask):
    kd = (k.astype(F32) * kdec).astype(BF16)
    kv = lax.dot_general(kd, v, (((0,), (0,)), ((), ())), preferred_element_type=F32)
    state_ref[p] = state_ref[p] * cdec + kv * bdmask


def _ret_state_kernel(k_ref, v_ref, kdec_ref, cdec_ref, bdmask_ref, sb_ref, state_ref):
    @pl.when(pl.program_id(1) == 0)
    def _():
        state_ref[...] = jnp.zeros_like(state_ref)

    sb_ref[...] = state_ref[...]
    for p in range(N_PAIRS):
        sl = slice(p * LANES, (p + 1) * LANES)
        _kv_update(state_ref, p, k_ref[:, sl], v_ref[:, sl], kdec_ref[:, sl], cdec_ref[:, sl], bdmask_ref[...])


def _ret_main_kernel(q_ref, k_ref, v_ref, g_ref, dmat_ref, qdf_ref, qdb_ref, kdf_ref, cdf_ref, bdmask_ref,
                     nw_ref, sb_ref, o_ref, state_ref):
    @pl.when(pl.program_id(1) == 0)
    def _():
        state_ref[...] = jnp.zeros_like(state_ref)

    c = q_ref.shape[0]
    lane = lax.broadcasted_iota(jnp.int32, (c, LANES), 1)
    lo = lane < HEAD_DIM
    for p in range(N_PAIRS):
        sl = slice(p * LANES, (p + 1) * LANES)
        q = q_ref[:, sl]
        k = k_ref[:, sl]
        v = v_ref[:, sl]
        qf = q.astype(F32)
        acc = jnp.dot((qf * qdf_ref[:, sl]).astype(BF16), state_ref[p].astype(BF16), preferred_element_type=F32)
        acc += jnp.dot((qf * qdb_ref[:, sl]).astype(BF16), sb_ref[p].astype(BF16), preferred_element_type=F32)
        for hh in range(2):
            sel = lo if hh == 0 else jnp.logical_not(lo)
            qm = jnp.where(sel, q, jnp.zeros_like(q))
            vm = jnp.where(sel, v, jnp.zeros_like(v))
            s = lax.dot_general(qm, k, (((1,), (1,)), ((), ())), preferred_element_type=F32)
            w = (s * dmat_ref[2 * p + hh]).astype(BF16)
            acc += jnp.dot(w, vm, preferred_element_type=F32)
        _kv_update(state_ref, p, k, v, kdf_ref[:, sl], cdf_ref[:, sl], bdmask_ref[...])
        sq = acc * acc
        ms_lo = jnp.sum(jnp.where(lo, sq, 0.0), axis=-1, keepdims=True)
        ms_hi = jnp.sum(jnp.where(lo, 0.0, sq), axis=-1, keepdims=True)
        ms = jnp.where(lo, ms_lo, ms_hi) * (1.0 / HEAD_DIM)
        y = acc * lax.rsqrt(ms + EPS) * nw_ref[:, sl]
        gf = g_ref[:, sl].astype(F32)
        o_ref[:, sl] = (gf * jax.nn.sigmoid(gf) * y).astype(BF16)


def _retention(rq, rk, rv, rg, tabs, ret_nw, batch, seq_len, c):
    t_total = rq.shape[0]
    nc = seq_len // c
    dmat, qdf, qdb, kdf, kdb, cdf, cdb, bdmask = tabs
    fixed2 = lambda b, i: (0, 0)
    rev = lambda b, i: (b * nc + nc - 1 - i, 0)
    fwd = lambda b, i: (b * nc + i, 0)
    tile = pl.BlockSpec((c, HALF_WIDTH), fwd)
    tile_rev = pl.BlockSpec((c, HALF_WIDTH), rev)
    tab = pl.BlockSpec((c, HALF_WIDTH), fixed2)
    vec = pl.BlockSpec((1, HALF_WIDTH), fixed2)
    mask = pl.BlockSpec((LANES, LANES), fixed2)
    state = pltpu.VMEM((N_PAIRS, LANES, LANES), F32)
    sb = pl.pallas_call(
        _ret_state_kernel,
        grid=(batch, nc),
        in_specs=[tile_rev, tile_rev, tab, vec, mask],
        out_specs=pl.BlockSpec((None, None, N_PAIRS, LANES, LANES), lambda b, i: (b, nc - 1 - i, 0, 0, 0)),
        out_shape=jax.ShapeDtypeStruct((batch, nc, N_PAIRS, LANES, LANES), F32),
        scratch_shapes=[state],
        compiler_params=_cparams(("parallel", "arbitrary")),
    )(rk, rv, kdb, cdb, bdmask)
    return pl.pallas_call(
        _ret_main_kernel,
        grid=(batch, nc),
        in_specs=[tile, tile, tile, tile,
                  pl.BlockSpec((RET_HEADS, c, c), lambda b, i: (0, 0, 0)),
                  tab, tab, tab, vec, mask, vec,
                  pl.BlockSpec((None, None, N_PAIRS, LANES, LANES), lambda b, i: (b, i, 0, 0, 0))],
        out_specs=tile,
        out_shape=jax.ShapeDtypeStruct((t_total, HALF_WIDTH), BF16),
        scratch_shapes=[state],
        compiler_params=_cparams(("parallel", "arbitrary")),
    )(rq, rk, rv, rg, dmat, qdf, qdb, kdf, cdf, bdmask, ret_nw, sb)


def _retention_tables(a_fwd, a_bwd, c):
    lg_f = jnp.log1p(-jnp.exp(a_fwd.astype(F32)))
    lg_b = jnp.log1p(-jnp.exp(a_bwd.astype(F32)))
    pos = jnp.arange(c, dtype=F32)
    diff = pos[:, None] - pos[None, :]
    dmat = jnp.where(diff[None] >= 0,
                     jnp.exp(jnp.maximum(diff, 0.0)[None] * lg_f[:, None, None]),
                     jnp.exp(jnp.maximum(-diff, 0.0)[None] * lg_b[:, None, None]))
    lane_f = jnp.repeat(lg_f, HEAD_DIM)[None, :]
    lane_b = jnp.repeat(lg_b, HEAD_DIM)[None, :]
    qdf = jnp.exp((pos + 1.0)[:, None] * lane_f)
    kdf = jnp.exp((c - 1.0 - pos)[:, None] * lane_f)
    qdb = jnp.exp((c - pos)[:, None] * lane_b)
    kdb = jnp.exp(pos[:, None] * lane_b)
    cdf = jnp.exp(c * lane_f)
    cdb = jnp.exp(c * lane_b)
    r = jnp.arange(LANES)
    bdmask = ((r[:, None] // HEAD_DIM) == (r[None, :] // HEAD_DIM)).astype(F32)
    return dmat, qdf, qdb, kdf, kdb, cdf, cdb, bdmask


def _attn_kernel(lam_ref, q_ref, k_ref, v_ref, bias_ref, sw_ref, o_ref, qq_ref, m_ref, acc_ref):
    i = pl.program_id(2)
    t = q_ref.shape[0]
    nk = k_ref.shape[0] // t
    q = q_ref[...]
    lane = lax.broadcasted_iota(jnp.int32, q.shape, 1)
    zero = jnp.zeros_like(q)
    qq_ref[...] = jnp.concatenate([jnp.where(lane < HEAD_DIM, q, zero), jnp.where(lane < HEAD_DIM, zero, q)], axis=0)
    m_ref[...] = jnp.full_like(m_ref, -jnp.inf)
    acc_ref[...] = jnp.zeros_like(acc_ref)

    def tile(j):
        keys = pl.ds(pl.multiple_of(j * t, t), t)
        which = jnp.clip(j - i, -2, 2) + 2
        for r in range(0, 2 * t, ATTN_ROWS):
            rows = slice(r, r + ATTN_ROWS)
            s = lax.dot_general(qq_ref[rows, :], k_ref[keys, :], (((1,), (1,)), ((), ())),
                                preferred_element_type=F32)
            s = s + bias_ref[which, r % t:r % t + ATTN_ROWS, :]
            m_prev = m_ref[rows, :]
            m_new = jnp.maximum(m_prev, jnp.max(s, axis=1, keepdims=True))
            alpha = jnp.exp2(m_prev - m_new)
            p = jnp.exp2(s - jnp.concatenate([m_new] * (t // LANES), axis=1)).astype(BF16)
            pv = jnp.dot(p, v_ref[keys, :], preferred_element_type=F32)
            acc_ref[rows, :] = jnp.concatenate([alpha, alpha], axis=1) * acc_ref[rows, :] + pv
            m_ref[rows, :] = m_new

    def step(jj, carry):
        for u in range(ATTN_UNROLL):
            tile(jj * ATTN_UNROLL + u)
        return carry

    lax.fori_loop(0, nk // ATTN_UNROLL, step, 0)

    acc = acc_ref[...]
    o = acc[:, :LANES] / acc[:, LANES:]
    att = o[:t] - lam_ref[0] * o[t:]
    msq = jnp.mean(att * att, axis=-1, keepdims=True)
    o_ref[...] = (att * lax.rsqrt(msq + EPS) * sw_ref[...]).astype(BF16)


def _attention(dq, dk, dv, bias_tiles, lam, subln_w, batch, seq_len, t):
    t_total = dq.shape[0]
    nq = seq_len // t
    assert nq % ATTN_UNROLL == 0 and t % ATTN_ROWS == 0
    grid_spec = pltpu.PrefetchScalarGridSpec(
        num_scalar_prefetch=1,
        grid=(batch, DIFF_HEADS, nq),
        in_specs=[pl.BlockSpec((t, LANES), lambda b, h, i, s: (b * nq + i, h)),
                  pl.BlockSpec((seq_len, LANES), lambda b, h, i, s: (b, h)),
                  pl.BlockSpec((seq_len, 2 * LANES), lambda b, h, i, s: (b, h)),
                  pl.BlockSpec((None, 5, t, t), lambda b, h, i, s: (h, 0, 0, 0)),
                  pl.BlockSpec((1, LANES), lambda b, h, i, s: (0, 0))],
        out_specs=pl.BlockSpec((t, LANES), lambda b, h, i, s: (b * nq + i, h)),
        scratch_shapes=[pltpu.VMEM((2 * t, LANES), BF16), pltpu.VMEM((2 * t, LANES), F32),
                        pltpu.VMEM((2 * t, 2 * LANES), F32)],
    )
    return pl.pallas_call(
        _attn_kernel,
        grid_spec=grid_spec,
        out_shape=jax.ShapeDtypeStruct((t_total, HALF_WIDTH), BF16),
        compiler_params=_cparams(("parallel", "parallel", "parallel")),
    )(lam, dq, dk, dv, bias_tiles, subln_w)


def _t5_bucket(rel):
    nb = NUM_BUCKETS // 2
    max_exact = nb // 2
    n = jnp.abs(rel)
    base = jnp.where(rel > 0, nb, 0)
    nf = jnp.maximum(n, 1).astype(F32)
    large = max_exact + (jnp.log(nf / max_exact) / math.log(MAX_DISTANCE / max_exact)
                         * (nb - max_exact)).astype(jnp.int32)
    large = jnp.minimum(large, nb - 1)
    return base + jnp.where(n < max_exact, n, large)


def _bias_tables(rel_bias, t):
    assert t >= MAX_DISTANCE
    table = rel_bias.astype(F32) * LOG2E
    qq = jnp.arange(t, dtype=jnp.int32)[:, None]
    kk = jnp.arange(t, dtype=jnp.int32)[None, :]
    bucket = jnp.stack([_t5_bucket(d * t + kk - qq) for d in (-2, -1, 0, 1, 2)], axis=0)
    onehot = (bucket[..., None] == jnp.arange(NUM_BUCKETS, dtype=jnp.int32)).astype(F32)
    return jnp.einsum('dqkn,nh->hdqk', onehot, table, precision=HIGHEST)


def _store_token_tiles(ref, value):
    n = value.shape[0]
    for s in range(SUBLANES):
        ref[pl.ds(s, n, stride=SUBLANES), :] = value[:, s * LANES:(s + 1) * LANES]


def _load_token_tiles(ref):
    n = ref.shape[0] // SUBLANES
    return jnp.concatenate([ref[pl.ds(s, n, stride=SUBLANES), :] for s in range(SUBLANES)], axis=1)


def _outproj_kernel(ret_ref, dif_ref, x_ref, g1_ref, sc2_ref, sh2_ref, nw2_ref, wo_ref, rw_ref, rb_ref, tri_ref,
                    x1_ref, h2_ref, idx_ref, gate_ref, count_ref, cnt_ref):
    mix = jnp.dot(ret_ref[...], wo_ref[:HALF_WIDTH, :], preferred_element_type=F32)
    mix += jnp.dot(dif_ref[...], wo_ref[HALF_WIDTH:, :], preferred_element_type=F32)
    x1 = x_ref[...] + g1_ref[...] * mix
    x1_ref[...] = x1
    ms = jnp.mean(x1 * x1, axis=-1, keepdims=True)
    h2 = x1 * lax.rsqrt(ms + EPS) * nw2_ref[...]
    h2 = h2 * (1.0 + sc2_ref[...]) + sh2_ref[...]
    _store_token_tiles(h2_ref, h2)
    h2_hi = h2.astype(BF16)
    h2_lo = (h2 - h2_hi.astype(F32)).astype(BF16)
    logits = (jnp.dot(h2_hi, rw_ref[0], preferred_element_type=F32)
              + jnp.dot(h2_lo, rw_ref[0], preferred_element_type=F32)
              + jnp.dot(h2_hi, rw_ref[1], preferred_element_type=F32)) + rb_ref[...]
    lane = lax.broadcasted_iota(jnp.int32, logits.shape, 1)
    vals, idxs = [], []
    for _ in range(TOP_K):
        m = jnp.max(logits, axis=-1, keepdims=True)
        am = jnp.min(jnp.where(logits == m, lane, LANES), axis=-1, keepdims=True)
        vals.append(m)
        idxs.append(am)
        logits = jnp.where(lane == am, -jnp.inf, logits)
    es = [jnp.exp(v - vals[0]) for v in vals]
    den = es[0] + es[1] + es[2] + es[3]
    @pl.when(pl.program_id(0) == 0)
    def _():
        cnt_ref[...] = jnp.zeros_like(cnt_ref)

    member = jnp.zeros(lane.shape, F32)
    for kk in range(TOP_K):
        member = jnp.where(lane == idxs[kk], 1.0, member)
    before = jnp.dot(tri_ref[...], member.astype(BF16), preferred_element_type=F32) + cnt_ref[...]
    cnt_ref[...] += jnp.sum(member, axis=0, keepdims=True)
    count_ref[...] = jnp.broadcast_to(cnt_ref[...], count_ref.shape).astype(jnp.int32)
    idx_out = jnp.zeros(lane.shape, jnp.int32)
    gate_out = jnp.zeros(lane.shape, F32)
    for kk in range(TOP_K):
        rank = jnp.sum(jnp.where(lane == idxs[kk], before, 0.0), axis=-1, keepdims=True).astype(jnp.int32)
        idx_out = jnp.where(lane == kk, idxs[kk], idx_out)
        idx_out = jnp.where(lane == TOP_K + kk, rank, idx_out)
        gate_out = jnp.where(lane == kk, es[kk] / den, gate_out)
    idx_ref[...] = idx_out
    gate_ref[...] = gate_out


def _outproj(ret_out, diff_out, x2, gate1, scale2, shift2, norm2_w, w_out, rw_pad, rb_pad, seq_len, tt):
    t_total, d = x2.shape
    nps = seq_len // tt
    row = lambda i: (i, 0)
    mod = lambda i: (i // nps, 0, 0)
    fixed = lambda i: (0, 0)
    pos = jnp.arange(tt)
    tri = (pos[None, :] < pos[:, None]).astype(BF16)
    return pl.pallas_call(
        _outproj_kernel,
        grid=(t_total // tt,),
        in_specs=[pl.BlockSpec((tt, HALF_WIDTH), row),
                  pl.BlockSpec((tt, HALF_WIDTH), row),
                  pl.BlockSpec((tt, d), row),
                  pl.BlockSpec((None, 1, d), mod),
                  pl.BlockSpec((None, 1, d), mod),
                  pl.BlockSpec((None, 1, d), mod),
                  pl.BlockSpec((1, d), fixed),
                  pl.BlockSpec((2 * HALF_WIDTH, d), fixed),
                  pl.BlockSpec((2, d, LANES), lambda i: (0, 0, 0)),
                  pl.BlockSpec((1, LANES), fixed),
                  pl.BlockSpec((tt, tt), fixed)],
        out_specs=[pl.BlockSpec((tt, d), row), pl.BlockSpec((tt * SUBLANES, LANES), row),
                   pl.BlockSpec((tt, LANES), row), pl.BlockSpec((tt, LANES), row),
                   pl.BlockSpec((SUBLANES, LANES), fixed)],
        out_shape=[jax.ShapeDtypeStruct((t_total, d), F32), jax.ShapeDtypeStruct((t_total * SUBLANES, LANES), F32),
                   jax.ShapeDtypeStruct((t_total, LANES), jnp.int32),
                   jax.ShapeDtypeStruct((t_total, LANES), F32),
                   jax.ShapeDtypeStruct((SUBLANES, LANES), jnp.int32)],
        scratch_shapes=[pltpu.VMEM((1, LANES), F32)],
        compiler_params=_cparams(("arbitrary",)),
    )(ret_out, diff_out, x2, gate1, scale2, shift2, norm2_w, w_out, rw_pad, rb_pad, tri)


def _dispatch_kernel(fill_ref, dest_ref, h2_ref, xs_hbm, zero_ref, sem):
    i = pl.program_id(0)
    tokens = h2_ref.shape[0] // SUBLANES
    for t in range(tokens):
        for kk in range(TOP_K):
            pltpu.async_copy(h2_ref.at[pl.ds(t * SUBLANES, SUBLANES)],
                             xs_hbm.at[pl.ds(pl.multiple_of(dest_ref[t * TOP_K + kk], SUBLANES), SUBLANES)],
                             sem, priority=kk % DMA_QUEUES)
    for kk in range(TOP_K):
        pltpu.make_async_copy(h2_ref, xs_hbm.at[pl.ds(0, tokens * SUBLANES)], sem).wait()

    @pl.when(i == pl.num_programs(0) - 1)
    def _():
        zero_ref[...] = jnp.zeros_like(zero_ref)

        def fill(n, carry):
            pltpu.make_async_copy(zero_ref, xs_hbm.at[pl.ds(pl.multiple_of(fill_ref[n], SUBLANES), SUBLANES)],
                                  sem).start()
            return carry

        lax.fori_loop(0, fill_ref.shape[0], fill, 0)
        for _ in range(fill_ref.shape[0] // tokens):
            pltpu.make_async_copy(h2_ref, xs_hbm.at[pl.ds(0, tokens * SUBLANES)], sem).wait()


def _dispatch(h2, dest, fill_rows, n_rows, tt):
    t_total = h2.shape[0] // SUBLANES
    assert fill_rows.shape[0] % tt == 0
    grid_spec = pltpu.PrefetchScalarGridSpec(
        num_scalar_prefetch=1,
        grid=(t_total // tt,),
        in_specs=[pl.BlockSpec((tt * TOP_K,), lambda i, fill: (i,), memory_space=pltpu.SMEM),
                  pl.BlockSpec((tt * SUBLANES, LANES), lambda i, fill: (i, 0))],
        out_specs=pl.BlockSpec(memory_space=pl.ANY),
        scratch_shapes=[pltpu.VMEM((SUBLANES, LANES), F32), pltpu.SemaphoreType.DMA],
    )
    return pl.pallas_call(
        _dispatch_kernel,
        grid_spec=grid_spec,
        out_shape=jax.ShapeDtypeStruct((n_rows * SUBLANES, LANES), F32),
        compiler_params=_cparams(("arbitrary",)),
    )(fill_rows, dest, h2)


def _expert_kernel(be_ref, act_ref, dst_prev_ref, x_ref, wgu_ref, bgu_ref, wd_ref, bd_ref, out_hbm, y0, y1, sem_out):
    i = pl.program_id(0)
    rows = y0.shape[0] // SUBLANES
    d_ff = wd_ref.shape[1]
    prev_active = jnp.logical_and(i > 0, act_ref[jnp.maximum(i - 1, 0)] == 1)

    @pl.when(i == 0)
    def _():
        y1[...] = jnp.zeros_like(y1)

    def body(ycur, yprev):
        @pl.when(prev_active)
        def _():
            pltpu.make_async_copy(ycur, out_hbm.at[pl.ds(0, rows * SUBLANES)], sem_out).wait()

        @pl.when(act_ref[i] == 1)
        def _():
            for r in range(rows):
                pltpu.async_copy(yprev.at[pl.ds(r * SUBLANES, SUBLANES)],
                                 out_hbm.at[pl.ds(pl.multiple_of(dst_prev_ref[r], SUBLANES), SUBLANES)],
                                 sem_out, priority=r % DMA_QUEUES)
            x = _load_token_tiles(x_ref).astype(BF16)
            gu = jnp.dot(x, wgu_ref[0], preferred_element_type=F32) + bgu_ref[0]
            glu = jnp.minimum(gu[:, :d_ff], SWIGLU_LIMIT)
            lin = jnp.clip(gu[:, d_ff:], -SWIGLU_LIMIT, SWIGLU_LIMIT)
            hidden = glu * jax.nn.sigmoid(SWIGLU_ALPHA * glu) * (lin + 1.0)
            _store_token_tiles(ycur, jnp.dot(hidden.astype(BF16), wd_ref[0], preferred_element_type=F32) + bd_ref[0])

    @pl.when(i % 2 == 0)
    def _():
        body(y0, y1)

    @pl.when(i % 2 == 1)
    def _():
        body(y1, y0)


def _experts(xs, block_expert, active, dst_prev, wgu, bgu, wd, bd, n_slots, rows):
    n_blocks = block_expert.shape[0]
    d, d_ff2 = wgu.shape[1:]
    assert d == SUBLANES * LANES
    buf = pltpu.VMEM((rows * SUBLANES, LANES), F32)
    grid_spec = pltpu.PrefetchScalarGridSpec(
        num_scalar_prefetch=2,
        grid=(n_blocks,),
        in_specs=[pl.BlockSpec((rows,), lambda i, be, act: (i,), memory_space=pltpu.SMEM),
                  pl.BlockSpec((rows * SUBLANES, LANES), lambda i, be, act: (i, 0)),
                  pl.BlockSpec((1, d, d_ff2), lambda i, be, act: (be[i], 0, 0)),
                  pl.BlockSpec((1, 1, d_ff2), lambda i, be, act: (be[i], 0, 0)),
                  pl.BlockSpec((1, d_ff2 // 2, d), lambda i, be, act: (be[i], 0, 0)),
                  pl.BlockSpec((1, 1, d), lambda i, be, act: (be[i], 0, 0))],
        out_specs=pl.BlockSpec(memory_space=pl.ANY),
        scratch_shapes=[buf, buf, pltpu.SemaphoreType.DMA],
    )
    return pl.pallas_call(
        _expert_kernel,
        grid_spec=grid_spec,
        out_shape=jax.ShapeDtypeStruct(((n_slots + rows) * SUBLANES, LANES), F32),
        compiler_params=_cparams(("arbitrary",)),
    )(block_expert, active, dst_prev, xs, wgu, bgu, wd, bd)


def _routing(top_idx, rank, counts, rows):
    t_total = top_idx.shape[0]
    padded = (counts + rows - 1) // rows * rows
    ends = jnp.cumsum(padded)
    starts = ends - padded
    dest = (starts[top_idx] + rank).reshape(-1)
    n_blocks = (t_total * TOP_K + N_EXPERTS * (rows - 1)) // rows + 2
    n_rows = n_blocks * rows
    n_slots = TOP_K * t_total
    tok = jnp.repeat(jnp.arange(t_total, dtype=jnp.int32), TOP_K)
    slot = jnp.tile(jnp.arange(TOP_K, dtype=jnp.int32), t_total)
    landing = n_slots + jnp.arange(n_rows, dtype=jnp.int32) % rows
    row_dst = landing.at[dest].set(slot * t_total + tok, unique_indices=True)
    dst_prev = jnp.concatenate([landing[:rows], row_dst[:-rows]])
    block_start = jnp.arange(n_blocks, dtype=jnp.int32) * rows
    block_expert = jnp.sum((ends[None, :] <= block_start[:, None]).astype(jnp.int32), axis=1)
    block_expert = jnp.minimum(block_expert, N_EXPERTS - 1)
    has_rows = block_start < ends[-1]
    active = jnp.logical_or(has_rows, jnp.concatenate([has_rows[:1], has_rows[:-1]])).astype(jnp.int32)
    pads = padded - counts
    pad_end = jnp.cumsum(pads)
    seg_first = jnp.concatenate([pad_end - pads, pad_end[-1:]])
    seg_row = jnp.concatenate([starts + counts, ends[-1:]])
    j = jnp.arange(n_rows - n_slots, dtype=jnp.int32)
    seg = jnp.sum((j[:, None] >= pad_end[None, :]).astype(jnp.int32), axis=1)
    pick = seg[:, None] == jnp.arange(N_EXPERTS + 1, dtype=jnp.int32)[None, :]
    fill = j + jnp.sum(jnp.where(pick, (seg_row - seg_first)[None, :], 0), axis=1)
    return block_expert, active, dest * SUBLANES, dst_prev * SUBLANES, fill * SUBLANES, n_rows


def _combine_kernel(x1_ref, g2_ref, gate_ref, *refs):
    y_refs, o_ref = refs[:TOP_K], refs[TOP_K]
    gates = gate_ref[...]
    moe = gates[:, 0:1] * _load_token_tiles(y_refs[0])
    for kk in range(1, TOP_K):
        moe += gates[:, kk:kk + 1] * _load_token_tiles(y_refs[kk])
    o_ref[...] = x1_ref[...] + g2_ref[...] * moe


def _combine(x1, gate2, gates, y4, seq_len, tt):
    t_total, d = x1.shape
    nps = seq_len // tt
    n_tiles = t_total // tt
    slot_specs = [pl.BlockSpec((tt * SUBLANES, LANES), functools.partial(lambda kk, i: (kk * n_tiles + i, 0), kk))
                  for kk in range(TOP_K)]
    return pl.pallas_call(
        _combine_kernel,
        grid=(n_tiles,),
        in_specs=[pl.BlockSpec((tt, d), lambda i: (i, 0)),
                  pl.BlockSpec((None, 1, d), lambda i: (i // nps, 0, 0)),
                  pl.BlockSpec((tt, LANES), lambda i: (i, 0))] + slot_specs,
        out_specs=pl.BlockSpec((tt, d), lambda i: (i, 0)),
        out_shape=jax.ShapeDtypeStruct((t_total, d), F32),
        compiler_params=_cparams(("parallel",)),
    )(x1, gate2, gates, *([y4] * TOP_K))


def _tiles(seq_len):
    token_tile = min(512, seq_len)
    ret_chunk = min(256, seq_len)
    attn_tile = min(512, seq_len)
    expert_rows = 256
    combine_tile = min(256, seq_len)
    return token_tile, ret_chunk, attn_tile, expert_rows, combine_tile


def _trunk(x, mod, lam_init, norm1_w, w_in_b, ret_a_fwd, ret_a_bwd, ret_norm_w, diff_q_norm_w, diff_k_norm_w,
           lam_q1, lam_k1, lam_q2, lam_k2, diff_subln_w, rel_bias, w_out_b, norm2_w, rw_pad, rb_pad,
           wgu_b, b_gate_up, wd_b, b_down):
    batch, seq_len, d = x.shape
    tt, c, t, rows, ct = _tiles(seq_len)
    x2 = x.reshape(batch * seq_len, d)
    shift1, scale1, gate1, shift2, scale2, gate2 = [mod[:, n][:, None, :] for n in range(6)]

    pos = jnp.arange(seq_len, dtype=F32)
    inv_freq = ROPE_BASE ** (-jnp.arange(0, HEAD_DIM, 2, dtype=F32) / HEAD_DIM)
    ang = pos[:, None] * inv_freq[None, :]
    cos_t = jnp.tile(jnp.cos(ang), (1, LANES // (HEAD_DIM // 2)))
    sin_h = jnp.sin(ang)
    sin_t = jnp.tile(jnp.concatenate([-sin_h, sin_h], axis=1), (1, LANES // HEAD_DIM))
    qnw = jnp.tile(diff_q_norm_w.astype(F32), HALF_WIDTH // HEAD_DIM)[None, :]
    knw = jnp.tile(diff_k_norm_w.astype(F32), HALF_WIDTH // HEAD_DIM)[None, :]
    r = jnp.arange(HALF_WIDTH)
    bd = jnp.where((r[:, None] // HEAD_DIM) == (r[None, :] // HEAD_DIM), 1.0 / HEAD_DIM, 0.0).astype(BF16)

    rq, rk, rv, rg, dq, dk, dv = _inproj(x2, scale1, shift1, norm1_w[None, :], w_in_b, cos_t, sin_t, qnw, knw, bd,
                                         seq_len, tt)

    tabs = _retention_tables(ret_a_fwd, ret_a_bwd, c)
    ret_out = _retention(rq, rk, rv, rg, tabs, ret_norm_w.reshape(1, HALF_WIDTH).astype(F32), batch, seq_len, c)

    lam = (jnp.exp(jnp.sum(lam_q1.astype(F32) * lam_k1.astype(F32)))
           - jnp.exp(jnp.sum(lam_q2.astype(F32) * lam_k2.astype(F32))) + lam_init)
    subln = (diff_subln_w.astype(F32) * (1.0 - lam_init))[None, :]
    diff_out = _attention(dq, dk, dv, _bias_tables(rel_bias, t), lam[None].astype(F32), subln, batch, seq_len, t)

    x1, h2, idx_pad, gates_pad, counts = _outproj(ret_out, diff_out, x2, gate1, scale2, shift2, norm2_w[None, :],
                                                  w_out_b, rw_pad, rb_pad, seq_len, tt)
    block_expert, active, dest, dst_prev, fill, n_rows = _routing(idx_pad[:, :TOP_K], idx_pad[:, TOP_K:2 * TOP_K],
                                                                  counts[0, :N_EXPERTS], rows)
    xs = _dispatch(h2, dest, fill, n_rows, ct)
    y4 = _experts(xs, block_expert, active, dst_prev, wgu_b, b_gate_up, wd_b, b_down, TOP_K * batch * seq_len, rows)
    y = _combine(x1, gate2, gates_pad, y4, seq_len, ct)
    return y.reshape(batch, seq_len, d)


def kernel(x_prompt, x_sample, c_prompt, c_sample, w_ada, b_ada, norm1_w, w_in, ret_a_fwd, ret_a_bwd, ret_norm_w,
           diff_q_norm_w, diff_k_norm_w, lam_q1, lam_k1, lam_q2, lam_k2, diff_subln_w, rel_bias, w_out, norm2_w,
           router_w, router_b, w_gate_up, b_gate_up, w_down, b_down):
    depth = w_ada.shape[0]
    d = x_prompt.shape[-1]
    n_prompt = c_prompt.shape[0]
    n_cond = n_prompt + c_sample.shape[0]
    c_all = jnp.concatenate([c_prompt, c_sample], axis=0).astype(F32)
    c_pad = jnp.pad(c_all, ((0, -n_cond % 8), (0, 0)))
    xs = [x_prompt, x_sample]
    for l in range(depth):
        lam_init = 0.8 - 0.6 * math.exp(-0.3 * l)
        mod = _ada(c_pad, w_ada[l], b_ada[l])[:n_cond].reshape(n_cond, 6, d)
        rw = jnp.pad(router_w[l].astype(F32), ((0, 0), (0, LANES - N_EXPERTS)))
        rw_hi = rw.astype(BF16)
        rw_pad = jnp.stack([rw_hi, (rw - rw_hi.astype(F32)).astype(BF16)])
        rb_pad = jnp.pad(router_b[l].astype(F32), (0, LANES - N_EXPERTS), constant_values=-jnp.inf)[None, :]
        shared = (norm1_w[l], w_in[l].astype(BF16), ret_a_fwd[l], ret_a_bwd[l], ret_norm_w[l], diff_q_norm_w[l],
                  diff_k_norm_w[l], lam_q1[l], lam_k1[l], lam_q2[l], lam_k2[l], diff_subln_w[l], rel_bias,
                  w_out[l].astype(BF16), norm2_w[l], rw_pad, rb_pad,
                  w_gate_up[l].astype(BF16), b_gate_up[l][:, None, :], w_down[l].astype(BF16), b_down[l][:, None, :])
        xs = [_trunk(xs[0], mod[:n_prompt], lam_init, *shared),
              _trunk(xs[1], mod[n_prompt:], lam_init, *shared)]
    return (xs[0], xs[1])
```

```python
import functools
import math

import jax
import jax.numpy as jnp
import numpy as np
from jax import lax
from jax.experimental import pallas as pl
from jax.experimental.pallas import tpu as pltpu

F32 = jnp.float32
BF16 = jnp.bfloat16
HIGHEST = lax.Precision.HIGHEST

LANES = 128
SUBLANES = 8
HEAD_DIM = 64
RET_HEADS = 8
DIFF_HEADS = 4
HALF_WIDTH = 512
N_PAIRS = HALF_WIDTH // LANES
ROPE_BASE = 10000.0
NUM_BUCKETS = 32
MAX_DISTANCE = 128
N_EXPERTS = 32
TOP_K = 4
SWIGLU_LIMIT = 7.0
SWIGLU_ALPHA = 1.702
EPS = 1e-6
LOG2E = math.log2(math.e)
ATTN_ROWS = 128
ATTN_UNROLL = 16
VMEM_LIMIT = 56 * 1024 * 1024
DMA_QUEUES = 2


def _cparams(sem):
    return pltpu.CompilerParams(dimension_semantics=sem, vmem_limit_bytes=VMEM_LIMIT)


def _ada_kernel(c_ref, w_ref, b_ref, o_ref):
    c = c_ref[...]
    a = c * jax.nn.sigmoid(c)
    o_ref[...] = jnp.dot(a, w_ref[...], preferred_element_type=F32, precision=HIGHEST) + b_ref[...]


def _ada(c_pad, w_ada, b_ada):
    rows, d = c_pad.shape
    n = w_ada.shape[1]
    tn = d
    return pl.pallas_call(
        _ada_kernel,
        grid=(n // tn,),
        in_specs=[pl.BlockSpec((rows, d), lambda j: (0, 0)),
                  pl.BlockSpec((d, tn), lambda j: (0, j)),
                  pl.BlockSpec((1, tn), lambda j: (0, j))],
        out_specs=pl.BlockSpec((rows, tn), lambda j: (0, j)),
        out_shape=jax.ShapeDtypeStruct((rows, n), F32),
        compiler_params=_cparams(("parallel",)),
    )(c_pad, w_ada, b_ada.reshape(1, n))


def _rotate_half(xg):
    lane = lax.broadcasted_iota(jnp.int32, xg.shape, 1)
    first = (lane % HEAD_DIM) < (HEAD_DIM // 2)
    return jnp.where(first, pltpu.roll(xg, LANES - HEAD_DIM // 2, 1), pltpu.roll(xg, HEAD_DIM // 2, 1))


def _inproj_kernel(x_ref, sc_ref, sh_ref, nw_ref, w_ref, cos_ref, sin_ref, qnw_ref, knw_ref, bd_ref,
                   rq_ref, rk_ref, rv_ref, rg_ref, dq_ref, dk_ref, dv_ref):
    x = x_ref[...]
    ms = jnp.mean(x * x, axis=-1, keepdims=True)
    h = x * lax.rsqrt(ms + EPS) * nw_ref[...]
    h = (h * (1.0 + sc_ref[...]) + sh_ref[...]).astype(BF16)

    def piece(n):
        return jnp.dot(h, w_ref[:, n * HALF_WIDTH:(n + 1) * HALF_WIDTH], preferred_element_type=F32)

    cos = cos_ref[...]
    sin = sin_ref[...]

    def rotary(p, out_ref, scale):
        for g in range(N_PAIRS):
            xg = p[:, g * LANES:(g + 1) * LANES]
            y = xg * cos + _rotate_half(xg) * sin
            out_ref[:, g * LANES:(g + 1) * LANES] = (y * scale).astype(BF16)

    def head_norm(p, w, out_ref, scale):
        msq = jnp.dot((p * p).astype(BF16), bd_ref[...], preferred_element_type=F32)
        out_ref[...] = (p * lax.rsqrt(msq + EPS) * w * scale).astype(BF16)

    rotary(piece(0), rq_ref, 1.0)
    rotary(piece(1), rk_ref, HEAD_DIM ** -0.5)
    rv_ref[...] = piece(2).astype(BF16)
    rg_ref[...] = piece(3).astype(BF16)
    head_norm(piece(4), qnw_ref[...], dq_ref, HEAD_DIM ** -0.5 * LOG2E)
    head_norm(piece(5), knw_ref[...], dk_ref, 1.0)
    dv = piece(6).astype(BF16)
    ones = jnp.ones((dv.shape[0], LANES), BF16)
    for hd in range(DIFF_HEADS):
        dv_ref[:, 2 * hd * LANES:(2 * hd + 1) * LANES] = dv[:, hd * LANES:(hd + 1) * LANES]
        dv_ref[:, (2 * hd + 1) * LANES:(2 * hd + 2) * LANES] = ones


def _inproj(x2, scale1, shift1, norm_w, w_in, cos_t, sin_t, qnw, knw, bd, seq_len, tt):
    t_total, d = x2.shape
    nps = seq_len // tt
    n_in = w_in.shape[1]
    row = lambda i: (i, 0)
    mod = lambda i: (i // nps, 0, 0)
    pos = lambda i: (i % nps, 0)
    fixed = lambda i: (0, 0)
    out_sd = jax.ShapeDtypeStruct((t_total, HALF_WIDTH), BF16)
    return pl.pallas_call(
        _inproj_kernel,
        grid=(t_total // tt,),
        in_specs=[pl.BlockSpec((tt, d), row),
                  pl.BlockSpec((None, 1, d), mod),
                  pl.BlockSpec((None, 1, d), mod),
                  pl.BlockSpec((1, d), fixed),
                  pl.BlockSpec((d, n_in), fixed),
                  pl.BlockSpec((tt, LANES), pos),
                  pl.BlockSpec((tt, LANES), pos),
                  pl.BlockSpec((1, HALF_WIDTH), fixed),
                  pl.BlockSpec((1, HALF_WIDTH), fixed),
                  pl.BlockSpec((HALF_WIDTH, HALF_WIDTH), fixed)],
        out_specs=[pl.BlockSpec((tt, HALF_WIDTH), row)] * 6 + [pl.BlockSpec((tt, 2 * HALF_WIDTH), row)],
        out_shape=[out_sd] * 6 + [jax.ShapeDtypeStruct((t_total, 2 * HALF_WIDTH), BF16)],
        compiler_params=_cparams(("parallel",)),
    )(x2, scale1, shift1, norm_w, w_in, cos_t, sin_t, qnw, knw, bd)


def _kv_update(state_ref, p, k, v, kdec, cdec, bdmask):
    kd = (k.astype(F32) * kdec).astype(BF16)
    kv = lax.dot_general(kd, v, (((0,), (0,)), ((), ())), preferred_element_type=F32)
    state_ref[p] = state_ref[p] * cdec + kv * bdmask


def _ret_state_kernel(k_ref, v_ref, kdec_ref, cdec_ref, bdmask_ref, sb_ref, state_ref):
    @pl.when(pl.program_id(1) == 0)
    def _():
        state_ref[...] = jnp.zeros_like(state_ref)

    sb_ref[...] = state_ref[...]
    for p in range(N_PAIRS):
        sl = slice(p * LANES, (p + 1) * LANES)
        _kv_update(state_ref, p, k_ref[:, sl], v_ref[:, sl], kdec_ref[:, sl], cdec_ref[:, sl], bdmask_ref[...])


def _ret_main_kernel(q_ref, k_ref, v_ref, g_ref, dmat_ref, qdf_ref, qdb_ref, kdf_ref, cdf_ref, bdmask_ref,
                     nw_ref, sb_ref, o_ref, state_ref):
    @pl.when(pl.program_id(1) == 0)
    def _():
        state_ref[...] = jnp.zeros_like(state_ref)

    c = q_ref.shape[0]
    lane = lax.broadcasted_iota(jnp.int32, (c, LANES), 1)
    lo = lane < HEAD_DIM
    for p in range(N_PAIRS):
        sl = slice(p * LANES, (p + 1) * LANES)
        q = q_ref[:, sl]
        k = k_ref[:, sl]
        v = v_ref[:, sl]
        qf = q.astype(F32)
        acc = jnp.dot((qf * qdf_ref[:, sl]).astype(BF16), state_ref[p].astype(BF16), preferred_element_type=F32)
        acc += jnp.dot((qf * qdb_ref[:, sl]).astype(BF16), sb_ref[p].astype(BF16), preferred_element_type=F32)
        for hh in range(2):
            sel = lo if hh == 0 else jnp.logical_not(lo)
            qm = jnp.where(sel, q, jnp.zeros_like(q))
            vm = jnp.where(sel, v, jnp.zeros_like(v))
            s = lax.dot_general(qm, k, (((1,), (1,)), ((), ())), preferred_element_type=F32)
            w = (s * dmat_ref[2 * p + hh]).astype(BF16)
            acc += jnp.dot(w, vm, preferred_element_type=F32)
        _kv_update(state_ref, p, k, v, kdf_ref[:, sl], cdf_ref[:, sl], bdmask_ref[...])
        sq = acc * acc
        ms_lo = jnp.sum(jnp.where(lo, sq, 0.0), axis=-1, keepdims=True)
        ms_hi = jnp.sum(jnp.where(lo, 0.0, sq), axis=-1, keepdims=True)
        ms = jnp.where(lo, ms_lo, ms_hi) * (1.0 / HEAD_DIM)
        y = acc * lax.rsqrt(ms + EPS) * nw_ref[:, sl]
        gf = g_ref[:, sl].astype(F32)
        o_ref[:, sl] = (gf * jax.nn.sigmoid(gf) * y).astype(BF16)


def _retention(rq, rk, rv, rg, tabs, ret_nw, batch, seq_len, c):
    t_total = rq.shape[0]
    nc = seq_len // c
    dmat, qdf, qdb, kdf, kdb, cdf, cdb, bdmask = tabs
    fixed2 = lambda b, i: (0, 0)
    rev = lambda b, i: (b * nc + nc - 1 - i, 0)
    fwd = lambda b, i: (b * nc + i, 0)
    tile = pl.BlockSpec((c, HALF_WIDTH), fwd)
    tile_rev = pl.BlockSpec((c, HALF_WIDTH), rev)
    tab = pl.BlockSpec((c, HALF_WIDTH), fixed2)
    vec = pl.BlockSpec((1, HALF_WIDTH), fixed2)
    mask = pl.BlockSpec((LANES, LANES), fixed2)
    state = pltpu.VMEM((N_PAIRS, LANES, LANES), F32)
    sb = pl.pallas_call(
        _ret_state_kernel,
        grid=(batch, nc),
        in_specs=[tile_rev, tile_rev, tab, vec, mask],
        out_specs=pl.BlockSpec((None, None, N_PAIRS, LANES, LANES), lambda b, i: (b, nc - 1 - i, 0, 0, 0)),
        out_shape=jax.ShapeDtypeStruct((batch, nc, N_PAIRS, LANES, LANES), F32),
        scratch_shapes=[state],
        compiler_params=_cparams(("parallel", "arbitrary")),
    )(rk, rv, kdb, cdb, bdmask)
    return pl.pallas_call(
        _ret_main_kernel,
        grid=(batch, nc),
        in_specs=[tile, tile, tile, tile,
                  pl.BlockSpec((RET_HEADS, c, c), lambda b, i: (0, 0, 0)),
                  tab, tab, tab, vec, mask, vec,
                  pl.BlockSpec((None, None, N_PAIRS, LANES, LANES), lambda b, i: (b, i, 0, 0, 0))],
        out_specs=tile,
        out_shape=jax.ShapeDtypeStruct((t_total, HALF_WIDTH), BF16),
        scratch_shapes=[state],
        compiler_params=_cparams(("parallel", "arbitrary")),
    )(rq, rk, rv, rg, dmat, qdf, qdb, kdf, cdf, bdmask, ret_nw, sb)


def _retention_tables(a_fwd, a_bwd, c):
    lg_f = jnp.log1p(-jnp.exp(a_fwd.astype(F32)))
    lg_b = jnp.log1p(-jnp.exp(a_bwd.astype(F32)))
    pos = jnp.arange(c, dtype=F32)
    diff = pos[:, None] - pos[None, :]
    dmat = jnp.where(diff[None] >= 0,
                     jnp.exp(jnp.maximum(diff, 0.0)[None] * lg_f[:, None, None]),
                     jnp.exp(jnp.maximum(-diff, 0.0)[None] * lg_b[:, None, None]))
    lane_f = jnp.repeat(lg_f, HEAD_DIM)[None, :]
    lane_b = jnp.repeat(lg_b, HEAD_DIM)[None, :]
    qdf = jnp.exp((pos + 1.0)[:, None] * lane_f)
    kdf = jnp.exp((c - 1.0 - pos)[:, None] * lane_f)
    qdb = jnp.exp((c - pos)[:, None] * lane_b)
    kdb = jnp.exp(pos[:, None] * lane_b)
    cdf = jnp.exp(c * lane_f)
    cdb = jnp.exp(c * lane_b)
    r = jnp.arange(LANES)
    bdmask = ((r[:, None] // HEAD_DIM) == (r[None, :] // HEAD_DIM)).astype(F32)
    return dmat, qdf, qdb, kdf, kdb, cdf, cdb, bdmask


def _attn_kernel(lam_ref, q_ref, k_ref, v_ref, bias_ref, sw_ref, o_ref, qq_ref, m_ref, acc_ref):
    i = pl.program_id(2)
    t = q_ref.shape[0]
    nk = k_ref.shape[0] // t
    q = q_ref[...]
    lane = lax.broadcasted_iota(jnp.int32, q.shape, 1)
    zero = jnp.zeros_like(q)
    qq_ref[...] = jnp.concatenate([jnp.where(lane < HEAD_DIM, q, zero), jnp.where(lane < HEAD_DIM, zero, q)], axis=0)
    m_ref[...] = jnp.full_like(m_ref, -jnp.inf)
    acc_ref[...] = jnp.zeros_like(acc_ref)

    def tile(j):
        keys = pl.ds(pl.multiple_of(j * t, t), t)
        which = jnp.clip(j - i, -2, 2) + 2
        for r in range(0, 2 * t, ATTN_ROWS):
            rows = slice(r, r + ATTN_ROWS)
            s = lax.dot_general(qq_ref[rows, :], k_ref[keys, :], (((1,), (1,)), ((), ())),
                                preferred_element_type=F32)
            s = s + bias_ref[which, r % t:r % t + ATTN_ROWS, :]
            m_prev = m_ref[rows, :]
            m_new = jnp.maximum(m_prev, jnp.max(s, axis=1, keepdims=True))
            alpha = jnp.exp2(m_prev - m_new)
            p = jnp.exp2(s - jnp.concatenate([m_new] * (t // LANES), axis=1)).astype(BF16)
            pv = jnp.dot(p, v_ref[keys, :], preferred_element_type=F32)
            acc_ref[rows, :] = jnp.concatenate([alpha, alpha], axis=1) * acc_ref[rows, :] + pv
            m_ref[rows, :] = m_new

    def step(jj, carry):
        for u in range(ATTN_UNROLL):
            tile(jj * ATTN_UNROLL + u)
        return carry

    lax.fori_loop(0, nk // ATTN_UNROLL, step, 0)

    acc = acc_ref[...]
    o = acc[:, :LANES] / acc[:, LANES:]
    att = o[:t] - lam_ref[0] * o[t:]
    msq = jnp.mean(att * att, axis=-1, keepdims=True)
    o_ref[...] = (att * lax.rsqrt(msq + EPS) * sw_ref[...]).astype(BF16)


def _attention(dq, dk, dv, bias_tiles, lam, subln_w, batch, seq_len, t):
    t_total = dq.shape[0]
    nq = seq_len // t
    assert nq % ATTN_UNROLL == 0 and t % ATTN_ROWS == 0
    grid_spec = pltpu.PrefetchScalarGridSpec(
        num_scalar_prefetch=1,
        grid=(batch, DIFF_HEADS, nq),
        in_specs=[pl.BlockSpec((t, LANES), lambda b, h, i, s: (b * nq + i, h)),
                  pl.BlockSpec((seq_len, LANES), lambda b, h, i, s: (b, h)),
                  pl.BlockSpec((seq_len, 2 * LANES), lambda b, h, i, s: (b, h)),
                  pl.BlockSpec((None, 5, t, t), lambda b, h, i, s: (h, 0, 0, 0)),
                  pl.BlockSpec((1, LANES), lambda b, h, i, s: (0, 0))],
        out_specs=pl.BlockSpec((t, LANES), lambda b, h, i, s: (b * nq + i, h)),
        scratch_shapes=[pltpu.VMEM((2 * t, LANES), BF16), pltpu.VMEM((2 * t, LANES), F32),
                        pltpu.VMEM((2 * t, 2 * LANES), F32)],
    )
    return pl.pallas_call(
        _attn_kernel,
        grid_spec=grid_spec,
        out_shape=jax.ShapeDtypeStruct((t_total, HALF_WIDTH), BF16),
        compiler_params=_cparams(("parallel", "parallel", "parallel")),
    )(lam, dq, dk, dv, bias_tiles, subln_w)


def _t5_bucket(rel):
    nb = NUM_BUCKETS // 2
    max_exact = nb // 2
    n = jnp.abs(rel)
    base = jnp.where(rel > 0, nb, 0)
    nf = jnp.maximum(n, 1).astype(F32)
    large = max_exact + (jnp.log(nf / max_exact) / math.log(MAX_DISTANCE / max_exact)
                         * (nb - max_exact)).astype(jnp.int32)
    large = jnp.minimum(large, nb - 1)
    return base + jnp.where(n < max_exact, n, large)


def _bias_tables(rel_bias, t):
    assert t >= MAX_DISTANCE
    table = rel_bias.astype(F32) * LOG2E
    qq = jnp.arange(t, dtype=jnp.int32)[:, None]
    kk = jnp.arange(t, dtype=jnp.int32)[None, :]
    bucket = jnp.stack([_t5_bucket(d * t + kk - qq) for d in (-2, -1, 0, 1, 2)], axis=0)
    onehot = (bucket[..., None] == jnp.arange(NUM_BUCKETS, dtype=jnp.int32)).astype(F32)
    return jnp.einsum('dqkn,nh->hdqk', onehot, table, precision=HIGHEST)


def _store_token_tiles(ref, value):
    n = value.shape[0]
    for s in range(SUBLANES):
        ref[pl.ds(s, n, stride=SUBLANES), :] = value[:, s * LANES:(s + 1) * LANES]


def _load_token_tiles(ref):
    n = ref.shape[0] // SUBLANES
    return jnp.concatenate([ref[pl.ds(s, n, stride=SUBLANES), :] for s in range(SUBLANES)], axis=1)


def _outproj_kernel(ret_ref, dif_ref, x_ref, g1_ref, sc2_ref, sh2_ref, nw2_ref, wo_ref, rw_ref, rb_ref, tri_ref,
                    x1_ref, h2_ref, idx_ref, gate_ref, count_ref, cnt_ref):
    mix = jnp.dot(ret_ref[...], wo_ref[:HALF_WIDTH, :], preferred_element_type=F32)
    mix += jnp.dot(dif_ref[...], wo_ref[HALF_WIDTH:, :], preferred_element_type=F32)
    x1 = x_ref[...] + g1_ref[...] * mix
    x1_ref[...] = x1
    ms = jnp.mean(x1 * x1, axis=-1, keepdims=True)
    h2 = x1 * lax.rsqrt(ms + EPS) * nw2_ref[...]
    h2 = h2 * (1.0 + sc2_ref[...]) + sh2_ref[...]
    _store_token_tiles(h2_ref, h2)
    h2_hi = h2.astype(BF16)
    h2_lo = (h2 - h2_hi.astype(F32)).astype(BF16)
    logits = (jnp.dot(h2_hi, rw_ref[0], preferred_element_type=F32)
              + jnp.dot(h2_lo, rw_ref[0], preferred_element_type=F32)
              + jnp.dot(h2_hi, rw_ref[1], preferred_element_type=F32)) + rb_ref[...]
    lane = lax.broadcasted_iota(jnp.int32, logits.shape, 1)
    vals, idxs = [], []
    for _ in range(TOP_K):
        m = jnp.max(logits, axis=-1, keepdims=True)
        am = jnp.min(jnp.where(logits == m, lane, LANES), axis=-1, keepdims=True)
        vals.append(m)
        idxs.append(am)
        logits = jnp.where(lane == am, -jnp.inf, logits)
    es = [jnp.exp(v - vals[0]) for v in vals]
    den = es[0] + es[1] + es[2] + es[3]
    @pl.when(pl.program_id(0) == 0)
    def _():
        cnt_ref[...] = jnp.zeros_like(cnt_ref)

    member = jnp.zeros(lane.shape, F32)
    for kk in range(TOP_K):
        member = jnp.where(lane == idxs[kk], 1.0, member)
    before = jnp.dot(tri_ref[...], member.astype(BF16), preferred_element_type=F32) + cnt_ref[...]
    cnt_ref[...] += jnp.sum(member, axis=0, keepdims=True)
    count_ref[...] = jnp.broadcast_to(cnt_ref[...], count_ref.shape).astype(jnp.int32)
    idx_out = jnp.zeros(lane.shape, jnp.int32)
    gate_out = jnp.zeros(lane.shape, F32)
    for kk in range(TOP_K):
        rank = jnp.sum(jnp.where(lane == idxs[kk], before, 0.0), axis=-1, keepdims=True).astype(jnp.int32)
        idx_out = jnp.where(lane == kk, idxs[kk], idx_out)
        idx_out = jnp.where(lane == TOP_K + kk, rank, idx_out)
        gate_out = jnp.where(lane == kk, es[kk] / den, gate_out)
    idx_ref[...] = idx_out
    gate_ref[...] = gate_out


def _outproj(ret_out, diff_out, x2, gate1, scale2, shift2, norm2_w, w_out, rw_pad, rb_pad, seq_len, tt):
    t_total, d = x2.shape
    nps = seq_len // tt
    row = lambda i: (i, 0)
    mod = lambda i: (i // nps, 0, 0)
    fixed = lambda i: (0, 0)
    pos = jnp.arange(tt)
    tri = (pos[None, :] < pos[:, None]).astype(BF16)
    return pl.pallas_call(
        _outproj_kernel,
        grid=(t_total // tt,),
        in_specs=[pl.BlockSpec((tt, HALF_WIDTH), row),
                  pl.BlockSpec((tt, HALF_WIDTH), row),
                  pl.BlockSpec((tt, d), row),
                  pl.BlockSpec((None, 1, d), mod),
                  pl.BlockSpec((None, 1, d), mod),
                  pl.BlockSpec((None, 1, d), mod),
                  pl.BlockSpec((1, d), fixed),
                  pl.BlockSpec((2 * HALF_WIDTH, d), fixed),
                  pl.BlockSpec((2, d, LANES), lambda i: (0, 0, 0)),
                  pl.BlockSpec((1, LANES), fixed),
                  pl.BlockSpec((tt, tt), fixed)],
        out_specs=[pl.BlockSpec((tt, d), row), pl.BlockSpec((tt * SUBLANES, LANES), row),
                   pl.BlockSpec((tt, LANES), row), pl.BlockSpec((tt, LANES), row),
                   pl.BlockSpec((SUBLANES, LANES), fixed)],
        out_shape=[jax.ShapeDtypeStruct((t_total, d), F32), jax.ShapeDtypeStruct((t_total * SUBLANES, LANES), F32),
                   jax.ShapeDtypeStruct((t_total, LANES), jnp.int32),
                   jax.ShapeDtypeStruct((t_total, LANES), F32),
                   jax.ShapeDtypeStruct((SUBLANES, LANES), jnp.int32)],
        scratch_shapes=[pltpu.VMEM((1, LANES), F32)],
        compiler_params=_cparams(("arbitrary",)),
    )(ret_out, diff_out, x2, gate1, scale2, shift2, norm2_w, w_out, rw_pad, rb_pad, tri)


def _dispatch_kernel(fill_ref, dest_ref, h2_ref, xs_hbm, zero_ref, sem):
    i = pl.program_id(0)
    tokens = h2_ref.shape[0] // SUBLANES
    for t in range(tokens):
        for kk in range(TOP_K):
            pltpu.async_copy(h2_ref.at[pl.ds(t * SUBLANES, SUBLANES)],
                             xs_hbm.at[pl.ds(pl.multiple_of(dest_ref[t * TOP_K + kk], SUBLANES), SUBLANES)],
                             sem, priority=kk % DMA_QUEUES)
    for kk in range(TOP_K):
        pltpu.make_async_copy(h2_ref, xs_hbm.at[pl.ds(0, tokens * SUBLANES)], sem).wait()

    @pl.when(i == pl.num_programs(0) - 1)
    def _():
        zero_ref[...] = jnp.zeros_like(zero_ref)

        def fill(n, carry):
            pltpu.make_async_copy(zero_ref, xs_hbm.at[pl.ds(pl.multiple_of(fill_ref[n], SUBLANES), SUBLANES)],
                                  sem).start()
            return carry

        lax.fori_loop(0, fill_ref.shape[0], fill, 0)
        for _ in range(fill_ref.shape[0] // tokens):
            pltpu.make_async_copy(h2_ref, xs_hbm.at[pl.ds(0, tokens * SUBLANES)], sem).wait()


def _dispatch(h2, dest, fill_rows, n_rows, tt):
    t_total = h2.shape[0] // SUBLANES
    assert fill_rows.shape[0] % tt == 0
    grid_spec = pltpu.PrefetchScalarGridSpec(
        num_scalar_prefetch=1,
        grid=(t_total // tt,),
        in_specs=[pl.BlockSpec((tt * TOP_K,), lambda i, fill: (i,), memory_space=pltpu.SMEM),
                  pl.BlockSpec((tt * SUBLANES, LANES), lambda i, fill: (i, 0))],
        out_specs=pl.BlockSpec(memory_space=pl.ANY),
        scratch_shapes=[pltpu.VMEM((SUBLANES, LANES), F32), pltpu.SemaphoreType.DMA],
    )
    return pl.pallas_call(
        _dispatch_kernel,
        grid_spec=grid_spec,
        out_shape=jax.ShapeDtypeStruct((n_rows * SUBLANES, LANES), F32),
        compiler_params=_cparams(("arbitrary",)),
    )(fill_rows, dest, h2)


def _expert_kernel(be_ref, act_ref, dst_prev_ref, x_ref, wgu_ref, bgu_ref, wd_ref, bd_ref, out_hbm, y0, y1, sem_out):
    i = pl.program_id(0)
    rows = y0.shape[0] // SUBLANES
    d_ff = wd_ref.shape[1]
    prev_active = jnp.logical_and(i > 0, act_ref[jnp.maximum(i - 1, 0)] == 1)

    @pl.when(i == 0)
    def _():
        y1[...] = jnp.zeros_like(y1)

    def body(ycur, yprev):
        @pl.when(prev_active)
        def _():
            pltpu.make_async_copy(ycur, out_hbm.at[pl.ds(0, rows * SUBLANES)], sem_out).wait()

        @pl.when(act_ref[i] == 1)
        def _():
            for r in range(rows):
                pltpu.async_copy(yprev.at[pl.ds(r * SUBLANES, SUBLANES)],
                                 out_hbm.at[pl.ds(pl.multiple_of(dst_prev_ref[r], SUBLANES), SUBLANES)],
                                 sem_out, priority=r % DMA_QUEUES)
            x = _load_token_tiles(x_ref).astype(BF16)
            gu = jnp.dot(x, wgu_ref[0], preferred_element_type=F32) + bgu_ref[0]
            glu = jnp.minimum(gu[:, :d_ff], SWIGLU_LIMIT)
            lin = jnp.clip(gu[:, d_ff:], -SWIGLU_LIMIT, SWIGLU_LIMIT)
            hidden = glu * jax.nn.sigmoid(SWIGLU_ALPHA * glu) * (lin + 1.0)
            _store_token_tiles(ycur, jnp.dot(hidden.astype(BF16), wd_ref[0], preferred_element_type=F32) + bd_ref[0])

    @pl.when(i % 2 == 0)
    def _():
        body(y0, y1)

    @pl.when(i % 2 == 1)
    def _():
        body(y1, y0)


def _experts(xs, block_expert, active, dst_prev, wgu, bgu, wd, bd, n_slots, rows):
    n_blocks = block_expert.shape[0]
    d, d_ff2 = wgu.shape[1:]
    assert d == SUBLANES * LANES
    buf = pltpu.VMEM((rows * SUBLANES, LANES), F32)
    grid_spec = pltpu.PrefetchScalarGridSpec(
        num_scalar_prefetch=2,
        grid=(n_blocks,),
        in_specs=[pl.BlockSpec((rows,), lambda i, be, act: (i,), memory_space=pltpu.SMEM),
                  pl.BlockSpec((rows * SUBLANES, LANES), lambda i, be, act: (i, 0)),
                  pl.BlockSpec((1, d, d_ff2), lambda i, be, act: (be[i], 0, 0)),
                  pl.BlockSpec((1, 1, d_ff2), lambda i, be, act: (be[i], 0, 0)),
                  pl.BlockSpec((1, d_ff2 // 2, d), lambda i, be, act: (be[i], 0, 0)),
                  pl.BlockSpec((1, 1, d), lambda i, be, act: (be[i], 0, 0))],
        out_specs=pl.BlockSpec(memory_space=pl.ANY),
        scratch_shapes=[buf, buf, pltpu.SemaphoreType.DMA],
    )
    return pl.pallas_call(
        _expert_kernel,
        grid_spec=grid_spec,
        out_shape=jax.ShapeDtypeStruct(((n_slots + rows) * SUBLANES, LANES), F32),
        compiler_params=_cparams(("arbitrary",)),
    )(block_expert, active, dst_prev, xs, wgu, bgu, wd, bd)


def _routing(top_idx, rank, counts, rows):
    t_total = top_idx.shape[0]
    padded = (counts + rows - 1) // rows * rows
    ends = jnp.cumsum(padded)
    starts = ends - padded
    dest = (starts[top_idx] + rank).reshape(-1)
    n_blocks = (t_total * TOP_K + N_EXPERTS * (rows - 1)) // rows + 2
    n_rows = n_blocks * rows
    n_slots = TOP_K * t_total
    tok = jnp.repeat(jnp.arange(t_total, dtype=jnp.int32), TOP_K)
    slot = jnp.tile(jnp.arange(TOP_K, dtype=jnp.int32), t_total)
    block_start = jnp.arange(n_blocks, dtype=jnp.int32) * rows
    block_expert = jnp.sum((ends[None, :] <= block_start[:, None]).astype(jnp.int32), axis=1)
    block_expert = jnp.minimum(block_expert, N_EXPERTS - 1)
    has_rows = block_start < ends[-1]
    active = jnp.logical_or(has_rows, jnp.concatenate([has_rows[:1], has_rows[:-1]])).astype(jnp.int32)
    pads = padded - counts
    pad_end = jnp.cumsum(pads)
    seg_first = jnp.concatenate([pad_end - pads, pad_end[-1:]])
    seg_row = jnp.concatenate([starts + counts, ends[-1:]])
    j = jnp.arange(n_rows - n_slots, dtype=jnp.int32)
    seg = jnp.sum((j[:, None] >= pad_end[None, :]).astype(jnp.int32), axis=1)
    pick = seg[:, None] == jnp.arange(N_EXPERTS + 1, dtype=jnp.int32)[None, :]
    fill = j + jnp.sum(jnp.where(pick, (seg_row - seg_first)[None, :], 0), axis=1)
    _, row_dst = lax.sort_key_val(jnp.concatenate([dest, fill]),
                                  jnp.concatenate([slot * t_total + tok, n_slots + fill % rows]))
    dst_prev = jnp.concatenate([n_slots + jnp.arange(rows, dtype=jnp.int32), row_dst[:-rows]])
    return block_expert, active, dest * SUBLANES, dst_prev * SUBLANES, fill * SUBLANES, n_rows


def _combine_kernel(x1_ref, g2_ref, gate_ref, *refs):
    y_refs, o_ref = refs[:TOP_K], refs[TOP_K]
    gates = gate_ref[...]
    moe = gates[:, 0:1] * _load_token_tiles(y_refs[0])
    for kk in range(1, TOP_K):
        moe += gates[:, kk:kk + 1] * _load_token_tiles(y_refs[kk])
    o_ref[...] = x1_ref[...] + g2_ref[...] * moe


def _combine(x1, gate2, gates, y4, seq_len, tt):
    t_total, d = x1.shape
    nps = seq_len // tt
    n_tiles = t_total // tt
    slot_specs = [pl.BlockSpec((tt * SUBLANES, LANES), functools.partial(lambda kk, i: (kk * n_tiles + i, 0), kk))
                  for kk in range(TOP_K)]
    return pl.pallas_call(
        _combine_kernel,
        grid=(n_tiles,),
        in_specs=[pl.BlockSpec((tt, d), lambda i: (i, 0)),
                  pl.BlockSpec((None, 1, d), lambda i: (i // nps, 0, 0)),
                  pl.BlockSpec((tt, LANES), lambda i: (i, 0))] + slot_specs,
        out_specs=pl.BlockSpec((tt, d), lambda i: (i, 0)),
        out_shape=jax.ShapeDtypeStruct((t_total, d), F32),
        compiler_params=_cparams(("parallel",)),
    )(x1, gate2, gates, *([y4] * TOP_K))


def _tiles(seq_len):
    token_tile = min(512, seq_len)
    ret_chunk = min(256, seq_len)
    attn_tile = min(512, seq_len)
    expert_rows = 256
    combine_tile = min(256, seq_len)
    return token_tile, ret_chunk, attn_tile, expert_rows, combine_tile


def _trunk(x, mod, lam_init, norm1_w, w_in_b, ret_a_fwd, ret_a_bwd, ret_norm_w, diff_q_norm_w, diff_k_norm_w,
           lam_q1, lam_k1, lam_q2, lam_k2, diff_subln_w, rel_bias, w_out_b, norm2_w, rw_pad, rb_pad,
           wgu_b, b_gate_up, wd_b, b_down):
    batch, seq_len, d = x.shape
    tt, c, t, rows, ct = _tiles(seq_len)
    x2 = x.reshape(batch * seq_len, d)
    shift1, scale1, gate1, shift2, scale2, gate2 = [mod[:, n][:, None, :] for n in range(6)]

    pos = jnp.arange(seq_len, dtype=F32)
    inv_freq = ROPE_BASE ** (-jnp.arange(0, HEAD_DIM, 2, dtype=F32) / HEAD_DIM)
    ang = pos[:, None] * inv_freq[None, :]
    cos_t = jnp.tile(jnp.cos(ang), (1, LANES // (HEAD_DIM // 2)))
    sin_h = jnp.sin(ang)
    sin_t = jnp.tile(jnp.concatenate([-sin_h, sin_h], axis=1), (1, LANES // HEAD_DIM))
    qnw = jnp.tile(diff_q_norm_w.astype(F32), HALF_WIDTH // HEAD_DIM)[None, :]
    knw = jnp.tile(diff_k_norm_w.astype(F32), HALF_WIDTH // HEAD_DIM)[None, :]
    r = jnp.arange(HALF_WIDTH)
    bd = jnp.where((r[:, None] // HEAD_DIM) == (r[None, :] // HEAD_DIM), 1.0 / HEAD_DIM, 0.0).astype(BF16)

    rq, rk, rv, rg, dq, dk, dv = _inproj(x2, scale1, shift1, norm1_w[None, :], w_in_b, cos_t, sin_t, qnw, knw, bd,
                                         seq_len, tt)

    tabs = _retention_tables(ret_a_fwd, ret_a_bwd, c)
    ret_out = _retention(rq, rk, rv, rg, tabs, ret_norm_w.reshape(1, HALF_WIDTH).astype(F32), batch, seq_len, c)

    lam = (jnp.exp(jnp.sum(lam_q1.astype(F32) * lam_k1.astype(F32)))
           - jnp.exp(jnp.sum(lam_q2.astype(F32) * lam_k2.astype(F32))) + lam_init)
    subln = (diff_subln_w.astype(F32) * (1.0 - lam_init))[None, :]
    diff_out = _attention(dq, dk, dv, _bias_tables(rel_bias, t), lam[None].astype(F32), subln, batch, seq_len, t)

    x1, h2, idx_pad, gates_pad, counts = _outproj(ret_out, diff_out, x2, gate1, scale2, shift2, norm2_w[None, :],
                                                  w_out_b, rw_pad, rb_pad, seq_len, tt)
    block_expert, active, dest, dst_prev, fill, n_rows = _routing(idx_pad[:, :TOP_K], idx_pad[:, TOP_K:2 * TOP_K],
                                                                  counts[0, :N_EXPERTS], rows)
    xs = _dispatch(h2, dest, fill, n_rows, ct)
    y4 = _experts(xs, block_expert, active, dst_prev, wgu_b, b_gate_up, wd_b, b_down, TOP_K * batch * seq_len, rows)
    y = _combine(x1, gate2, gates_pad, y4, seq_len, ct)
    return y.reshape(batch, seq_len, d)


def kernel(x_prompt, x_sample, c_prompt, c_sample, w_ada, b_ada, norm1_w, w_in, ret_a_fwd, ret_a_bwd, ret_norm_w,
           diff_q_norm_w, diff_k_norm_w, lam_q1, lam_k1, lam_q2, lam_k2, diff_subln_w, rel_bias, w_out, norm2_w,
           router_w, router_b, w_gate_up, b_gate_up, w_down, b_down):
    depth = w_ada.shape[0]
    d = x_prompt.shape[-1]
    n_prompt = c_prompt.shape[0]
    n_cond = n_prompt + c_sample.shape[0]
    c_all = jnp.concatenate([c_prompt, c_sample], axis=0).astype(F32)
    c_pad = jnp.pad(c_all, ((0, -n_cond % 8), (0, 0)))
    xs = [x_prompt, x_sample]
    for l in range(depth):
        lam_init = 0.8 - 0.6 * math.exp(-0.3 * l)
        mod = _ada(c_pad, w_ada[l], b_ada[l])[:n_cond].reshape(n_cond, 6, d)
        rw = jnp.pad(router_w[l].astype(F32), ((0, 0), (0, LANES - N_EXPERTS)))
        rw_hi = rw.astype(BF16)
        rw_pad = jnp.stack([rw_hi, (rw - rw_hi.astype(F32)).astype(BF16)])
        rb_pad = jnp.pad(router_b[l].astype(F32), (0, LANES - N_EXPERTS), constant_values=-jnp.inf)[None, :]
        shared = (norm1_w[l], w_in[l].astype(BF16), ret_a_fwd[l], ret_a_bwd[l], ret_norm_w[l], diff_q_norm_w[l],
                  diff_k_norm_w[l], lam_q1[l], lam_k1[l], lam_q2[l], lam_k2[l], diff_subln_w[l], rel_bias,
                  w_out[l].astype(BF16), norm2_w[l], rw_pad, rb_pad,
                  w_gate_up[l].astype(BF16), b_gate_up[l][:, None, :], w_down[l].astype(BF16), b_down[l][:, None, :])
        xs = [_trunk(xs[0], mod[:n_prompt], lam_init, *shared),
              _trunk(xs[1], mod[n_prompt:], lam_init, *shared)]
    return (xs[0], xs[1])
```
